```python
import math
import jax
import jax.numpy as jnp
from jax import lax
import numpy as np

D_MODEL = 1024
BATCH = 16
SEQ = 4096
DEPTH = 4

HEAD_DIM = 64
ROPE_THETA = 500000.0
ROPE_FRACTION = 4
Q_BLOCK = 128
DSA_HEADS = 8
DSA_LATENT = 64
IDX_HEADS = 4
IDX_DIM = 32
DSA_TOPK = 256
NSA_HEADS = 8
NSA_KV_HEADS = 1
CMP_BLOCK = 32
CMP_STRIDE = 16
SEL_BLOCK = 32
N_SEL_BLOCKS = 8
WINDOW = 256
FORCED_SCORE = 1.0e4
GDN_HEADS = 4
GDN_DIM = 128
CONV_WIDTH = 4
GDN_CHUNK = 64
BRANCH_WIDTH = 512
N_BRANCHES = 3
GDN_QKV = 3 * GDN_HEADS * GDN_DIM
N_EXPERTS = 64
TOP_K = 8
N_GROUPS = 8
TOPK_GROUPS = 4
D_EXPERT = 256
D_SHARED = 256
ROUTED_SCALE = 2.5
EXPERT_BLOCK = 512
PLE_DIM = 256
LN_EPS = 1e-5
RMS_EPS = 1e-6
DEEPNORM_ALPHA = (2 * DEPTH) ** 0.25
DEEPNORM_BETA = (8 * DEPTH) ** -0.25

IN_SPLITS = (
    ("dsa_q", DSA_HEADS * DSA_LATENT), ("dsa_kv", DSA_LATENT),
    ("idx_q", IDX_HEADS * IDX_DIM), ("idx_k", IDX_DIM), ("idx_w", IDX_HEADS),
    ("nsa_q", NSA_HEADS * HEAD_DIM),
    ("nsa_kc", NSA_KV_HEADS * HEAD_DIM), ("nsa_vc", NSA_KV_HEADS * HEAD_DIM),
    ("nsa_ks", NSA_KV_HEADS * HEAD_DIM), ("nsa_vs", NSA_KV_HEADS * HEAD_DIM),
    ("nsa_kw", NSA_KV_HEADS * HEAD_DIM), ("nsa_vw", NSA_KV_HEADS * HEAD_DIM),
    ("nsa_g", NSA_HEADS * 3),
    ("gdn_qkv", GDN_QKV), ("gdn_a", GDN_HEADS), ("gdn_b", GDN_HEADS), ("gdn_z", GDN_HEADS * GDN_DIM),
    ("merge_g", N_BRANCHES * D_MODEL),
)
IN_WIDTH = sum(s for _, s in IN_SPLITS)

kernel_name = "hybrid_dsa_nsa_gdn_moe_deepnorm"

F32 = jnp.float32


def split_columns(y):
    sizes = [s for _, s in IN_SPLITS]
    parts = jnp.split(y, np.cumsum(sizes)[:-1].tolist(), axis=-1)
    return {name: part for (name, _), part in zip(IN_SPLITS, parts)}


def layer_norm(x, g, b):
    xf = x.astype(F32)
    mu = jnp.mean(xf, axis=-1, keepdims=True)
    var = jnp.mean(jnp.square(xf - mu), axis=-1, keepdims=True)
    return ((xf - mu) * lax.rsqrt(var + LN_EPS) * g.astype(F32) + b.astype(F32)).astype(x.dtype)


def rope_tables(seq, head_dim):
    rot = head_dim // ROPE_FRACTION
    half = rot // 2
    inv_freq = ROPE_THETA ** (-(jnp.arange(half, dtype=F32) * 2.0 / rot))
    ang = jnp.arange(seq, dtype=F32)[:, None] * inv_freq[None, :]
    return jnp.cos(ang), jnp.sin(ang)


def rotate(x, c, s):
    half = c.shape[-1]
    c = c.astype(x.dtype)
    s = s.astype(x.dtype)
    x1, x2 = x[..., :half], x[..., half:2 * half]
    return jnp.concatenate([x1 * c - x2 * s, x2 * c + x1 * s, x[..., 2 * half:]], axis=-1)


def apply_partial_rope(x, cos, sin):
    return rotate(x, cos[:, None, :], sin[:, None, :])


def masked_softmax(logits, mask):
    s = jnp.where(mask, logits.astype(F32), -jnp.inf)
    m = jnp.max(s, axis=-1, keepdims=True)
    m = jnp.where(jnp.isfinite(m), m, 0.0)
    e = jnp.where(mask, jnp.exp(s - m), 0.0)
    return e / jnp.maximum(jnp.sum(e, axis=-1, keepdims=True), 1e-30)


def to_blocks(a):
    b, s = a.shape[:2]
    return jnp.moveaxis(a.reshape((b, s // Q_BLOCK, Q_BLOCK) + a.shape[2:]), 1, 0)


def from_blocks(a):
    a = jnp.moveaxis(a, 0, 1)
    return a.reshape((a.shape[0], a.shape[1] * a.shape[2]) + a.shape[3:])


def dsa_attention(q, c_kv, iq, ik, iw, w_uv, cos, sin):
    bsz, seq = q.shape[:2]
    n_sel = min(DSA_TOPK, seq // 4)
    scale = DSA_LATENT ** -0.5
    ikf = ik.astype(F32)
    iwf = iw.astype(F32) * (IDX_HEADS ** -0.5)
    key_pos = jnp.arange(seq)
    gather = jax.vmap(lambda kk, ii: kk[ii])

    def block(args):
        n, qb, iqb, iwb = args
        t = n * Q_BLOCK + jnp.arange(Q_BLOCK)
        s = jnp.einsum('bqhd,bsd->bqhs', iqb.astype(F32), ikf) * (IDX_DIM ** -0.5)
        score = jnp.einsum('bqhs,bqh->bqs', jax.nn.relu(s), iwb)
        score = jnp.where((key_pos[None, :] <= t[:, None])[None], score, -jnp.inf)
        vals, idx = lax.top_k(score, n_sel)
        valid = jnp.isfinite(vals)
        c_sel = gather(c_kv, idx)
        k_sel = rotate(c_sel, cos[idx], sin[idx])
        logits = jnp.einsum('bqhd,bqkd->bqhk', qb, k_sel) * scale
        pr = masked_softmax(logits, valid[:, :, None, :])
        o_lat = jnp.einsum('bqhk,bqkd->bqhd', pr.astype(c_sel.dtype), c_sel)
        o = jnp.einsum('bqhd,hde->bqhe', o_lat, w_uv)
        return o.reshape(bsz, Q_BLOCK, DSA_HEADS * HEAD_DIM).astype(q.dtype)

    nb = seq // Q_BLOCK
    out = lax.map(block, (jnp.arange(nb), to_blocks(q), to_blocks(iq), to_blocks(iwf)))
    return from_blocks(out)


def nsa_attention(q, kc, vc, ks, vs, kw, vw, gates, w_ck, pe_k, w_cv, pe_v):
    bsz, seq = q.shape[:2]
    grp = NSA_KV_HEADS
    rep = NSA_HEADS // grp
    scale = HEAD_DIM ** -0.5
    n_cmp = (seq - CMP_BLOCK) // CMP_STRIDE + 1
    n_blk = seq // SEL_BLOCK
    n_sel = min(N_SEL_BLOCKS, n_blk)
    win = np.arange(n_cmp)[:, None] * CMP_STRIDE + np.arange(CMP_BLOCK)[None, :]
    k_cmp = jnp.einsum('bjrgd,rde->bjge', kc[:, win] + pe_k[:, None, :], w_ck)
    v_cmp = jnp.einsum('bjrgd,rde->bjge', vc[:, win] + pe_v[:, None, :], w_cv)
    cmp_start = np.arange(n_cmp) * CMP_STRIDE
    cmp_end = jnp.asarray(cmp_start + CMP_BLOCK - 1)
    blk_start = np.arange(n_blk) * SEL_BLOCK
    overlap = jnp.asarray(((cmp_start[:, None] <= blk_start[None, :] + SEL_BLOCK - 1)
                           & (cmp_start[:, None] + CMP_BLOCK - 1 >= blk_start[None, :])).astype(np.float32))
    ks_blocks = jnp.moveaxis(ks.reshape(bsz, n_blk, SEL_BLOCK, grp, HEAD_DIM), 3, 1)
    vs_blocks = jnp.moveaxis(vs.reshape(bsz, n_blk, SEL_BLOCK, grp, HEAD_DIM), 3, 1)
    pad = ((0, 0), (WINDOW, 0), (0, 0), (0, 0))
    kw_pad = jnp.pad(kw, pad)
    vw_pad = jnp.pad(vw, pad)
    g_all = jax.nn.sigmoid(gates.astype(F32))
    blk_ids = jnp.arange(n_blk)
    gather = jax.vmap(jax.vmap(lambda kk, ii: kk[ii]))

    def block(args):
        n, qb, gb = args
        q0 = n * Q_BLOCK
        t = q0 + jnp.arange(Q_BLOCK)
        qg = qb.reshape(bsz, Q_BLOCK, grp, rep, HEAD_DIM)
        lc = jnp.einsum('bqgrd,bjgd->bqgrj', qg, k_cmp) * scale
        pc = masked_softmax(lc, (cmp_end[None, :] <= t[:, None])[None, :, None, None, :])
        o_c = jnp.einsum('bqgrj,bjgd->bqgrd', pc.astype(v_cmp.dtype), v_cmp)
        imp = jnp.einsum('bqgrj,jn->bgqn', pc, overlap)
        cur = t // SEL_BLOCK
        adm = blk_ids[None, :] <= cur[:, None]
        forced = (blk_ids[None, :] == 0) | (blk_ids[None, :] == cur[:, None]) | (blk_ids[None, :] == cur[:, None] - 1)
        imp = jnp.where(adm, jnp.where(forced, FORCED_SCORE, imp), -jnp.inf)
        vals, sel = lax.top_k(imp, n_sel)
        k_sel = gather(ks_blocks, sel).reshape(bsz, grp, Q_BLOCK, n_sel * SEL_BLOCK, HEAD_DIM)
        v_sel = gather(vs_blocks, sel).reshape(bsz, grp, Q_BLOCK, n_sel * SEL_BLOCK, HEAD_DIM)
        tok = sel[..., None] * SEL_BLOCK + jnp.arange(SEL_BLOCK)
        ms = (jnp.isfinite(vals)[..., None] & (tok <= t[None, None, :, None, None]))
        ms = jnp.transpose(ms.reshape(bsz, grp, Q_BLOCK, n_sel * SEL_BLOCK), (0, 2, 1, 3))[:, :, :, None, :]
        ls = jnp.einsum('bqgrd,bgqkd->bqgrk', qg, k_sel) * scale
        ps = masked_softmax(ls, ms)
        o_s = jnp.einsum('bqgrk,bgqkd->bqgrd', ps.astype(v_sel.dtype), v_sel)
        kwin = lax.dynamic_slice_in_dim(kw_pad, q0, Q_BLOCK + WINDOW, axis=1)
        vwin = lax.dynamic_slice_in_dim(vw_pad, q0, Q_BLOCK + WINDOW, axis=1)
        kp = q0 - WINDOW + jnp.arange(Q_BLOCK + WINDOW)
        mw = (kp[None, :] <= t[:, None]) & (kp[None, :] > t[:, None] - WINDOW) & (kp[None, :] >= 0)
        lw = jnp.einsum('bqgrd,bkgd->bqgrk', qg, kwin) * scale
        pw = masked_softmax(lw, mw[None, :, None, None, :])
        o_w = jnp.einsum('bqgrk,bkgd->bqgrd', pw.astype(vwin.dtype), vwin)
        g = gb.reshape(bsz, Q_BLOCK, grp, rep, 3)
        o = g[..., 0:1] * o_c + g[..., 1:2] * o_s + g[..., 2:3] * o_w
        return o.reshape(bsz, Q_BLOCK, NSA_HEADS * HEAD_DIM).astype(q.dtype)

    nb = seq // Q_BLOCK
    out = lax.map(block, (jnp.arange(nb), to_blocks(q), to_blocks(g_all)))
    return from_blocks(out)


def causal_depthwise_conv(x, w):
    return lax.conv_general_dilated(x, w[:, None, :].astype(x.dtype), window_strides=(1,),
                                    padding=[(CONV_WIDTH - 1, 0)], dimension_numbers=('NWC', 'WIO', 'NWC'),
                                    feature_group_count=x.shape[-1])


def l2_normalize(x):
    return x * lax.rsqrt(jnp.sum(jnp.square(x), axis=-1, keepdims=True) + RMS_EPS)


def gated_delta_rule(q, k, v, g, beta):
    bsz, seq, nh, dk = q.shape
    dv = v.shape[-1]
    nc = seq // GDN_CHUNK

    def chunk(a):
        return jnp.moveaxis(a.reshape((bsz, nc, GDN_CHUNK, nh) + a.shape[3:]), 3, 1)

    q, k, v, g, beta = chunk(q), chunk(k), chunk(v), chunk(g), chunk(beta)
    g = jnp.cumsum(g, axis=-1)
    tri = jnp.tril(jnp.ones((GDN_CHUNK, GDN_CHUNK), bool))
    strict = jnp.tril(jnp.ones((GDN_CHUNK, GDN_CHUNK), bool), -1)
    decay = jnp.exp(jnp.where(tri, g[..., :, None] - g[..., None, :], -jnp.inf))
    kb = k * beta[..., None]
    lower = jnp.where(strict, jnp.einsum('bhncd,bhnsd->bhncs', kb, k) * decay, 0.0)
    a_mat = lower + jnp.eye(GDN_CHUNK, dtype=F32)
    rhs = jnp.concatenate([v * beta[..., None], kb * jnp.exp(g)[..., None]], axis=-1)
    sol = lax.linalg.triangular_solve(a_mat, rhs, left_side=True, lower=True, unit_diagonal=True)
    u, w = sol[..., :dv], sol[..., dv:]
    attn_qk = jnp.where(tri, jnp.einsum('bhncd,bhnsd->bhncs', q, k) * decay, 0.0)
    q_dec = q * jnp.exp(g)[..., None]
    k_dec = k * jnp.exp(g[..., -1:] - g)[..., None]
    g_last = jnp.exp(g[..., -1])

    def step(state, xs):
        qd, kd, uc, wc, ac, gl = xs
        v_new = uc - jnp.einsum('bhcd,bhde->bhce', wc, state)
        o = jnp.einsum('bhcd,bhde->bhce', qd, state) + jnp.einsum('bhcs,bhse->bhce', ac, v_new)
        state = state * gl[..., None, None] + jnp.einsum('bhcd,bhce->bhde', kd, v_new)
        return state, o

    xs = tuple(jnp.moveaxis(a, 2, 0) for a in (q_dec, k_dec, u, w, attn_qk, g_last))
    s0 = jnp.zeros((bsz, nh, dk, dv), F32)
    _, o = lax.scan(step, s0, xs)
    o = jnp.moveaxis(o, 0, 2)
    return jnp.moveaxis(o, 1, 3).reshape(bsz, seq, nh, dv)


def gdn_mixer(qkv, a_raw, b_raw, z, conv_w, a_log, dt_bias, norm_g):
    bsz, seq = qkv.shape[:2]
    xc = jax.nn.silu(causal_depthwise_conv(qkv, conv_w)).astype(F32)
    q, k, v = jnp.split(xc, 3, axis=-1)
    shp = (bsz, seq, GDN_HEADS, GDN_DIM)
    q = l2_normalize(q.reshape(shp)) * (GDN_DIM ** -0.5)
    k = l2_normalize(k.reshape(shp))
    v = v.reshape(shp)
    beta = jax.nn.sigmoid(b_raw.astype(F32))
    g = -jnp.exp(a_log.astype(F32)) * jax.nn.softplus(a_raw.astype(F32) + dt_bias.astype(F32))
    o = gated_delta_rule(q, k, v, g, beta)
    o = o * lax.rsqrt(jnp.mean(jnp.square(o), axis=-1, keepdims=True) + RMS_EPS) * norm_g.astype(F32)
    o = o * jax.nn.silu(z.reshape(shp).astype(F32))
    return o.reshape(bsz, seq, GDN_HEADS * GDN_DIM).astype(qkv.dtype)


def moe_ffn(h, w_router, router_bias, w_gate_up, w_down, w_sh_gate_up, w_sh_down):
    bsz, seq, d = h.shape
    xt = h.reshape(-1, d)
    n_tok = xt.shape[0]
    scores = jax.nn.sigmoid((xt @ w_router).astype(F32))
    biased = scores + router_bias.astype(F32)
    per_group = N_EXPERTS // N_GROUPS
    grp_score = jnp.sum(lax.top_k(biased.reshape(n_tok, N_GROUPS, per_group), 2)[0], axis=-1)
    _, top_grp = lax.top_k(grp_score, TOPK_GROUPS)
    grp_mask = jnp.any(top_grp[..., None] == jnp.arange(N_GROUPS), axis=1)
    expert_mask = jnp.repeat(grp_mask, per_group, axis=1)
    _, eidx = lax.top_k(jnp.where(expert_mask, biased, -jnp.inf), TOP_K)
    gate = jnp.take_along_axis(scores, eidx, axis=1)
    gate = gate / jnp.sum(gate, axis=-1, keepdims=True) * ROUTED_SCALE
    n_rows = n_tok * TOP_K
    n_blocks = -(-n_rows // EXPERT_BLOCK) + N_EXPERTS
    n_pad = n_blocks * EXPERT_BLOCK
    flat_e = eidx.reshape(-1)
    order = jnp.argsort(flat_e)
    sorted_e = flat_e[order]
    counts = jnp.bincount(flat_e, length=N_EXPERTS).astype(jnp.int32)
    padded = (counts + EXPERT_BLOCK - 1) // EXPERT_BLOCK * EXPERT_BLOCK
    start = jnp.cumsum(counts) - counts
    pad_end = jnp.cumsum(padded)
    pad_start = pad_end - padded
    dest = pad_start[sorted_e] + jnp.arange(n_rows, dtype=jnp.int32) - start[sorted_e]
    row_tok = jnp.full((n_pad,), n_tok, jnp.int32).at[dest].set((order // TOP_K).astype(jnp.int32))
    row_gate = jnp.zeros((n_pad,), xt.dtype).at[dest].set(gate.reshape(-1)[order].astype(xt.dtype))
    blk_e = jnp.minimum(jnp.searchsorted(pad_end, jnp.arange(n_blocks, dtype=jnp.int32) * EXPERT_BLOCK,
                                         side='right'), N_EXPERTS - 1)
    x_pad = jnp.concatenate([xt, jnp.zeros((1, d), xt.dtype)], axis=0)
    xb = x_pad[row_tok].reshape(n_blocks, EXPERT_BLOCK, d)
    gu = jnp.einsum('nmd,ndf->nmf', xb, w_gate_up[blk_e])
    gg, uu = jnp.split(gu, 2, axis=-1)
    ye = jnp.einsum('nmf,nfd->nmd', jax.nn.silu(gg) * uu, w_down[blk_e]).reshape(n_pad, d)
    ye = ye * row_gate[:, None]
    routed = jnp.zeros((n_tok + 1, d), ye.dtype).at[row_tok].add(ye)[:n_tok]
    sg, su = jnp.split(xt @ w_sh_gate_up, 2, axis=-1)
    shared = (jax.nn.silu(sg) * su) @ w_sh_down
    return (routed + shared).reshape(bsz, seq, d)


def setup_inputs(seed: int = 0) -> dict:
    key = jax.random.key(seed)
    ks = jax.random.split(key, 26)
    L, D, E = DEPTH, D_MODEL, N_EXPERTS

    def nrm(k, shape, scale):
        return jax.random.normal(k, shape, F32) * scale

    dt = jnp.exp(jax.random.uniform(ks[9], (L, GDN_HEADS), F32, math.log(1e-3), math.log(1e-1)))
    return {
        "x": nrm(ks[0], (BATCH, SEQ, D), 1.0),
        "p": nrm(ks[1], (DEPTH, BATCH, SEQ, PLE_DIM), 1.0),
        "w_in": nrm(ks[2], (L, D, IN_WIDTH), D ** -0.5),
        "w_dsa_uv": nrm(ks[25], (L, DSA_HEADS, DSA_LATENT, HEAD_DIM), DSA_LATENT ** -0.5),
        "w_cmp_k": nrm(ks[3], (L, CMP_BLOCK, HEAD_DIM, HEAD_DIM), (CMP_BLOCK * HEAD_DIM) ** -0.5),
        "pe_cmp_k": nrm(ks[4], (L, CMP_BLOCK, HEAD_DIM), 0.02),
        "w_cmp_v": nrm(ks[5], (L, CMP_BLOCK, HEAD_DIM, HEAD_DIM), (CMP_BLOCK * HEAD_DIM) ** -0.5),
        "pe_cmp_v": nrm(ks[6], (L, CMP_BLOCK, HEAD_DIM), 0.02),
        "conv_w": nrm(ks[7], (L, CONV_WIDTH, GDN_QKV), CONV_WIDTH ** -0.5),
        "a_log": jnp.log(jax.random.uniform(ks[8], (L, GDN_HEADS), F32, 1.0, 16.0)),
        "dt_bias": dt + jnp.log(-jnp.expm1(-dt)),
        "gdn_norm": 1.0 + nrm(ks[10], (L, GDN_DIM), 0.02),
        "w_branch": nrm(ks[11], (L, N_BRANCHES, BRANCH_WIDTH, D), BRANCH_WIDTH ** -0.5),
        "w_out": nrm(ks[12], (L, D, D), D ** -0.5 * DEEPNORM_BETA),
        "ln1_g": 1.0 + nrm(ks[13], (L, D), 0.02),
        "ln1_b": nrm(ks[14], (L, D), 0.02),
        "w_router": nrm(ks[15], (L, D, E), D ** -0.5),
        "router_bias": nrm(ks[16], (L, E), 0.01),
        "w_gate_up": nrm(ks[17], (L, E, D, 2 * D_EXPERT), D ** -0.5),
        "w_down": nrm(ks[18], (L, E, D_EXPERT, D), D_EXPERT ** -0.5 * DEEPNORM_BETA),
        "w_sh_gate_up": nrm(ks[19], (L, D, 2 * D_SHARED), D ** -0.5),
        "w_sh_down": nrm(ks[20], (L, D_SHARED, D), D_SHARED ** -0.5 * DEEPNORM_BETA),
        "w_ple_proj": nrm(ks[21], (L, PLE_DIM, D), PLE_DIM ** -0.5 * DEEPNORM_BETA),
        "w_ple_gate": nrm(ks[22], (L, D, D), D ** -0.5),
        "ln2_g": 1.0 + nrm(ks[23], (L, D), 0.02),
        "ln2_b": nrm(ks[24], (L, D), 0.02),
    }


def reference(x, p, w_in, w_dsa_uv, w_cmp_k, pe_cmp_k, w_cmp_v, pe_cmp_v, conv_w, a_log, dt_bias, gdn_norm,
              w_branch, w_out, ln1_g, ln1_b, w_router, router_bias, w_gate_up, w_down,
              w_sh_gate_up, w_sh_down, w_ple_proj, w_ple_gate, ln2_g, ln2_b):
    bsz, seq, _ = x.shape
    cos_h, sin_h = rope_tables(seq, HEAD_DIM)
    cos_l, sin_l = rope_tables(seq, DSA_LATENT)
    cos_i, sin_i = rope_tables(seq, IDX_DIM)
    kv_shape = (bsz, seq, NSA_KV_HEADS, HEAD_DIM)
    for i in range(DEPTH):
        h = x
        c = split_columns(h @ w_in[i])
        dq = apply_partial_rope(c["dsa_q"].reshape(bsz, seq, DSA_HEADS, DSA_LATENT), cos_l, sin_l)
        iq = apply_partial_rope(c["idx_q"].reshape(bsz, seq, IDX_HEADS, IDX_DIM), cos_i, sin_i)
        ik = apply_partial_rope(c["idx_k"][:, :, None, :], cos_i, sin_i)[:, :, 0, :]
        y_a = dsa_attention(dq, c["dsa_kv"], iq, ik, c["idx_w"], w_dsa_uv[i], cos_l, sin_l)
        nq = apply_partial_rope(c["nsa_q"].reshape(bsz, seq, NSA_HEADS, HEAD_DIM), cos_h, sin_h)
        nkc = apply_partial_rope(c["nsa_kc"].reshape(kv_shape), cos_h, sin_h)
        nks = apply_partial_rope(c["nsa_ks"].reshape(kv_shape), cos_h, sin_h)
        nkw = apply_partial_rope(c["nsa_kw"].reshape(kv_shape), cos_h, sin_h)
        y_b = nsa_attention(nq, nkc, c["nsa_vc"].reshape(kv_shape), nks, c["nsa_vs"].reshape(kv_shape),
                            nkw, c["nsa_vw"].reshape(kv_shape), c["nsa_g"].reshape(bsz, seq, NSA_HEADS, 3),
                            w_cmp_k[i], pe_cmp_k[i], w_cmp_v[i], pe_cmp_v[i])
        y_c = gdn_mixer(c["gdn_qkv"], c["gdn_a"], c["gdn_b"], c["gdn_z"], conv_w[i], a_log[i], dt_bias[i], gdn_norm[i])
        mg = jax.nn.sigmoid(c["merge_g"].astype(F32)).reshape(bsz, seq, N_BRANCHES, D_MODEL).astype(x.dtype)
        merged = (mg[:, :, 0] * (y_a @ w_branch[i, 0])
                  + mg[:, :, 1] * (y_b @ w_branch[i, 1])
                  + mg[:, :, 2] * (y_c @ w_branch[i, 2]))
        x = layer_norm(DEEPNORM_ALPHA * x + merged @ w_out[i], ln1_g[i], ln1_b[i])
        ff = moe_ffn(x, w_router[i], router_bias[i], w_gate_up[i], w_down[i], w_sh_gate_up[i], w_sh_down[i])
        ple = jax.nn.sigmoid(x @ w_ple_gate[i]) * (p[i] @ w_ple_proj[i])
        x = layer_norm(DEEPNORM_ALPHA * x + ff + ple, ln2_g[i], ln2_b[i])
    return x
```

```python
import functools
import math

import numpy as np
import jax
import jax.numpy as jnp
from jax import lax
from jax.experimental import pallas as pl
from jax.experimental.pallas import tpu as pltpu

F32 = jnp.float32
BF16 = jnp.bfloat16
I32 = jnp.int32

HEAD_DIM = 64
ROPE_THETA = 500000.0
ROPE_FRACTION = 4
DSA_HEADS = 8
DSA_LATENT = 64
IDX_HEADS = 4
IDX_DIM = 32
DSA_TOPK = 256
NSA_HEADS = 8
CMP_BLOCK = 32
CMP_STRIDE = 16
SEL_BLOCK = 32
N_SEL_BLOCKS = 8
WINDOW = 256
FORCED_SCORE = 1.0e4
GDN_HEADS = 4
GDN_DIM = 128
CONV_WIDTH = 4
GDN_CHUNK = 64
BRANCH_WIDTH = 512
N_BRANCHES = 3
GDN_QKV = 3 * GDN_HEADS * GDN_DIM
N_EXPERTS = 64
TOP_K = 8
N_GROUPS = 8
TOPK_GROUPS = 4
D_EXPERT = 256
D_SHARED = 256
ROUTED_SCALE = 2.5
EXPERT_BLOCK = 512
PLE_DIM = 256
LN_EPS = 1e-5
RMS_EPS = 1e-6

IN_SPLITS = (
    ("dsa_q", DSA_HEADS * DSA_LATENT), ("dsa_kv", DSA_LATENT),
    ("idx_q", IDX_HEADS * IDX_DIM), ("idx_k", IDX_DIM), ("idx_w", IDX_HEADS),
    ("nsa_q", NSA_HEADS * HEAD_DIM),
    ("nsa_kc", HEAD_DIM), ("nsa_vc", HEAD_DIM),
    ("nsa_ks", HEAD_DIM), ("nsa_vs", HEAD_DIM),
    ("nsa_kw", HEAD_DIM), ("nsa_vw", HEAD_DIM),
    ("nsa_g", NSA_HEADS * 3),
    ("gdn_qkv", GDN_QKV), ("gdn_a", GDN_HEADS), ("gdn_b", GDN_HEADS), ("gdn_z", GDN_HEADS * GDN_DIM),
    ("merge_g", N_BRANCHES * 1024),
)

LANES = 128
Q_TILE = 128
KEY_CHUNK = 256
VMEM_LIMIT = 56 * 1024 * 1024
INT_MIN = -2 ** 31
NEG_BIG = -1e30


def _cparams(*sem):
    return pltpu.CompilerParams(dimension_semantics=sem, vmem_limit_bytes=VMEM_LIMIT)


def _rope_lane_tables(seq, head_dim):
    rot = head_dim // ROPE_FRACTION
    half = rot // 2
    inv_freq = ROPE_THETA ** (-(jnp.arange(half, dtype=F32) * 2.0 / rot))
    ang = jnp.arange(seq, dtype=F32)[:, None] * inv_freq[None, :]
    cos, sin = jnp.cos(ang), jnp.sin(ang)
    one = jnp.ones((seq, head_dim - rot), F32)
    c = jnp.concatenate([cos, cos, one], axis=-1)
    sa = jnp.concatenate([-sin, jnp.zeros((seq, head_dim - half), F32)], axis=-1)
    sb = jnp.concatenate([jnp.zeros((seq, half), F32), sin, jnp.zeros((seq, head_dim - rot), F32)], axis=-1)
    rep = LANES // head_dim
    return tuple(jnp.tile(a, (1, rep)) for a in (c, sa, sb)), half


def _rope_apply(y, c, sa, sb, half):
    n = y.shape[-1]
    rep = n // LANES
    if rep > 1:
        c, sa, sb = (jnp.concatenate([a] * rep, axis=-1) for a in (c, sa, sb))
    up = pltpu.roll(y, n - half, 1)
    dn = pltpu.roll(y, half, 1)
    return y * c + up * sa + dn * sb


IN_TILE = 256


def _inproj_body(x_ref, w64_ref, w32_ref, wpl_ref, wvt_ref, wmg_ref,
                 c64_ref, sa64_ref, sb64_ref, c32_ref, sa32_ref, sb32_ref,
                 dq_ref, nq_ref, krot_ref, nkc_ref, nks_ref, nkw_ref, iq_ref, ik_ref,
                 nvc_ref, misc_ref, gz_ref, gqkv_ref, vt_ref, mg_ref):
    xb = x_ref[...].astype(BF16)
    c64, sa64, sb64 = c64_ref[...], sa64_ref[...], sb64_ref[...]
    half64 = HEAD_DIM // ROPE_FRACTION // 2
    half32 = IDX_DIM // ROPE_FRACTION // 2

    def proj(w_ref, lo, hi):
        return jnp.dot(xb, w_ref[:, lo:hi], preferred_element_type=F32)

    dq_ref[...] = _rope_apply(proj(w64_ref, 0, 512), c64, sa64, sb64, half64).astype(BF16)
    nq_ref[...] = _rope_apply(proj(w64_ref, 512, 1024), c64, sa64, sb64, half64).astype(BF16)
    k1 = _rope_apply(proj(w64_ref, 1024, 1152), c64, sa64, sb64, half64).astype(BF16)
    krot_ref[...] = k1[:, :64]
    nkc_ref[...] = k1[:, 64:]
    k2 = _rope_apply(proj(w64_ref, 1152, 1280), c64, sa64, sb64, half64).astype(BF16)
    nks_ref[...] = k2[:, :64]
    nkw_ref[...] = k2[:, 64:]
    c32, sa32, sb32 = c32_ref[...], sa32_ref[...], sb32_ref[...]
    iq_ref[...] = _rope_apply(proj(w32_ref, 0, 128), c32, sa32, sb32, half32).astype(BF16)
    ik = _rope_apply(proj(w32_ref, 128, 256), c32, sa32, sb32, half32).astype(BF16)
    ik_ref[...] = ik[:, :IDX_DIM]
    pl0 = proj(wpl_ref, 0, 128)
    nvc_ref[...] = pl0[:, :64].astype(BF16)
    misc_ref[...] = proj(wpl_ref, 128, 256)
    gz_ref[...] = proj(wpl_ref, 256, 768)
    for j in range(3):
        gqkv_ref[:, j * 512:(j + 1) * 512] = proj(wpl_ref, 768 + j * 512, 768 + (j + 1) * 512)
    vt_ref[...] = lax.dot_general(wvt_ref[...], xb, (((1,), (1,)), ((), ())),
                                  preferred_element_type=F32).astype(BF16)
    for j in range(6):
        g = proj(wmg_ref, j * 512, (j + 1) * 512)
        mg_ref[:, j * 512:(j + 1) * 512] = jax.nn.sigmoid(g)


def _split_w_in(w):
    cols = {}
    off = 0
    for name, size in IN_SPLITS:
        cols[name] = w[:, off:off + size]
        off += size
    d = w.shape[0]
    z = lambda n: jnp.zeros((d, n), w.dtype)
    w64 = jnp.concatenate([cols["dsa_q"], cols["nsa_q"], cols["dsa_kv"], cols["nsa_kc"],
                           cols["nsa_ks"], cols["nsa_kw"]], axis=1)
    w32 = jnp.concatenate([cols["idx_q"], cols["idx_k"], z(LANES - IDX_DIM)], axis=1)
    misc = jnp.concatenate([cols["idx_w"], cols["gdn_a"], cols["gdn_b"], cols["nsa_g"]], axis=1)
    misc = jnp.concatenate([misc, z(LANES - misc.shape[1])], axis=1)
    wpl = jnp.concatenate([cols["nsa_vc"], z(64), misc, cols["gdn_z"], cols["gdn_qkv"]], axis=1)
    wvt = jnp.concatenate([cols["dsa_kv"], cols["nsa_vs"], cols["nsa_vw"], z(64)], axis=1).T
    return [a.astype(BF16) for a in (w64, w32, wpl, wvt, cols["merge_g"])]


MISC_IW = 0
MISC_GA = IDX_HEADS
MISC_GB = MISC_GA + GDN_HEADS
MISC_NG = MISC_GB + GDN_HEADS


def _in_projection(x2d, wparts, tabs64, tabs32, seq):
    t, d = x2d.shape
    tm = IN_TILE
    nt = seq // tm
    w64, w32, wpl, wvt, wmg = wparts
    row = lambda w: pl.BlockSpec((tm, w), lambda i: (i, 0))
    full = lambda a: pl.BlockSpec(a.shape, lambda i: (0, 0))
    tab = pl.BlockSpec((tm, LANES), lambda i: (i % nt, 0))
    outs = [
        ("dq", 512, BF16), ("nq", 512, BF16), ("krot", 64, BF16), ("nkc", 64, BF16), ("nks", 64, BF16),
        ("nkw", 64, BF16), ("iq", 128, BF16), ("ik", IDX_DIM, BF16), ("nvc", 64, BF16),
        ("misc", 128, F32), ("gz", 512, F32), ("gqkv", GDN_QKV, F32),
    ]
    out_shape = [jax.ShapeDtypeStruct((t, w), dt) for _, w, dt in outs]
    out_specs = [row(w) for _, w, _ in outs]
    out_shape += [jax.ShapeDtypeStruct((256, t), BF16), jax.ShapeDtypeStruct((t, 3 * d), F32)]
    out_specs += [pl.BlockSpec((256, tm), lambda i: (0, i)), row(3 * d)]
    res = pl.pallas_call(
        _inproj_body,
        grid=(t // tm,),
        in_specs=[row(d), full(w64), full(w32), full(wpl), full(wvt), full(wmg)] + [tab] * 6,
        out_specs=out_specs,
        out_shape=out_shape,
        compiler_params=_cparams("parallel"),
        name="in_projection",
    )(x2d, w64, w32, wpl, wvt, wmg, *tabs64, *tabs32)
    names = [n for n, _, _ in outs] + ["vt", "mg"]
    return dict(zip(names, res))


_NT = (((1,), (1,)), ((), ()))


def _stack_heads(q, n_heads, width):
    return jnp.concatenate([q[:, h * width:(h + 1) * width] for h in range(n_heads)], axis=0)


def _unstack_heads_t(o_t, n_heads):
    q = o_t.shape[1] // n_heads
    rows = jnp.concatenate([o_t[:, h * q:(h + 1) * q] for h in range(n_heads)], axis=0)
    return rows.T


def _flash_chunks(qs, k_ref, vt_ref, lo, hi, rows, mask_fn, carry0, acc_ref, m_ref, l_ref, n_heads):
    m_ref[...] = jnp.full(m_ref.shape, NEG_BIG, F32)
    l_ref[...] = jnp.zeros(l_ref.shape, F32)
    acc_ref[...] = jnp.zeros(acc_ref.shape, F32)

    def body(c, carry):
        r0 = pl.multiple_of(c * rows, rows)
        bias, carry = mask_fn(r0, carry)
        kc = k_ref[pl.ds(r0, rows), :]
        s = lax.dot_general(kc, qs, _NT, preferred_element_type=F32)
        s = s + jnp.concatenate([bias] * n_heads, axis=1)
        m_old = m_ref[...]
        m_new = jnp.maximum(m_old, jnp.max(s, axis=0, keepdims=True))
        alpha = jnp.exp(m_old - m_new)
        p = jnp.exp(s - m_new)
        l_ref[...] = alpha * l_ref[...] + jnp.sum(p, axis=0, keepdims=True)
        pv = jnp.dot(vt_ref[:, pl.ds(r0, rows)], p.astype(BF16), preferred_element_type=F32)
        acc_ref[...] = acc_ref[...] * alpha + pv
        m_ref[...] = m_new
        return carry

    lax.fori_loop(lo, hi, body, carry0)
    return acc_ref[...] / jnp.maximum(l_ref[...], 1e-30)


def _dsa_body(iq_ref, dq_ref, misc_ref, ik_ref, krot_ref, vt_ref, wuvt_ref, o_ref,
              keys_ref, acc_ref, m_ref, l_ref, *, n_sel):
    rows = KEY_CHUNK
    q0 = pl.program_id(1) * Q_TILE
    nch = (q0 + Q_TILE + rows - 1) // rows
    tq = q0 + lax.broadcasted_iota(I32, (1, Q_TILE), 1)
    misc_t = misc_ref[...].T
    coef = (IDX_HEADS ** -0.5) * (IDX_DIM ** -0.5)
    iq = iq_ref[...]
    int_min = jnp.int32(INT_MIN)

    def score_chunk(c, _):
        r0 = pl.multiple_of(c * rows, rows)
        ikc = ik_ref[pl.ds(r0, rows), :]
        acc = jnp.zeros((rows, Q_TILE), F32)
        for h in range(IDX_HEADS):
            s = lax.dot_general(ikc, iq[:, h * IDX_DIM:(h + 1) * IDX_DIM], _NT, preferred_element_type=F32)
            acc = acc + jnp.maximum(s, 0.0) * (misc_t[MISC_IW + h:MISC_IW + h + 1, :] * coef)
        acc = jnp.where(acc == 0.0, 0.0, acc)
        bits = pltpu.bitcast(acc, I32)
        key = bits ^ ((bits >> 31) & jnp.int32(0x7FFFFFFF))
        kpos = r0 + lax.broadcasted_iota(I32, (rows, Q_TILE), 0)
        keys_ref[pl.ds(r0, rows), :] = jnp.where(kpos <= tq, key, int_min)
        return 0

    lax.fori_loop(0, nch, score_chunk, 0)

    def count(pred):
        def body(c, acc):
            r0 = pl.multiple_of(c * rows, rows)
            hit = pred(keys_ref[pl.ds(r0, rows), :])
            return acc + jnp.sum(jnp.where(hit, 1, 0).astype(I32), axis=0, keepdims=True)
        return lax.fori_loop(0, nch, body, jnp.zeros((1, Q_TILE), I32))

    kq = jnp.minimum(tq + 1, n_sel)

    def bit_step(i, tu):
        cand_u = tu | lax.shift_left(jnp.int32(1), 31 - i)
        cand = cand_u ^ int_min
        cnt = count(lambda kk: kk >= cand)
        return jnp.where(cnt >= kq, cand_u, tu)

    thr = lax.fori_loop(0, 32, bit_step, jnp.zeros((1, Q_TILE), I32)) ^ int_min
    need = (kq - count(lambda kk: kk > thr)).astype(F32)

    ri = lax.broadcasted_iota(I32, (rows, rows), 0)
    ci = lax.broadcasted_iota(I32, (rows, rows), 1)
    tri = jnp.where(ri > ci, 1.0, 0.0).astype(BF16)

    def mask_fn(r0, seen):
        kk = keys_ref[pl.ds(r0, rows), :]
        tie = jnp.where(kk == thr, 1.0, 0.0)
        before = jnp.dot(tri, tie.astype(BF16), preferred_element_type=F32) + seen
        take = (kk > thr) | ((kk == thr) & (before < need))
        return jnp.where(take, 0.0, -jnp.inf), seen + jnp.sum(tie, axis=0, keepdims=True)

    qs = _stack_heads(dq_ref[...], DSA_HEADS, DSA_LATENT) * jnp.asarray(DSA_LATENT ** -0.5, BF16)
    o_lat = _flash_chunks(qs, krot_ref, vt_ref, 0, nch, rows, mask_fn, jnp.zeros((1, Q_TILE), F32),
                          acc_ref, m_ref, l_ref, DSA_HEADS)
    outs = []
    for h in range(DSA_HEADS):
        oh = o_lat[:, h * Q_TILE:(h + 1) * Q_TILE].astype(BF16)
        outs.append(jnp.dot(wuvt_ref[h], oh, preferred_element_type=F32))
    o_ref[...] = jnp.concatenate(outs, axis=0).T.astype(o_ref.dtype)


def _dsa_attention(c, w_uv, bsz, seq):
    t = bsz * seq
    nq = seq // Q_TILE
    n_sel = min(DSA_TOPK, seq // 4)
    wuvt = jnp.swapaxes(w_uv, 1, 2).astype(BF16)
    qrow = lambda w: pl.BlockSpec((Q_TILE, w), lambda b, n: (b * nq + n, 0))
    seqrow = lambda w: pl.BlockSpec((seq, w), lambda b, n: (b, 0))
    nqh = DSA_HEADS * Q_TILE
    return pl.pallas_call(
        functools.partial(_dsa_body, n_sel=n_sel),
        grid=(bsz, nq),
        in_specs=[qrow(128), qrow(512), qrow(128), seqrow(IDX_DIM), seqrow(DSA_LATENT),
                  pl.BlockSpec((DSA_LATENT, seq), lambda b, n: (0, b)),
                  pl.BlockSpec(wuvt.shape, lambda b, n: (0, 0, 0))],
        out_specs=qrow(512),
        out_shape=jax.ShapeDtypeStruct((t, 512), BF16),
        scratch_shapes=[pltpu.VMEM((seq, Q_TILE), I32), pltpu.VMEM((DSA_LATENT, nqh), F32),
                        pltpu.VMEM((1, nqh), F32), pltpu.VMEM((1, nqh), F32)],
        compiler_params=_cparams("parallel", "arbitrary"),
        name="dsa_attention",
    )(c["iq"], c["dq"], c["misc"], c["ik"], c["krot"], c["vt"], wuvt)


CMP_PER_ROW = CMP_STRIDE * HEAD_DIM


def _nsa_cmp_body(kc_ref, vc_ref, wk_ref, wv_ref, pek_ref, pev_ref, kcmp_ref, vcmpt_ref):
    n = kc_ref.shape[0]

    def compress(x_ref, w_ref, pe_ref):
        xf = x_ref[...].astype(F32)
        lo = jnp.dot((xf + pe_ref[0:1, :]).astype(BF16), w_ref[0], preferred_element_type=F32)
        hi = jnp.dot((xf + pe_ref[1:2, :]).astype(BF16), w_ref[1], preferred_element_type=F32)
        return lo + pltpu.roll(hi, n - 1, 0)

    kcmp_ref[...] = compress(kc_ref, wk_ref, pek_ref).astype(BF16)
    vc = compress(vc_ref, wv_ref, pev_ref)
    vpad = jnp.concatenate([vc, jnp.zeros_like(vc)], axis=1)
    vcmpt_ref[...] = vpad.T[:HEAD_DIM, :].astype(BF16)


def _nsa_compress(c, w_ck, pe_k, w_cv, pe_v, bsz, seq):
    ng = seq // CMP_STRIDE
    kc2 = c["nkc"].reshape(bsz * ng, CMP_PER_ROW)
    vc2 = c["nvc"].reshape(bsz * ng, CMP_PER_ROW)
    wk = w_ck.reshape(2, CMP_PER_ROW, HEAD_DIM).astype(BF16)
    wv = w_cv.reshape(2, CMP_PER_ROW, HEAD_DIM).astype(BF16)
    pek = pe_k.reshape(2, CMP_PER_ROW)
    pev = pe_v.reshape(2, CMP_PER_ROW)
    grp = pl.BlockSpec((ng, CMP_PER_ROW), lambda b: (b, 0))
    wsp = pl.BlockSpec((2, CMP_PER_ROW, HEAD_DIM), lambda b: (0, 0, 0))
    psp = pl.BlockSpec((2, CMP_PER_ROW), lambda b: (0, 0))
    return pl.pallas_call(
        _nsa_cmp_body,
        grid=(bsz,),
        in_specs=[grp, grp, wsp, wsp, psp, psp],
        out_specs=[pl.BlockSpec((ng, HEAD_DIM), lambda b: (b, 0)), pl.BlockSpec((HEAD_DIM, ng), lambda b: (0, b))],
        out_shape=[jax.ShapeDtypeStruct((bsz * ng, HEAD_DIM), BF16), jax.ShapeDtypeStruct((HEAD_DIM, bsz * ng), BF16)],
        compiler_params=_cparams("parallel"),
        name="nsa_compress",
    )(kc2, vc2, wk, wv, pek, pev)


def _nsa_body(nq_ref, misc_ref, kcmp_ref, vcmpt_ref, ovl_ref, exp_ref, ks_ref, vst_ref, kw_ref, vwt_ref,
              o_ref, sel_ref, acc_ref, m_ref, l_ref, *, n_sel, n_cmp):
    rows = KEY_CHUNK
    n = pl.program_id(1)
    q0 = n * Q_TILE
    tq = q0 + lax.broadcasted_iota(I32, (1, Q_TILE), 1)
    qs = _stack_heads(nq_ref[...], NSA_HEADS, HEAD_DIM) * jnp.asarray(HEAD_DIM ** -0.5, BF16)
    gate_t = jax.nn.sigmoid(misc_ref[...]).T

    def gate_row(branch):
        return jnp.concatenate([gate_t[MISC_NG + 3 * h + branch:MISC_NG + 3 * h + branch + 1, :]
                                for h in range(NSA_HEADS)], axis=1)

    ng = kcmp_ref.shape[0]
    jrow = lax.broadcasted_iota(I32, (ng, Q_TILE), 0)
    ok_c = (jrow * CMP_STRIDE + (CMP_BLOCK - 1) <= tq) & (jrow < n_cmp)
    bias_c = jnp.where(ok_c, 0.0, -jnp.inf)
    s = lax.dot_general(kcmp_ref[...], qs, _NT, preferred_element_type=F32)
    s = s + jnp.concatenate([bias_c] * NSA_HEADS, axis=1)
    mx = jnp.maximum(jnp.max(s, axis=0, keepdims=True), NEG_BIG)
    e = jnp.exp(s - mx)
    pc = (e / jnp.maximum(jnp.sum(e, axis=0, keepdims=True), 1e-30)).astype(BF16)
    out = gate_row(0) * jnp.dot(vcmpt_ref[...], pc, preferred_element_type=F32)

    imp8 = jnp.dot(ovl_ref[...], pc, preferred_element_type=F32)
    imp = imp8[:, 0:Q_TILE]
    for h in range(1, NSA_HEADS):
        imp = imp + imp8[:, h * Q_TILE:(h + 1) * Q_TILE]
    n_blk = imp.shape[0]
    blk = lax.broadcasted_iota(I32, (n_blk, Q_TILE), 0)
    cur = tq // SEL_BLOCK
    forced = (blk == 0) | (blk == cur) | (blk == cur - 1)
    imp = jnp.where(blk <= cur, jnp.where(forced, FORCED_SCORE, imp), -jnp.inf)
    chosen = jnp.zeros((n_blk, Q_TILE), F32)
    for _ in range(n_sel):
        top = jnp.max(imp, axis=0, keepdims=True)
        first = jnp.min(jnp.where(imp == top, blk, n_blk), axis=0, keepdims=True)
        pick = (blk == first) & (top > -jnp.inf)
        chosen = jnp.where(pick, 1.0, chosen)
        imp = jnp.where(pick, -jnp.inf, imp)
    sel_ref[...] = chosen.astype(BF16)

    def sel_mask(r0, carry):
        hit = jnp.dot(exp_ref[pl.ds(r0, rows), :], sel_ref[...], preferred_element_type=F32)
        kpos = r0 + lax.broadcasted_iota(I32, (rows, Q_TILE), 0)
        return jnp.where((hit > 0.5) & (kpos <= tq), 0.0, -jnp.inf), carry

    nch = (q0 + Q_TILE + rows - 1) // rows
    o_s = _flash_chunks(qs, ks_ref, vst_ref, 0, nch, rows, sel_mask, 0, acc_ref, m_ref, l_ref, NSA_HEADS)
    out = out + gate_row(1) * o_s

    def win_mask(r0, carry):
        kpos = r0 + lax.broadcasted_iota(I32, (Q_TILE, Q_TILE), 0)
        return jnp.where((kpos <= tq) & (kpos > tq - WINDOW), 0.0, -jnp.inf), carry

    lo = jnp.maximum(n - WINDOW // Q_TILE, 0)
    o_w = _flash_chunks(qs, kw_ref, vwt_ref, lo, n + 1, Q_TILE, win_mask, 0, acc_ref, m_ref, l_ref, NSA_HEADS)
    out = out + gate_row(2) * o_w
    o_ref[...] = _unstack_heads_t(out, NSA_HEADS).astype(o_ref.dtype)


def _nsa_attention(c, kcmp, vcmpt, bsz, seq):
    t = bsz * seq
    nq = seq // Q_TILE
    ng = seq // CMP_STRIDE
    n_cmp = (seq - CMP_BLOCK) // CMP_STRIDE + 1
    n_blk = seq // SEL_BLOCK
    n_sel = min(N_SEL_BLOCKS, n_blk)
    j = np.arange(ng)[None, :] * CMP_STRIDE
    b0 = np.arange(n_blk)[:, None] * SEL_BLOCK
    ovl = ((j <= b0 + SEL_BLOCK - 1) & (j + CMP_BLOCK - 1 >= b0) & (np.arange(ng)[None, :] < n_cmp))
    ovl = jnp.asarray(ovl.astype(np.float32), BF16)
    expand = jnp.asarray((np.arange(seq)[:, None] // SEL_BLOCK == np.arange(n_blk)[None, :]).astype(np.float32), BF16)
    qrow = lambda w: pl.BlockSpec((Q_TILE, w), lambda b, n: (b * nq + n, 0))
    seqrow = lambda w: pl.BlockSpec((seq, w), lambda b, n: (b, 0))
    vrow = lambda r: pl.BlockSpec((HEAD_DIM, seq), lambda b, n: (r, b))
    const = lambda a: pl.BlockSpec(a.shape, lambda b, n: (0, 0))
    nqh = NSA_HEADS * Q_TILE
    return pl.pallas_call(
        functools.partial(_nsa_body, n_sel=n_sel, n_cmp=n_cmp),
        grid=(bsz, nq),
        in_specs=[qrow(512), qrow(128),
                  pl.BlockSpec((ng, HEAD_DIM), lambda b, n: (b, 0)), pl.BlockSpec((HEAD_DIM, ng), lambda b, n: (0, b)),
                  const(ovl), const(expand), seqrow(HEAD_DIM), vrow(1), seqrow(HEAD_DIM), vrow(2)],
        out_specs=qrow(512),
        out_shape=jax.ShapeDtypeStruct((t, 512), BF16),
        scratch_shapes=[pltpu.VMEM((n_blk, Q_TILE), BF16), pltpu.VMEM((HEAD_DIM, nqh), F32),
                        pltpu.VMEM((1, nqh), F32), pltpu.VMEM((1, nqh), F32)],
        compiler_params=_cparams("parallel", "arbitrary"),
        name="nsa_attention",
    )(c["nq"], c["misc"], kcmp, vcmpt, ovl, expand, c["nks"], c["vt"], c["nkw"], c["vt"])


GDN_TILE = 512


def _split_bf16(a):
    hi = a.astype(BF16)
    return hi, (a - hi.astype(F32)).astype(BF16)


def _dot3(a, b, dims=None):
    ah, al = _split_bf16(a)
    bh, bl = _split_bf16(b)
    if dims is None:
        f = lambda u, v: jnp.dot(u, v, preferred_element_type=F32)
    else:
        f = lambda u, v: lax.dot_general(u, v, dims, preferred_element_type=F32)
    return f(ah, bh) + (f(ah, bl) + f(al, bh))


def _bdot(a, b, dims=None):
    a, b = a.astype(BF16), b.astype(BF16)
    if dims is None:
        return jnp.dot(a, b, preferred_element_type=F32)
    return lax.dot_general(a, b, dims, preferred_element_type=F32)


def _gdn_body(qkv_ref, misc_ref, z_ref, convw_ref, alog_ref, dtb_ref, normg_ref, o_ref,
              state_ref, tail_ref, q_s, k_s, v_s, gc_s, beta_s):
    tt = qkv_ref.shape[0]
    ch = GDN_CHUNK
    hw = GDN_HEADS * GDN_DIM

    @pl.when(pl.program_id(1) == 0)
    def _():
        state_ref[...] = jnp.zeros(state_ref.shape, F32)
        tail_ref[...] = jnp.zeros(tail_ref.shape, F32)

    x = qkv_ref[...]
    w = convw_ref[...]
    tail = tail_ref[...]
    x8 = x[0:8, :]
    row8 = lax.broadcasted_iota(I32, (8, 1), 0)
    acc = x * w[CONV_WIDTH - 1:CONV_WIDTH, :]
    acc8 = x8 * w[CONV_WIDTH - 1:CONV_WIDTH, :]
    for k in range(1, CONV_WIDTH):
        wk = w[CONV_WIDTH - 1 - k:CONV_WIDTH - k, :]
        acc = acc + pltpu.roll(x, k, 0) * wk
        acc8 = acc8 + jnp.where(row8 < k, pltpu.roll(tail, k, 0), pltpu.roll(x8, k, 0)) * wk
    tail_ref[...] = x[tt - 8:tt, :]
    rowt = lax.broadcasted_iota(I32, (tt, 1), 0)
    acc = jnp.where(rowt < 8, jnp.concatenate([acc8, acc[8:, :]], axis=0), acc)
    xc = acc * jax.nn.sigmoid(acc)

    def l2n(a):
        return a * lax.rsqrt(jnp.sum(a * a, axis=-1, keepdims=True) + RMS_EPS)

    for h in range(GDN_HEADS):
        q_s[h] = l2n(xc[:, h * GDN_DIM:(h + 1) * GDN_DIM]) * (GDN_DIM ** -0.5)
        k_s[h] = l2n(xc[:, hw + h * GDN_DIM:hw + (h + 1) * GDN_DIM])
        v_s[h] = xc[:, 2 * hw + h * GDN_DIM:2 * hw + (h + 1) * GDN_DIM]

    misc = misc_ref[...]
    sp = misc + dtb_ref[...]
    softplus = jnp.maximum(sp, 0.0) + jnp.log(1.0 + jnp.exp(-jnp.abs(sp)))
    g = -jnp.exp(alog_ref[...]) * softplus
    beta_s[...] = jax.nn.sigmoid(misc)
    ri = lax.broadcasted_iota(I32, (tt, tt), 0)
    ci = lax.broadcasted_iota(I32, (tt, tt), 1)
    blk_tril = jnp.where((ri >= ci) & (ri // ch == ci // ch), 1.0, 0.0)
    gc_s[...] = jnp.dot(blk_tril, g, preferred_element_type=F32, precision=lax.Precision.HIGHEST)

    r64 = lax.broadcasted_iota(I32, (ch, ch), 0)
    c64 = lax.broadcasted_iota(I32, (ch, ch), 1)
    tri = r64 >= c64
    strict = r64 > c64
    eye = jnp.where(r64 == c64, 1.0, 0.0)
    norm_g = normg_ref[...]

    def chunk_step(c, _):
        r0 = pl.multiple_of(c * ch, ch)
        gcs = gc_s[pl.ds(r0, ch), :]
        gct = jnp.concatenate([gcs, jnp.zeros_like(gcs)], axis=0).T
        bts = beta_s[pl.ds(r0, ch), :]
        for h in range(GDN_HEADS):
            q = q_s[h, pl.ds(r0, ch), :]
            k = k_s[h, pl.ds(r0, ch), :]
            v = v_s[h, pl.ds(r0, ch), :]
            gcol = gcs[:, MISC_GA + h:MISC_GA + h + 1]
            grow = gct[MISC_GA + h:MISC_GA + h + 1, 0:ch]
            beta = bts[:, MISC_GB + h:MISC_GB + h + 1]
            decay = jnp.exp(jnp.where(tri, gcol - grow, -jnp.inf))
            kb = k * beta
            lower = jnp.where(strict, _bdot(kb, k, _NT) * decay, 0.0)
            nmat = -lower
            inv = eye + nmat
            for _ in range(5):
                nmat = _dot3(nmat, nmat)
                inv = inv + _dot3(inv, nmat)
            eg = jnp.exp(gcol)
            rhs = jnp.concatenate([v * beta, kb * eg], axis=1)
            sol = _dot3(inv, rhs)
            u, wmat = sol[:, :GDN_DIM], sol[:, GDN_DIM:]
            attn = jnp.where(tri, _bdot(q, k, _NT) * decay, 0.0)
            glast = gcol[ch - 1:ch, :]
            state = state_ref[h]
            v_new = u - _bdot(wmat, state)
            o = _bdot(q * eg, state) + _bdot(attn, v_new)
            k_dec = k * jnp.exp(glast - gcol)
            state_ref[h] = state * jnp.exp(glast) + _bdot(k_dec, v_new, (((0,), (0,)), ((), ())))
            o = o * lax.rsqrt(jnp.mean(o * o, axis=-1, keepdims=True) + RMS_EPS) * norm_g
            zz = z_ref[pl.ds(r0, ch), h * GDN_DIM:(h + 1) * GDN_DIM]
            o_ref[pl.ds(r0, ch), h * GDN_DIM:(h + 1) * GDN_DIM] = (o * (zz * jax.nn.sigmoid(zz))).astype(o_ref.dtype)
        return 0

    lax.fori_loop(0, tt // ch, chunk_step, 0)


def _gdn_mixer(c, conv_w, a_log, dt_bias, norm_g, bsz, seq):
    t = bsz * seq
    tt = min(GDN_TILE, seq)
    nt = seq // tt
    lane_row = lambda vals, off: jnp.zeros((1, LANES), F32).at[0, off:off + GDN_HEADS].set(vals.astype(F32))
    alog = lane_row(a_log, MISC_GA)
    dtb = lane_row(dt_bias, MISC_GA)
    row = lambda w: pl.BlockSpec((tt, w), lambda b, j: (b * nt + j, 0))
    const = lambda a: pl.BlockSpec(a.shape, lambda b, j: (0, 0))
    ng = norm_g.reshape(1, GDN_DIM).astype(F32)
    hs = pltpu.VMEM((GDN_HEADS, tt, GDN_DIM), F32)
    return pl.pallas_call(
        _gdn_body,
        grid=(bsz, nt),
        in_specs=[row(GDN_QKV), row(LANES), row(GDN_HEADS * GDN_DIM), const(conv_w), const(alog), const(dtb), const(ng)],
        out_specs=row(GDN_HEADS * GDN_DIM),
        out_shape=jax.ShapeDtypeStruct((t, GDN_HEADS * GDN_DIM), BF16),
        scratch_shapes=[pltpu.VMEM((GDN_HEADS, GDN_DIM, GDN_DIM), F32), pltpu.VMEM((8, GDN_QKV), F32),
                        hs, hs, hs, pltpu.VMEM((tt, LANES), F32), pltpu.VMEM((tt, LANES), F32)],
        compiler_params=_cparams("parallel", "arbitrary"),
        name="gdn_mixer",
    )(c["gqkv"], c["misc"], c["gz"], conv_w.astype(F32), alog, dtb, ng)


ROW_TILE = 256


def _layer_norm(v, g, b):
    mu = jnp.mean(v, axis=-1, keepdims=True)
    vc = v - mu
    var = jnp.mean(vc * vc, axis=-1, keepdims=True)
    return vc * lax.rsqrt(var + LN_EPS) * g + b


def _merge_body(ya_ref, yb_ref, yc_ref, mg_ref, x_ref, wb_ref, wo_ref, g_ref, b_ref, wrt_ref,
                x1_ref, x1b_ref, sct_ref, *, alpha):
    d = x_ref.shape[1]
    merged = mg_ref[:, 0:d] * jnp.dot(ya_ref[...], wb_ref[0], preferred_element_type=F32)
    merged = merged + mg_ref[:, d:2 * d] * jnp.dot(yb_ref[...], wb_ref[1], preferred_element_type=F32)
    merged = merged + mg_ref[:, 2 * d:3 * d] * jnp.dot(yc_ref[...], wb_ref[2], preferred_element_type=F32)
    y = alpha * x_ref[...] + jnp.dot(merged.astype(BF16), wo_ref[...], preferred_element_type=F32)
    x1 = _layer_norm(y, g_ref[...], b_ref[...])
    x1_ref[...] = x1
    x1b = x1.astype(BF16)
    x1b_ref[...] = x1b
    logits_t = lax.dot_general(wrt_ref[...], x1b, _NT, preferred_element_type=F32)
    sct_ref[...] = jax.nn.sigmoid(logits_t)


def _merge_out(ya, yb, yc, mg, x2d, w_branch, w_out, ln_g, ln_b, w_router, alpha):
    t, d = x2d.shape
    tm = ROW_TILE
    row = lambda w: pl.BlockSpec((tm, w), lambda i: (i, 0))
    wb = w_branch.astype(BF16)
    wo = w_out.astype(BF16)
    wrt = w_router.T.astype(BF16)
    g = ln_g.reshape(1, d)
    b = ln_b.reshape(1, d)
    return pl.pallas_call(
        functools.partial(_merge_body, alpha=alpha),
        grid=(t // tm,),
        in_specs=[row(BRANCH_WIDTH)] * 3 + [row(3 * d), row(d),
                  pl.BlockSpec(wb.shape, lambda i: (0, 0, 0)), pl.BlockSpec(wo.shape, lambda i: (0, 0)),
                  pl.BlockSpec((1, d), lambda i: (0, 0)), pl.BlockSpec((1, d), lambda i: (0, 0)),
                  pl.BlockSpec(wrt.shape, lambda i: (0, 0))],
        out_specs=[row(d), row(d), pl.BlockSpec((N_EXPERTS, tm), lambda i: (0, i))],
        out_shape=[jax.ShapeDtypeStruct((t, d), F32), jax.ShapeDtypeStruct((t, d), BF16),
                   jax.ShapeDtypeStruct((N_EXPERTS, t), F32)],
        compiler_params=_cparams("parallel"),
        name="merge_out_ln",
    )(ya, yb, yc, mg, x2d, wb, wo, g, b, wrt)


ROUTE_TILE = 512


def _route_body(sct_ref, bias_ref, eidx_ref, gate_ref):
    sc = sct_ref[...]
    n = sc.shape[1]
    per = N_EXPERTS // N_GROUPS
    biased = sc + bias_ref[...]
    b3 = biased.reshape(N_GROUPS, per, n)
    sub = lax.broadcasted_iota(I32, (N_GROUPS, per, n), 1)
    m1 = jnp.max(b3, axis=1, keepdims=True)
    first = jnp.min(jnp.where(b3 == m1, sub, per), axis=1, keepdims=True)
    m2 = jnp.max(jnp.where(sub == first, -jnp.inf, b3), axis=1, keepdims=True)
    gs = (m1 + m2).reshape(N_GROUPS, n)
    gi = lax.broadcasted_iota(I32, (N_GROUPS, n), 0)
    gmask = jnp.zeros((N_GROUPS, n), F32)
    for _ in range(TOPK_GROUPS):
        top = jnp.max(gs, axis=0, keepdims=True)
        pick = gi == jnp.min(jnp.where(gs == top, gi, N_GROUPS), axis=0, keepdims=True)
        gmask = jnp.where(pick, 1.0, gmask)
        gs = jnp.where(pick, -jnp.inf, gs)
    emask = jnp.broadcast_to(gmask.reshape(N_GROUPS, 1, n), (N_GROUPS, per, n)).reshape(N_EXPERTS, n)
    cand = jnp.where(emask > 0.5, biased, -jnp.inf)
    ei = lax.broadcasted_iota(I32, (N_EXPERTS, n), 0)
    ids, gates = [], []
    for _ in range(TOP_K):
        top = jnp.max(cand, axis=0, keepdims=True)
        idx = jnp.min(jnp.where(cand == top, ei, N_EXPERTS), axis=0, keepdims=True)
        pick = ei == idx
        ids.append(idx)
        gates.append(jnp.sum(jnp.where(pick, sc, 0.0), axis=0, keepdims=True))
        cand = jnp.where(pick, -jnp.inf, cand)
    gate = jnp.concatenate(gates, axis=0)
    gate = gate / jnp.sum(gate, axis=0, keepdims=True) * ROUTED_SCALE
    eidx_ref[...] = jnp.concatenate(ids, axis=0)
    gate_ref[...] = gate


def _route(sct, router_bias):
    e, t = sct.shape
    tn = ROUTE_TILE
    col = lambda r: pl.BlockSpec((r, tn), lambda i: (0, i))
    return pl.pallas_call(
        _route_body,
        grid=(t // tn,),
        in_specs=[col(e), pl.BlockSpec((e, 1), lambda i: (0, 0))],
        out_specs=[col(TOP_K), col(TOP_K)],
        out_shape=[jax.ShapeDtypeStruct((TOP_K, t), I32), jax.ShapeDtypeStruct((TOP_K, t), F32)],
        compiler_params=_cparams("parallel"),
        name="route_topk",
    )(sct, router_bias.reshape(e, 1).astype(F32))


def _expert_body(blk_e_ref, nused_ref, xb_ref, gate_ref, wgu_ref, wd_ref, o_ref):
    i = pl.program_id(0)

    @pl.when(i < nused_ref[0])
    def _():
        gu = jnp.dot(xb_ref[...], wgu_ref[0], preferred_element_type=F32)
        gg, uu = gu[:, :D_EXPERT], gu[:, D_EXPERT:]
        hmid = (gg * jax.nn.sigmoid(gg) * uu).astype(BF16)
        ye = jnp.dot(hmid, wd_ref[0], preferred_element_type=F32)
        o_ref[...] = ye * gate_ref[...]

    @pl.when(i >= nused_ref[0])
    def _():
        o_ref[...] = jnp.zeros(o_ref.shape, o_ref.dtype)


def _expert_ffn(xs, row_gate, blk_e, n_used, wgu, wd):
    n_pad, d = xs.shape
    nb = n_pad // EXPERT_BLOCK
    grid_spec = pltpu.PrefetchScalarGridSpec(
        num_scalar_prefetch=2,
        grid=(nb,),
        in_specs=[pl.BlockSpec((EXPERT_BLOCK, d), lambda i, be, nu: (i, 0)),
                  pl.BlockSpec((EXPERT_BLOCK, 1), lambda i, be, nu: (i, 0)),
                  pl.BlockSpec((1, d, 2 * D_EXPERT), lambda i, be, nu: (be[i], 0, 0)),
                  pl.BlockSpec((1, D_EXPERT, d), lambda i, be, nu: (be[i], 0, 0))],
        out_specs=pl.BlockSpec((EXPERT_BLOCK, d), lambda i, be, nu: (i, 0)),
    )
    return pl.pallas_call(
        _expert_body,
        grid_spec=grid_spec,
        out_shape=jax.ShapeDtypeStruct((n_pad, d), F32),
        compiler_params=_cparams("arbitrary"),
        name="expert_ffn",
    )(blk_e, n_used, xs, row_gate, wgu, wd)


def _moe_routed(x1b, eidx_t, gate_t, wgu, wd):
    n_tok, d = x1b.shape
    n_rows = n_tok * TOP_K
    n_blocks = -(-n_rows // EXPERT_BLOCK) + N_EXPERTS
    n_pad = n_blocks * EXPERT_BLOCK
    flat_e = eidx_t.T.reshape(-1)
    flat_g = gate_t.T.reshape(-1)
    order = jnp.argsort(flat_e)
    sorted_e = flat_e[order]
    counts = jnp.bincount(flat_e, length=N_EXPERTS).astype(I32)
    padded = (counts + EXPERT_BLOCK - 1) // EXPERT_BLOCK * EXPERT_BLOCK
    start = jnp.cumsum(counts) - counts
    pad_end = jnp.cumsum(padded)
    pad_start = pad_end - padded
    dest = pad_start[sorted_e] + jnp.arange(n_rows, dtype=I32) - start[sorted_e]
    row_tok = jnp.full((n_pad,), n_tok, I32).at[dest].set((order // TOP_K).astype(I32))
    row_gate = jnp.zeros((n_pad,), F32).at[dest].set(flat_g[order])
    blk_e = jnp.minimum(jnp.searchsorted(pad_end, jnp.arange(n_blocks, dtype=I32) * EXPERT_BLOCK, side='right'),
                        N_EXPERTS - 1).astype(I32)
    n_used = (pad_end[-1] // EXPERT_BLOCK).astype(I32).reshape(1)
    x_pad = jnp.concatenate([x1b, jnp.zeros((1, d), x1b.dtype)], axis=0)
    xs = x_pad[row_tok]
    ye = _expert_ffn(xs, row_gate[:, None], blk_e, n_used, wgu, wd)
    return jnp.zeros((n_tok + 1, d), F32).at[row_tok].add(ye)[:n_tok]


def _ffn_out_body(x1_ref, x1b_ref, routed_ref, p_ref, wsgu_ref, wsd_ref, wpg_ref, wpp_ref, g_ref, b_ref, o_ref, *, alpha):
    xb = x1b_ref[...]
    su = jnp.dot(xb, wsgu_ref[...], preferred_element_type=F32)
    sg, uu = su[:, :D_SHARED], su[:, D_SHARED:]
    shared = jnp.dot((sg * jax.nn.sigmoid(sg) * uu).astype(BF16), wsd_ref[...], preferred_element_type=F32)
    pg = jax.nn.sigmoid(jnp.dot(xb, wpg_ref[...], preferred_element_type=F32))
    ple = pg * jnp.dot(p_ref[...].astype(BF16), wpp_ref[...], preferred_element_type=F32)
    y = alpha * x1_ref[...] + (routed_ref[...] + shared) + ple
    o_ref[...] = _layer_norm(y, g_ref[...], b_ref[...])


def _ffn_out(x1, x1b, routed, p2d, w_sgu, w_sd, w_pg, w_pp, ln_g, ln_b, alpha):
    t, d = x1.shape
    tm = ROW_TILE
    row = lambda w: pl.BlockSpec((tm, w), lambda i: (i, 0))
    const = lambda a: pl.BlockSpec(a.shape, lambda i: (0, 0))
    ws = [w.astype(BF16) for w in (w_sgu, w_sd, w_pg, w_pp)]
    g = ln_g.reshape(1, d)
    b = ln_b.reshape(1, d)
    return pl.pallas_call(
        functools.partial(_ffn_out_body, alpha=alpha),
        grid=(t // tm,),
        in_specs=[row(d), row(d), row(d), row(PLE_DIM)] + [const(w) for w in ws] + [const(g), const(b)],
        out_specs=row(d),
        out_shape=jax.ShapeDtypeStruct((t, d), F32),
        compiler_params=_cparams("parallel"),
        name="ffn_out_ln",
    )(x1, x1b, routed, p2d, *ws, g, b)


def kernel(x, p, w_in, w_dsa_uv, w_cmp_k, pe_cmp_k, w_cmp_v, pe_cmp_v, conv_w, a_log, dt_bias, gdn_norm, w_branch, w_out, ln1_g, ln1_b, w_router, router_bias, w_gate_up, w_down, w_sh_gate_up, w_sh_down, w_ple_proj, w_ple_gate, ln2_g, ln2_b):
    bsz, seq, d = x.shape
    depth = w_in.shape[0]
    alpha = (2 * depth) ** 0.25
    t = bsz * seq
    tabs64, _ = _rope_lane_tables(seq, HEAD_DIM)
    tabs32, _ = _rope_lane_tables(seq, IDX_DIM)
    x2d = x.reshape(t, d)
    for i in range(depth):
        c = _in_projection(x2d, _split_w_in(w_in[i]), tabs64, tabs32, seq)
        ya = _dsa_attention(c, w_dsa_uv[i], bsz, seq)
        kcmp, vcmpt = _nsa_compress(c, w_cmp_k[i], pe_cmp_k[i], w_cmp_v[i], pe_cmp_v[i], bsz, seq)
        yb = _nsa_attention(c, kcmp, vcmpt, bsz, seq)
        yc = _gdn_mixer(c, conv_w[i], a_log[i], dt_bias[i], gdn_norm[i], bsz, seq)
        x1, x1b, sct = _merge_out(ya, yb, yc, c["mg"], x2d, w_branch[i], w_out[i], ln1_g[i], ln1_b[i],
                                  w_router[i], alpha)
        eidx_t, gate_t = _route(sct, router_bias[i])
        routed = _moe_routed(x1b, eidx_t, gate_t, w_gate_up[i].astype(BF16), w_down[i].astype(BF16))
        x2d = _ffn_out(x1, x1b, routed, p[i].reshape(t, PLE_DIM), w_sh_gate_up[i], w_sh_down[i],
                       w_ple_gate[i], w_ple_proj[i], ln2_g[i], ln2_b[i], alpha)
    return x2d.reshape(bsz, seq, d)
```

```python
import functools
import math

import numpy as np
import jax
import jax.numpy as jnp
from jax import lax
from jax.experimental import pallas as pl
from jax.experimental.pallas import tpu as pltpu

F32 = jnp.float32
BF16 = jnp.bfloat16
I32 = jnp.int32

HEAD_DIM = 64
ROPE_THETA = 500000.0
ROPE_FRACTION = 4
DSA_HEADS = 8
DSA_LATENT = 64
IDX_HEADS = 4
IDX_DIM = 32
DSA_TOPK = 256
NSA_HEADS = 8
CMP_BLOCK = 32
CMP_STRIDE = 16
SEL_BLOCK = 32
N_SEL_BLOCKS = 8
WINDOW = 256
FORCED_SCORE = 1.0e4
GDN_HEADS = 4
GDN_DIM = 128
CONV_WIDTH = 4
GDN_CHUNK = 64
BRANCH_WIDTH = 512
N_BRANCHES = 3
GDN_QKV = 3 * GDN_HEADS * GDN_DIM
N_EXPERTS = 64
TOP_K = 8
N_GROUPS = 8
TOPK_GROUPS = 4
D_EXPERT = 256
D_SHARED = 256
ROUTED_SCALE = 2.5
EXPERT_BLOCK = 512
PLE_DIM = 256
LN_EPS = 1e-5
RMS_EPS = 1e-6

IN_SPLITS = (
    ("dsa_q", DSA_HEADS * DSA_LATENT), ("dsa_kv", DSA_LATENT),
    ("idx_q", IDX_HEADS * IDX_DIM), ("idx_k", IDX_DIM), ("idx_w", IDX_HEADS),
    ("nsa_q", NSA_HEADS * HEAD_DIM),
    ("nsa_kc", HEAD_DIM), ("nsa_vc", HEAD_DIM),
    ("nsa_ks", HEAD_DIM), ("nsa_vs", HEAD_DIM),
    ("nsa_kw", HEAD_DIM), ("nsa_vw", HEAD_DIM),
    ("nsa_g", NSA_HEADS * 3),
    ("gdn_qkv", GDN_QKV), ("gdn_a", GDN_HEADS), ("gdn_b", GDN_HEADS), ("gdn_z", GDN_HEADS * GDN_DIM),
    ("merge_g", N_BRANCHES * 1024),
)

LANES = 128
Q_TILE = 128
KEY_CHUNK = 256
VMEM_LIMIT = 56 * 1024 * 1024
INT_MIN = -2 ** 31
NEG_BIG = -1e30


def _cparams(*sem):
    return pltpu.CompilerParams(dimension_semantics=sem, vmem_limit_bytes=VMEM_LIMIT)


def _rope_lane_tables(seq, head_dim):
    rot = head_dim // ROPE_FRACTION
    half = rot // 2
    inv_freq = ROPE_THETA ** (-(jnp.arange(half, dtype=F32) * 2.0 / rot))
    ang = jnp.arange(seq, dtype=F32)[:, None] * inv_freq[None, :]
    cos, sin = jnp.cos(ang), jnp.sin(ang)
    one = jnp.ones((seq, head_dim - rot), F32)
    c = jnp.concatenate([cos, cos, one], axis=-1)
    sa = jnp.concatenate([-sin, jnp.zeros((seq, head_dim - half), F32)], axis=-1)
    sb = jnp.concatenate([jnp.zeros((seq, half), F32), sin, jnp.zeros((seq, head_dim - rot), F32)], axis=-1)
    rep = LANES // head_dim
    return tuple(jnp.tile(a, (1, rep)) for a in (c, sa, sb)), half


def _rope_apply(y, c, sa, sb, half):
    n = y.shape[-1]
    rep = n // LANES
    if rep > 1:
        c, sa, sb = (jnp.concatenate([a] * rep, axis=-1) for a in (c, sa, sb))
    up = pltpu.roll(y, n - half, 1)
    dn = pltpu.roll(y, half, 1)
    return y * c + up * sa + dn * sb


IN_TILE = 256


def _inproj_body(x_ref, w64_ref, w32_ref, wpl_ref, wvt_ref, wmg_ref,
                 c64_ref, sa64_ref, sb64_ref, c32_ref, sa32_ref, sb32_ref,
                 dq_ref, nq_ref, krot_ref, nkc_ref, nks_ref, nkw_ref, iq_ref, ik_ref,
                 nvc_ref, misc_ref, gz_ref, gqkv_ref, vt_ref, mg_ref):
    xb = x_ref[...].astype(BF16)
    c64, sa64, sb64 = c64_ref[...], sa64_ref[...], sb64_ref[...]
    half64 = HEAD_DIM // ROPE_FRACTION // 2
    half32 = IDX_DIM // ROPE_FRACTION // 2

    def proj(w_ref, lo, hi):
        return jnp.dot(xb, w_ref[:, lo:hi], preferred_element_type=F32)

    dq_ref[...] = _rope_apply(proj(w64_ref, 0, 512), c64, sa64, sb64, half64).astype(BF16)
    nq_ref[...] = _rope_apply(proj(w64_ref, 512, 1024), c64, sa64, sb64, half64).astype(BF16)
    k1 = _rope_apply(proj(w64_ref, 1024, 1152), c64, sa64, sb64, half64).astype(BF16)
    krot_ref[...] = k1[:, :64]
    nkc_ref[...] = k1[:, 64:]
    k2 = _rope_apply(proj(w64_ref, 1152, 1280), c64, sa64, sb64, half64).astype(BF16)
    nks_ref[...] = k2[:, :64]
    nkw_ref[...] = k2[:, 64:]
    c32, sa32, sb32 = c32_ref[...], sa32_ref[...], sb32_ref[...]
    iq_ref[...] = _rope_apply(proj(w32_ref, 0, 128), c32, sa32, sb32, half32).astype(BF16)
    ik = _rope_apply(proj(w32_ref, 128, 256), c32, sa32, sb32, half32).astype(BF16)
    ik_ref[...] = ik[:, :IDX_DIM]
    pl0 = proj(wpl_ref, 0, 128)
    nvc_ref[...] = pl0[:, :64].astype(BF16)
    misc_ref[...] = proj(wpl_ref, 128, 256)
    gz_ref[...] = proj(wpl_ref, 256, 768)
    for j in range(3):
        gqkv_ref[:, j * 512:(j + 1) * 512] = proj(wpl_ref, 768 + j * 512, 768 + (j + 1) * 512)
    vt_ref[...] = lax.dot_general(wvt_ref[...], xb, (((1,), (1,)), ((), ())),
                                  preferred_element_type=F32).astype(BF16)
    for j in range(6):
        g = proj(wmg_ref, j * 512, (j + 1) * 512)
        mg_ref[:, j * 512:(j + 1) * 512] = jax.nn.sigmoid(g)


def _split_w_in(w):
    cols = {}
    off = 0
    for name, size in IN_SPLITS:
        cols[name] = w[:, off:off + size]
        off += size
    d = w.shape[0]
    z = lambda n: jnp.zeros((d, n), w.dtype)
    w64 = jnp.concatenate([cols["dsa_q"], cols["nsa_q"], cols["dsa_kv"], cols["nsa_kc"],
                           cols["nsa_ks"], cols["nsa_kw"]], axis=1)
    w32 = jnp.concatenate([cols["idx_q"], cols["idx_k"], z(LANES - IDX_DIM)], axis=1)
    misc = jnp.concatenate([cols["idx_w"], cols["gdn_a"], cols["gdn_b"], cols["nsa_g"]], axis=1)
    misc = jnp.concatenate([misc, z(LANES - misc.shape[1])], axis=1)
    wpl = jnp.concatenate([cols["nsa_vc"], z(64), misc, cols["gdn_z"], cols["gdn_qkv"]], axis=1)
    wvt = jnp.concatenate([cols["dsa_kv"], cols["nsa_vs"], cols["nsa_vw"], z(64)], axis=1).T
    return [a.astype(BF16) for a in (w64, w32, wpl, wvt, cols["merge_g"])]


MISC_IW = 0
MISC_GA = IDX_HEADS
MISC_GB = MISC_GA + GDN_HEADS
MISC_NG = MISC_GB + GDN_HEADS


def _in_projection(x2d, wparts, tabs64, tabs32, seq):
    t, d = x2d.shape
    tm = IN_TILE
    nt = seq // tm
    w64, w32, wpl, wvt, wmg = wparts
    row = lambda w: pl.BlockSpec((tm, w), lambda i: (i, 0))
    full = lambda a: pl.BlockSpec(a.shape, lambda i: (0, 0))
    tab = pl.BlockSpec((tm, LANES), lambda i: (i % nt, 0))
    outs = [
        ("dq", 512, BF16), ("nq", 512, BF16), ("krot", 64, BF16), ("nkc", 64, BF16), ("nks", 64, BF16),
        ("nkw", 64, BF16), ("iq", 128, BF16), ("ik", IDX_DIM, BF16), ("nvc", 64, BF16),
        ("misc", 128, F32), ("gz", 512, F32), ("gqkv", GDN_QKV, F32),
    ]
    out_shape = [jax.ShapeDtypeStruct((t, w), dt) for _, w, dt in outs]
    out_specs = [row(w) for _, w, _ in outs]
    out_shape += [jax.ShapeDtypeStruct((256, t), BF16), jax.ShapeDtypeStruct((t, 3 * d), F32)]
    out_specs += [pl.BlockSpec((256, tm), lambda i: (0, i)), row(3 * d)]
    res = pl.pallas_call(
        _inproj_body,
        grid=(t // tm,),
        in_specs=[row(d), full(w64), full(w32), full(wpl), full(wvt), full(wmg)] + [tab] * 6,
        out_specs=out_specs,
        out_shape=out_shape,
        compiler_params=_cparams("parallel"),
        name="in_projection",
    )(x2d, w64, w32, wpl, wvt, wmg, *tabs64, *tabs32)
    names = [n for n, _, _ in outs] + ["vt", "mg"]
    return dict(zip(names, res))


_NT = (((1,), (1,)), ((), ()))


def _stack_heads(q, n_heads, width):
    return jnp.concatenate([q[:, h * width:(h + 1) * width] for h in range(n_heads)], axis=0)


def _unstack_heads_t(o_t, n_heads):
    q = o_t.shape[1] // n_heads
    rows = jnp.concatenate([o_t[:, h * q:(h + 1) * q] for h in range(n_heads)], axis=0)
    return rows.T


def _flash_chunks(qs, k_ref, vt_ref, lo, hi, rows, mask_fn, carry0, acc_ref, m_ref, l_ref, n_heads):
    m_ref[...] = jnp.full(m_ref.shape, NEG_BIG, F32)
    l_ref[...] = jnp.zeros(l_ref.shape, F32)
    acc_ref[...] = jnp.zeros(acc_ref.shape, F32)

    def body(c, carry):
        r0 = pl.multiple_of(c * rows, rows)
        bias, carry = mask_fn(r0, carry)
        kc = k_ref[pl.ds(r0, rows), :]
        s = lax.dot_general(kc, qs, _NT, preferred_element_type=F32)
        s = s + jnp.concatenate([bias] * n_heads, axis=1)
        m_old = m_ref[...]
        m_new = jnp.maximum(m_old, jnp.max(s, axis=0, keepdims=True))
        alpha = jnp.exp(m_old - m_new)
        p = jnp.exp(s - m_new)
        l_ref[...] = alpha * l_ref[...] + jnp.sum(p, axis=0, keepdims=True)
        pv = jnp.dot(vt_ref[:, pl.ds(r0, rows)], p.astype(BF16), preferred_element_type=F32)
        acc_ref[...] = acc_ref[...] * alpha + pv
        m_ref[...] = m_new
        return carry

    lax.fori_loop(lo, hi, body, carry0)
    return acc_ref[...] / jnp.maximum(l_ref[...], 1e-30)


def _dsa_body(iq_ref, dq_ref, misc_ref, ik_ref, krot_ref, vt_ref, wuvt_ref, o_ref,
              keys_ref, acc_ref, m_ref, l_ref, *, n_sel):
    rows = KEY_CHUNK
    q0 = pl.program_id(1) * Q_TILE
    nch = (q0 + Q_TILE + rows - 1) // rows
    tq = q0 + lax.broadcasted_iota(I32, (1, Q_TILE), 1)
    misc_t = misc_ref[...].T
    coef = (IDX_HEADS ** -0.5) * (IDX_DIM ** -0.5)
    iq = iq_ref[...]
    int_min = jnp.int32(INT_MIN)

    def score_chunk(c, _):
        r0 = pl.multiple_of(c * rows, rows)
        ikc = ik_ref[pl.ds(r0, rows), :]
        acc = jnp.zeros((rows, Q_TILE), F32)
        for h in range(IDX_HEADS):
            s = lax.dot_general(ikc, iq[:, h * IDX_DIM:(h + 1) * IDX_DIM], _NT, preferred_element_type=F32)
            acc = acc + jnp.maximum(s, 0.0) * (misc_t[MISC_IW + h:MISC_IW + h + 1, :] * coef)
        acc = jnp.where(acc == 0.0, 0.0, acc)
        bits = pltpu.bitcast(acc, I32)
        key = bits ^ ((bits >> 31) & jnp.int32(0x7FFFFFFF))
        kpos = r0 + lax.broadcasted_iota(I32, (rows, Q_TILE), 0)
        keys_ref[pl.ds(r0, rows), :] = jnp.where(kpos <= tq, key, int_min)
        return 0

    lax.fori_loop(0, nch, score_chunk, 0)

    def count(pred):
        def body(c, acc):
            r0 = pl.multiple_of(c * rows, rows)
            hit = pred(keys_ref[pl.ds(r0, rows), :])
            return acc + jnp.sum(jnp.where(hit, 1, 0).astype(I32), axis=0, keepdims=True)
        return lax.fori_loop(0, nch, body, jnp.zeros((1, Q_TILE), I32))

    kq = jnp.minimum(tq + 1, n_sel)

    def bit_step(i, tu):
        cand_u = tu | lax.shift_left(jnp.int32(1), 31 - i)
        cand = cand_u ^ int_min
        cnt = count(lambda kk: kk >= cand)
        return jnp.where(cnt >= kq, cand_u, tu)

    thr = lax.fori_loop(0, 32, bit_step, jnp.zeros((1, Q_TILE), I32)) ^ int_min
    need = (kq - count(lambda kk: kk > thr)).astype(F32)

    ri = lax.broadcasted_iota(I32, (rows, rows), 0)
    ci = lax.broadcasted_iota(I32, (rows, rows), 1)
    tri = jnp.where(ri > ci, 1.0, 0.0).astype(BF16)

    def mask_fn(r0, seen):
        kk = keys_ref[pl.ds(r0, rows), :]
        tie = jnp.where(kk == thr, 1.0, 0.0)
        before = jnp.dot(tri, tie.astype(BF16), preferred_element_type=F32) + seen
        take = (kk > thr) | ((kk == thr) & (before < need))
        return jnp.where(take, 0.0, -jnp.inf), seen + jnp.sum(tie, axis=0, keepdims=True)

    qs = _stack_heads(dq_ref[...], DSA_HEADS, DSA_LATENT) * jnp.asarray(DSA_LATENT ** -0.5, BF16)
    o_lat = _flash_chunks(qs, krot_ref, vt_ref, 0, nch, rows, mask_fn, jnp.zeros((1, Q_TILE), F32),
                          acc_ref, m_ref, l_ref, DSA_HEADS)
    outs = []
    for h in range(DSA_HEADS):
        oh = o_lat[:, h * Q_TILE:(h + 1) * Q_TILE].astype(BF16)
        outs.append(jnp.dot(wuvt_ref[h], oh, preferred_element_type=F32))
    o_ref[...] = jnp.concatenate(outs, axis=0).T.astype(o_ref.dtype)


def _dsa_attention(c, w_uv, bsz, seq):
    t = bsz * seq
    nq = seq // Q_TILE
    n_sel = min(DSA_TOPK, seq // 4)
    wuvt = jnp.swapaxes(w_uv, 1, 2).astype(BF16)
    qrow = lambda w: pl.BlockSpec((Q_TILE, w), lambda b, n: (b * nq + n, 0))
    seqrow = lambda w: pl.BlockSpec((seq, w), lambda b, n: (b, 0))
    nqh = DSA_HEADS * Q_TILE
    return pl.pallas_call(
        functools.partial(_dsa_body, n_sel=n_sel),
        grid=(bsz, nq),
        in_specs=[qrow(128), qrow(512), qrow(128), seqrow(IDX_DIM), seqrow(DSA_LATENT),
                  pl.BlockSpec((DSA_LATENT, seq), lambda b, n: (0, b)),
                  pl.BlockSpec(wuvt.shape, lambda b, n: (0, 0, 0))],
        out_specs=qrow(512),
        out_shape=jax.ShapeDtypeStruct((t, 512), BF16),
        scratch_shapes=[pltpu.VMEM((seq, Q_TILE), I32), pltpu.VMEM((DSA_LATENT, nqh), F32),
                        pltpu.VMEM((1, nqh), F32), pltpu.VMEM((1, nqh), F32)],
        compiler_params=_cparams("parallel", "arbitrary"),
        name="dsa_attention",
    )(c["iq"], c["dq"], c["misc"], c["ik"], c["krot"], c["vt"], wuvt)


CMP_PER_ROW = CMP_STRIDE * HEAD_DIM


def _nsa_cmp_body(kc_ref, vc_ref, wk_ref, wv_ref, pek_ref, pev_ref, kcmp_ref, vcmpt_ref):
    n = kc_ref.shape[0]

    def compress(x_ref, w_ref, pe_ref):
        xf = x_ref[...].astype(F32)
        lo = jnp.dot((xf + pe_ref[0:1, :]).astype(BF16), w_ref[0], preferred_element_type=F32)
        hi = jnp.dot((xf + pe_ref[1:2, :]).astype(BF16), w_ref[1], preferred_element_type=F32)
        return lo + pltpu.roll(hi, n - 1, 0)

    kcmp_ref[...] = compress(kc_ref, wk_ref, pek_ref).astype(BF16)
    vc = compress(vc_ref, wv_ref, pev_ref)
    vpad = jnp.concatenate([vc, jnp.zeros_like(vc)], axis=1)
    vcmpt_ref[...] = vpad.T[:HEAD_DIM, :].astype(BF16)


def _nsa_compress(c, w_ck, pe_k, w_cv, pe_v, bsz, seq):
    ng = seq // CMP_STRIDE
    kc2 = c["nkc"].reshape(bsz * ng, CMP_PER_ROW)
    vc2 = c["nvc"].reshape(bsz * ng, CMP_PER_ROW)
    wk = w_ck.reshape(2, CMP_PER_ROW, HEAD_DIM).astype(BF16)
    wv = w_cv.reshape(2, CMP_PER_ROW, HEAD_DIM).astype(BF16)
    pek = pe_k.reshape(2, CMP_PER_ROW)
    pev = pe_v.reshape(2, CMP_PER_ROW)
    grp = pl.BlockSpec((ng, CMP_PER_ROW), lambda b: (b, 0))
    wsp = pl.BlockSpec((2, CMP_PER_ROW, HEAD_DIM), lambda b: (0, 0, 0))
    psp = pl.BlockSpec((2, CMP_PER_ROW), lambda b: (0, 0))
    return pl.pallas_call(
        _nsa_cmp_body,
        grid=(bsz,),
        in_specs=[grp, grp, wsp, wsp, psp, psp],
        out_specs=[pl.BlockSpec((ng, HEAD_DIM), lambda b: (b, 0)), pl.BlockSpec((HEAD_DIM, ng), lambda b: (0, b))],
        out_shape=[jax.ShapeDtypeStruct((bsz * ng, HEAD_DIM), BF16), jax.ShapeDtypeStruct((HEAD_DIM, bsz * ng), BF16)],
        compiler_params=_cparams("parallel"),
        name="nsa_compress",
    )(kc2, vc2, wk, wv, pek, pev)


def _nsa_body(nq_ref, misc_ref, kcmp_ref, vcmpt_ref, ovl_ref, exp_ref, ks_ref, vst_ref, kw_ref, vwt_ref,
              o_ref, sel_ref, acc_ref, m_ref, l_ref, *, n_sel, n_cmp):
    rows = KEY_CHUNK
    n = pl.program_id(1)
    q0 = n * Q_TILE
    tq = q0 + lax.broadcasted_iota(I32, (1, Q_TILE), 1)
    qs = _stack_heads(nq_ref[...], NSA_HEADS, HEAD_DIM) * jnp.asarray(HEAD_DIM ** -0.5, BF16)
    gate_t = jax.nn.sigmoid(misc_ref[...]).T

    def gate_row(branch):
        return jnp.concatenate([gate_t[MISC_NG + 3 * h + branch:MISC_NG + 3 * h + branch + 1, :]
                                for h in range(NSA_HEADS)], axis=1)

    ng = kcmp_ref.shape[0]
    jrow = lax.broadcasted_iota(I32, (ng, Q_TILE), 0)
    ok_c = (jrow * CMP_STRIDE + (CMP_BLOCK - 1) <= tq) & (jrow < n_cmp)
    bias_c = jnp.where(ok_c, 0.0, -jnp.inf)
    s = lax.dot_general(kcmp_ref[...], qs, _NT, preferred_element_type=F32)
    s = s + jnp.concatenate([bias_c] * NSA_HEADS, axis=1)
    mx = jnp.maximum(jnp.max(s, axis=0, keepdims=True), NEG_BIG)
    e = jnp.exp(s - mx)
    pc = (e / jnp.maximum(jnp.sum(e, axis=0, keepdims=True), 1e-30)).astype(BF16)
    out = gate_row(0) * jnp.dot(vcmpt_ref[...], pc, preferred_element_type=F32)

    imp8 = jnp.dot(ovl_ref[...], pc, preferred_element_type=F32)
    imp = imp8[:, 0:Q_TILE]
    for h in range(1, NSA_HEADS):
        imp = imp + imp8[:, h * Q_TILE:(h + 1) * Q_TILE]
    n_blk = imp.shape[0]
    blk = lax.broadcasted_iota(I32, (n_blk, Q_TILE), 0)
    cur = tq // SEL_BLOCK
    forced = (blk == 0) | (blk == cur) | (blk == cur - 1)
    imp = jnp.where(blk <= cur, jnp.where(forced, FORCED_SCORE, imp), -jnp.inf)
    chosen = jnp.zeros((n_blk, Q_TILE), F32)
    for _ in range(n_sel):
        top = jnp.max(imp, axis=0, keepdims=True)
        first = jnp.min(jnp.where(imp == top, blk, n_blk), axis=0, keepdims=True)
        pick = (blk == first) & (top > -jnp.inf)
        chosen = jnp.where(pick, 1.0, chosen)
        imp = jnp.where(pick, -jnp.inf, imp)
    sel_ref[...] = chosen.astype(BF16)

    def sel_mask(r0, carry):
        hit = jnp.dot(exp_ref[pl.ds(r0, rows), :], sel_ref[...], preferred_element_type=F32)
        kpos = r0 + lax.broadcasted_iota(I32, (rows, Q_TILE), 0)
        return jnp.where((hit > 0.5) & (kpos <= tq), 0.0, -jnp.inf), carry

    nch = (q0 + Q_TILE + rows - 1) // rows
    o_s = _flash_chunks(qs, ks_ref, vst_ref, 0, nch, rows, sel_mask, 0, acc_ref, m_ref, l_ref, NSA_HEADS)
    out = out + gate_row(1) * o_s

    def win_mask(r0, carry):
        kpos = r0 + lax.broadcasted_iota(I32, (Q_TILE, Q_TILE), 0)
        return jnp.where((kpos <= tq) & (kpos > tq - WINDOW), 0.0, -jnp.inf), carry

    lo = jnp.maximum(n - WINDOW // Q_TILE, 0)
    o_w = _flash_chunks(qs, kw_ref, vwt_ref, lo, n + 1, Q_TILE, win_mask, 0, acc_ref, m_ref, l_ref, NSA_HEADS)
    out = out + gate_row(2) * o_w
    o_ref[...] = _unstack_heads_t(out, NSA_HEADS).astype(o_ref.dtype)


def _nsa_attention(c, kcmp, vcmpt, bsz, seq):
    t = bsz * seq
    nq = seq // Q_TILE
    ng = seq // CMP_STRIDE
    n_cmp = (seq - CMP_BLOCK) // CMP_STRIDE + 1
    n_blk = seq // SEL_BLOCK
    n_sel = min(N_SEL_BLOCKS, n_blk)
    j = np.arange(ng)[None, :] * CMP_STRIDE
    b0 = np.arange(n_blk)[:, None] * SEL_BLOCK
    ovl = ((j <= b0 + SEL_BLOCK - 1) & (j + CMP_BLOCK - 1 >= b0) & (np.arange(ng)[None, :] < n_cmp))
    ovl = jnp.asarray(ovl.astype(np.float32), BF16)
    expand = jnp.asarray((np.arange(seq)[:, None] // SEL_BLOCK == np.arange(n_blk)[None, :]).astype(np.float32), BF16)
    qrow = lambda w: pl.BlockSpec((Q_TILE, w), lambda b, n: (b * nq + n, 0))
    seqrow = lambda w: pl.BlockSpec((seq, w), lambda b, n: (b, 0))
    vrow = lambda r: pl.BlockSpec((HEAD_DIM, seq), lambda b, n: (r, b))
    const = lambda a: pl.BlockSpec(a.shape, lambda b, n: (0, 0))
    nqh = NSA_HEADS * Q_TILE
    return pl.pallas_call(
        functools.partial(_nsa_body, n_sel=n_sel, n_cmp=n_cmp),
        grid=(bsz, nq),
        in_specs=[qrow(512), qrow(128),
                  pl.BlockSpec((ng, HEAD_DIM), lambda b, n: (b, 0)), pl.BlockSpec((HEAD_DIM, ng), lambda b, n: (0, b)),
                  const(ovl), const(expand), seqrow(HEAD_DIM), vrow(1), seqrow(HEAD_DIM), vrow(2)],
        out_specs=qrow(512),
        out_shape=jax.ShapeDtypeStruct((t, 512), BF16),
        scratch_shapes=[pltpu.VMEM((n_blk, Q_TILE), BF16), pltpu.VMEM((HEAD_DIM, nqh), F32),
                        pltpu.VMEM((1, nqh), F32), pltpu.VMEM((1, nqh), F32)],
        compiler_params=_cparams("parallel", "arbitrary"),
        name="nsa_attention",
    )(c["nq"], c["misc"], kcmp, vcmpt, ovl, expand, c["nks"], c["vt"], c["nkw"], c["vt"])


GDN_TILE = 512


def _split_bf16(a):
    hi = a.astype(BF16)
    return hi, (a - hi.astype(F32)).astype(BF16)


def _dot3(a, b, dims=None):
    ah, al = _split_bf16(a)
    bh, bl = _split_bf16(b)
    if dims is None:
        f = lambda u, v: jnp.dot(u, v, preferred_element_type=F32)
    else:
        f = lambda u, v: lax.dot_general(u, v, dims, preferred_element_type=F32)
    return f(ah, bh) + (f(ah, bl) + f(al, bh))


def _bdot(a, b, dims=None):
    a, b = a.astype(BF16), b.astype(BF16)
    if dims is None:
        return jnp.dot(a, b, preferred_element_type=F32)
    return lax.dot_general(a, b, dims, preferred_element_type=F32)


def _gdn_body(qkv_ref, misc_ref, z_ref, convw_ref, alog_ref, dtb_ref, normg_ref, o_ref,
              state_ref, tail_ref, q_s, k_s, v_s, gc_s, beta_s):
    tt = qkv_ref.shape[0]
    ch = GDN_CHUNK
    hw = GDN_HEADS * GDN_DIM

    @pl.when(pl.program_id(1) == 0)
    def _():
        state_ref[...] = jnp.zeros(state_ref.shape, F32)
        tail_ref[...] = jnp.zeros(tail_ref.shape, F32)

    x = qkv_ref[...]
    w = convw_ref[...]
    tail = tail_ref[...]
    x8 = x[0:8, :]
    row8 = lax.broadcasted_iota(I32, (8, 1), 0)
    acc = x * w[CONV_WIDTH - 1:CONV_WIDTH, :]
    acc8 = x8 * w[CONV_WIDTH - 1:CONV_WIDTH, :]
    for k in range(1, CONV_WIDTH):
        wk = w[CONV_WIDTH - 1 - k:CONV_WIDTH - k, :]
        acc = acc + pltpu.roll(x, k, 0) * wk
        acc8 = acc8 + jnp.where(row8 < k, pltpu.roll(tail, k, 0), pltpu.roll(x8, k, 0)) * wk
    tail_ref[...] = x[tt - 8:tt, :]
    rowt = lax.broadcasted_iota(I32, (tt, 1), 0)
    acc = jnp.where(rowt < 8, jnp.concatenate([acc8, acc[8:, :]], axis=0), acc)
    xc = acc * jax.nn.sigmoid(acc)

    def l2n(a):
        return a * lax.rsqrt(jnp.sum(a * a, axis=-1, keepdims=True) + RMS_EPS)

    for h in range(GDN_HEADS):
        q_s[h] = l2n(xc[:, h * GDN_DIM:(h + 1) * GDN_DIM]) * (GDN_DIM ** -0.5)
        k_s[h] = l2n(xc[:, hw + h * GDN_DIM:hw + (h + 1) * GDN_DIM])
        v_s[h] = xc[:, 2 * hw + h * GDN_DIM:2 * hw + (h + 1) * GDN_DIM]

    misc = misc_ref[...]
    sp = misc + dtb_ref[...]
    softplus = jnp.maximum(sp, 0.0) + jnp.log(1.0 + jnp.exp(-jnp.abs(sp)))
    g = -jnp.exp(alog_ref[...]) * softplus
    beta_s[...] = jax.nn.sigmoid(misc)
    ri = lax.broadcasted_iota(I32, (tt, tt), 0)
    ci = lax.broadcasted_iota(I32, (tt, tt), 1)
    blk_tril = jnp.where((ri >= ci) & (ri // ch == ci // ch), 1.0, 0.0)
    gc_s[...] = jnp.dot(blk_tril, g, preferred_element_type=F32, precision=lax.Precision.HIGHEST)

    r64 = lax.broadcasted_iota(I32, (ch, ch), 0)
    c64 = lax.broadcasted_iota(I32, (ch, ch), 1)
    tri = r64 >= c64
    strict = r64 > c64
    eye = jnp.where(r64 == c64, 1.0, 0.0)
    norm_g = normg_ref[...]

    def chunk_step(c, _):
        r0 = pl.multiple_of(c * ch, ch)
        gcs = gc_s[pl.ds(r0, ch), :]
        gct = jnp.concatenate([gcs, jnp.zeros_like(gcs)], axis=0).T
        bts = beta_s[pl.ds(r0, ch), :]
        for h in range(GDN_HEADS):
            q = q_s[h, pl.ds(r0, ch), :]
            k = k_s[h, pl.ds(r0, ch), :]
            v = v_s[h, pl.ds(r0, ch), :]
            gcol = gcs[:, MISC_GA + h:MISC_GA + h + 1]
            grow = gct[MISC_GA + h:MISC_GA + h + 1, 0:ch]
            beta = bts[:, MISC_GB + h:MISC_GB + h + 1]
            decay = jnp.exp(jnp.where(tri, gcol - grow, -jnp.inf))
            kb = k * beta
            lower = jnp.where(strict, _bdot(kb, k, _NT) * decay, 0.0)
            nmat = -lower
            inv = eye + nmat
            for _ in range(5):
                nmat = _dot3(nmat, nmat)
                inv = inv + _dot3(inv, nmat)
            eg = jnp.exp(gcol)
            rhs = jnp.concatenate([v * beta, kb * eg], axis=1)
            sol = _dot3(inv, rhs)
            u, wmat = sol[:, :GDN_DIM], sol[:, GDN_DIM:]
            attn = jnp.where(tri, _bdot(q, k, _NT) * decay, 0.0)
            glast = gcol[ch - 1:ch, :]
            state = state_ref[h]
            v_new = u - _bdot(wmat, state)
            o = _bdot(q * eg, state) + _bdot(attn, v_new)
            k_dec = k * jnp.exp(glast - gcol)
            state_ref[h] = state * jnp.exp(glast) + _bdot(k_dec, v_new, (((0,), (0,)), ((), ())))
            o = o * lax.rsqrt(jnp.mean(o * o, axis=-1, keepdims=True) + RMS_EPS) * norm_g
            zz = z_ref[pl.ds(r0, ch), h * GDN_DIM:(h + 1) * GDN_DIM]
            o_ref[pl.ds(r0, ch), h * GDN_DIM:(h + 1) * GDN_DIM] = (o * (zz * jax.nn.sigmoid(zz))).astype(o_ref.dtype)
        return 0

    lax.fori_loop(0, tt // ch, chunk_step, 0)


def _gdn_mixer(c, conv_w, a_log, dt_bias, norm_g, bsz, seq):
    t = bsz * seq
    tt = min(GDN_TILE, seq)
    nt = seq // tt
    lane_row = lambda vals, off: jnp.zeros((1, LANES), F32).at[0, off:off + GDN_HEADS].set(vals.astype(F32))
    alog = lane_row(a_log, MISC_GA)
    dtb = lane_row(dt_bias, MISC_GA)
    row = lambda w: pl.BlockSpec((tt, w), lambda b, j: (b * nt + j, 0))
    const = lambda a: pl.BlockSpec(a.shape, lambda b, j: (0, 0))
    ng = norm_g.reshape(1, GDN_DIM).astype(F32)
    hs = pltpu.VMEM((GDN_HEADS, tt, GDN_DIM), F32)
    return pl.pallas_call(
        _gdn_body,
        grid=(bsz, nt),
        in_specs=[row(GDN_QKV), row(LANES), row(GDN_HEADS * GDN_DIM), const(conv_w), const(alog), const(dtb), const(ng)],
        out_specs=row(GDN_HEADS * GDN_DIM),
        out_shape=jax.ShapeDtypeStruct((t, GDN_HEADS * GDN_DIM), BF16),
        scratch_shapes=[pltpu.VMEM((GDN_HEADS, GDN_DIM, GDN_DIM), F32), pltpu.VMEM((8, GDN_QKV), F32),
                        hs, hs, hs, pltpu.VMEM((tt, LANES), F32), pltpu.VMEM((tt, LANES), F32)],
        compiler_params=_cparams("parallel", "arbitrary"),
        name="gdn_mixer",
    )(c["gqkv"], c["misc"], c["gz"], conv_w.astype(F32), alog, dtb, ng)


ROW_TILE = 256


def _layer_norm(v, g, b):
    mu = jnp.mean(v, axis=-1, keepdims=True)
    vc = v - mu
    var = jnp.mean(vc * vc, axis=-1, keepdims=True)
    return vc * lax.rsqrt(var + LN_EPS) * g + b


def _merge_body(ya_ref, yb_ref, yc_ref, mg_ref, x_ref, wb_ref, wo_ref, g_ref, b_ref, wrt_ref,
                x1_ref, x1b_ref, xpk_ref, sct_ref, *, alpha):
    d = x_ref.shape[1]
    merged = mg_ref[:, 0:d] * jnp.dot(ya_ref[...], wb_ref[0], preferred_element_type=F32)
    merged = merged + mg_ref[:, d:2 * d] * jnp.dot(yb_ref[...], wb_ref[1], preferred_element_type=F32)
    merged = merged + mg_ref[:, 2 * d:3 * d] * jnp.dot(yc_ref[...], wb_ref[2], preferred_element_type=F32)
    y = alpha * x_ref[...] + jnp.dot(merged.astype(BF16), wo_ref[...], preferred_element_type=F32)
    x1 = _layer_norm(y, g_ref[...], b_ref[...])
    x1_ref[...] = x1
    x1b = x1.astype(BF16)
    x1b_ref[...] = x1b
    xpk_ref[...] = _pack_bf16_pairs(x1b)
    logits_t = lax.dot_general(wrt_ref[...], x1b, _NT, preferred_element_type=F32)
    sct_ref[...] = jax.nn.sigmoid(logits_t)


def _merge_out(ya, yb, yc, mg, x2d, w_branch, w_out, ln_g, ln_b, w_router, alpha):
    t, d = x2d.shape
    tm = ROW_TILE
    row = lambda w: pl.BlockSpec((tm, w), lambda i: (i, 0))
    wb = w_branch.astype(BF16)
    wo = w_out.astype(BF16)
    wrt = w_router.T.astype(BF16)
    g = ln_g.reshape(1, d)
    b = ln_b.reshape(1, d)
    return pl.pallas_call(
        functools.partial(_merge_body, alpha=alpha),
        grid=(t // tm,),
        in_specs=[row(BRANCH_WIDTH)] * 3 + [row(3 * d), row(d),
                  pl.BlockSpec(wb.shape, lambda i: (0, 0, 0)), pl.BlockSpec(wo.shape, lambda i: (0, 0)),
                  pl.BlockSpec((1, d), lambda i: (0, 0)), pl.BlockSpec((1, d), lambda i: (0, 0)),
                  pl.BlockSpec(wrt.shape, lambda i: (0, 0))],
        out_specs=[row(d), row(d), row(d // 2), pl.BlockSpec((N_EXPERTS, tm), lambda i: (0, i))],
        out_shape=[jax.ShapeDtypeStruct((t, d), F32), jax.ShapeDtypeStruct((t, d), BF16),
                   jax.ShapeDtypeStruct((t, d // 2), jnp.uint32), jax.ShapeDtypeStruct((N_EXPERTS, t), F32)],
        compiler_params=_cparams("parallel"),
        name="merge_out_ln",
    )(ya, yb, yc, mg, x2d, wb, wo, g, b, wrt)


ROUTE_TILE = 512


def _route_body(sct_ref, bias_ref, eidx_ref, gate_ref, rank_ref, cnt_ref):
    sc = sct_ref[...]
    n = sc.shape[1]
    per = N_EXPERTS // N_GROUPS
    biased = sc + bias_ref[...]
    b3 = biased.reshape(N_GROUPS, per, n)
    sub = lax.broadcasted_iota(I32, (N_GROUPS, per, n), 1)
    m1 = jnp.max(b3, axis=1, keepdims=True)
    first = jnp.min(jnp.where(b3 == m1, sub, per), axis=1, keepdims=True)
    m2 = jnp.max(jnp.where(sub == first, -jnp.inf, b3), axis=1, keepdims=True)
    gs = (m1 + m2).reshape(N_GROUPS, n)
    gi = lax.broadcasted_iota(I32, (N_GROUPS, n), 0)
    gmask = jnp.zeros((N_GROUPS, n), F32)
    for _ in range(TOPK_GROUPS):
        top = jnp.max(gs, axis=0, keepdims=True)
        pick = gi == jnp.min(jnp.where(gs == top, gi, N_GROUPS), axis=0, keepdims=True)
        gmask = jnp.where(pick, 1.0, gmask)
        gs = jnp.where(pick, -jnp.inf, gs)
    emask = jnp.broadcast_to(gmask.reshape(N_GROUPS, 1, n), (N_GROUPS, per, n)).reshape(N_EXPERTS, n)
    cand = jnp.where(emask > 0.5, biased, -jnp.inf)
    ei = lax.broadcasted_iota(I32, (N_EXPERTS, n), 0)
    ids, gates, picks = [], [], []
    for _ in range(TOP_K):
        top = jnp.max(cand, axis=0, keepdims=True)
        idx = jnp.min(jnp.where(cand == top, ei, N_EXPERTS), axis=0, keepdims=True)
        pick = ei == idx
        ids.append(idx)
        picks.append(pick)
        gates.append(jnp.sum(jnp.where(pick, sc, 0.0), axis=0, keepdims=True))
        cand = jnp.where(pick, -jnp.inf, cand)
    gate = jnp.concatenate(gates, axis=0)
    gate = gate / jnp.sum(gate, axis=0, keepdims=True) * ROUTED_SCALE
    eidx_ref[...] = jnp.concatenate(ids, axis=0)
    gate_ref[...] = gate

    @pl.when(pl.program_id(0) == 0)
    def _():
        cnt_ref[...] = jnp.zeros(cnt_ref.shape, F32)

    sel = jnp.where(cand == -jnp.inf, 1.0, 0.0) * jnp.where(emask > 0.5, 1.0, 0.0)
    ri = lax.broadcasted_iota(I32, (n, n), 0)
    ci = lax.broadcasted_iota(I32, (n, n), 1)
    before = jnp.dot(sel.astype(BF16), jnp.where(ri < ci, 1.0, 0.0).astype(BF16), preferred_element_type=F32)
    pos = before + cnt_ref[...]
    rank_ref[...] = jnp.concatenate(
        [jnp.sum(jnp.where(pk, pos, 0.0), axis=0, keepdims=True) for pk in picks], axis=0).astype(I32)
    cnt_ref[...] = cnt_ref[...] + jnp.sum(sel, axis=1, keepdims=True)


def _route(sct, router_bias):
    e, t = sct.shape
    tn = ROUTE_TILE
    col = lambda r: pl.BlockSpec((r, tn), lambda i: (0, i))
    return pl.pallas_call(
        _route_body,
        grid=(t // tn,),
        in_specs=[col(e), pl.BlockSpec((e, 1), lambda i: (0, 0))],
        out_specs=[col(TOP_K), col(TOP_K), col(TOP_K), pl.BlockSpec((e, 1), lambda i: (0, 0))],
        out_shape=[jax.ShapeDtypeStruct((TOP_K, t), I32), jax.ShapeDtypeStruct((TOP_K, t), F32),
                   jax.ShapeDtypeStruct((TOP_K, t), I32), jax.ShapeDtypeStruct((e, 1), F32)],
        compiler_params=_cparams("arbitrary"),
        name="route_topk",
    )(sct, router_bias.reshape(e, 1).astype(F32))


U32 = jnp.uint32
DISPATCH_TILE = 512


def _pack_bf16_pairs(xb):
    n = xb.shape[1] // 2
    bits = pltpu.bitcast(xb.astype(F32), U32)
    return (bits[:, :n] >> 16) | (bits[:, n:] & jnp.uint32(0xFFFF0000))


def _unpack_bf16_pairs(packed):
    lo = pltpu.bitcast(packed << 16, F32).astype(BF16)
    hi = pltpu.bitcast(packed & jnp.uint32(0xFFFF0000), F32).astype(BF16)
    return lo, hi


def _dispatch_body(dest_hbm, x_hbm, init_hbm, xs_hbm, idx_smem, isem, rsem):
    del init_hbm
    i = pl.program_id(0)
    tn = DISPATCH_TILE
    idx_cp = pltpu.make_async_copy(dest_hbm.at[i], idx_smem, isem)
    idx_cp.start()
    idx_cp.wait()

    def issue(j, _):
        src = x_hbm.at[pl.ds(i * tn + j, 1)]
        for k in range(TOP_K):
            pltpu.make_async_copy(src, xs_hbm.at[pl.ds(idx_smem[k * tn + j], 1)], rsem).start()
        return 0

    lax.fori_loop(0, tn, issue, 0)
    for k in range(TOP_K):
        pltpu.make_async_copy(x_hbm.at[pl.ds(0, tn)], xs_hbm.at[pl.ds(0, tn)], rsem).wait()


def _dispatch(xpk, dest_tiles, n_pad):
    t, w = xpk.shape
    tn = DISPATCH_TILE
    anyspec = pl.BlockSpec(memory_space=pl.ANY)
    return pl.pallas_call(
        _dispatch_body,
        grid=(t // tn,),
        in_specs=[anyspec, anyspec, anyspec],
        out_specs=anyspec,
        out_shape=jax.ShapeDtypeStruct((n_pad, w), xpk.dtype),
        scratch_shapes=[pltpu.SMEM((TOP_K * tn,), I32), pltpu.SemaphoreType.DMA, pltpu.SemaphoreType.DMA],
        input_output_aliases={2: 0},
        compiler_params=_cparams("arbitrary"),
        name="moe_dispatch",
    )(dest_tiles, xpk, jnp.zeros((n_pad, w), xpk.dtype))


def _expert_body(blk_e_ref, nused_ref, xb_ref, wgu_ref, wd_ref, o_ref):
    i = pl.program_id(0)
    half = xb_ref.shape[1]

    @pl.when(i < nused_ref[0])
    def _():
        lo, hi = _unpack_bf16_pairs(xb_ref[...])
        gu = (jnp.dot(lo, wgu_ref[0, :half, :], preferred_element_type=F32)
              + jnp.dot(hi, wgu_ref[0, half:, :], preferred_element_type=F32))
        gg, uu = gu[:, :D_EXPERT], gu[:, D_EXPERT:]
        hmid = (gg * jax.nn.sigmoid(gg) * uu).astype(BF16)
        o_ref[...] = jnp.dot(hmid, wd_ref[0], preferred_element_type=F32)

    @pl.when(i >= nused_ref[0])
    def _():
        o_ref[...] = jnp.zeros(o_ref.shape, o_ref.dtype)


def _expert_ffn(xs, blk_e, n_used, wgu, wd):
    n_pad, half = xs.shape
    d = 2 * half
    nb = n_pad // EXPERT_BLOCK
    grid_spec = pltpu.PrefetchScalarGridSpec(
        num_scalar_prefetch=2,
        grid=(nb,),
        in_specs=[pl.BlockSpec((EXPERT_BLOCK, half), lambda i, be, nu: (i, 0)),
                  pl.BlockSpec((1, d, 2 * D_EXPERT), lambda i, be, nu: (be[i], 0, 0)),
                  pl.BlockSpec((1, D_EXPERT, d), lambda i, be, nu: (be[i], 0, 0))],
        out_specs=pl.BlockSpec((EXPERT_BLOCK, d), lambda i, be, nu: (i, 0)),
    )
    return pl.pallas_call(
        _expert_body,
        grid_spec=grid_spec,
        out_shape=jax.ShapeDtypeStruct((n_pad, d), F32),
        compiler_params=_cparams("arbitrary"),
        name="expert_ffn",
    )(blk_e, n_used, xs, wgu, wd)


def _moe_plan(eidx_t, rank_t, counts, n_tok):
    n_rows = n_tok * TOP_K
    n_blocks = -(-n_rows // EXPERT_BLOCK) + N_EXPERTS
    cnt = counts.reshape(-1).astype(I32)
    padded = (cnt + EXPERT_BLOCK - 1) // EXPERT_BLOCK * EXPERT_BLOCK
    pad_end = jnp.cumsum(padded)
    pad_start = pad_end - padded
    dest = pad_start[eidx_t] + rank_t
    blk_e = jnp.minimum(jnp.searchsorted(pad_end, jnp.arange(n_blocks, dtype=I32) * EXPERT_BLOCK, side='right'),
                        N_EXPERTS - 1).astype(I32)
    n_used = (pad_end[-1] // EXPERT_BLOCK).astype(I32).reshape(1)
    return dest, blk_e, n_used, n_blocks * EXPERT_BLOCK


def _tile_indices(dest, tn):
    k, t = dest.shape
    return dest.reshape(k, t // tn, tn).transpose(1, 0, 2).reshape(t // tn, k * tn)


def _ffn_out_body(dest_hbm, ye_hbm, gate_ref, x1_ref, x1b_ref, p_ref, wsgu_ref, wsd_ref, wpg_ref, wpp_ref,
                  g_ref, b_ref, o_ref, idx_smem, rows_ref, isem, rsem, *, alpha):
    i = pl.program_id(0)
    n_tiles = pl.num_programs(0)
    tm = x1_ref.shape[0]
    slot = i % 2

    def gather_tile(tile, s):
        idx_cp = pltpu.make_async_copy(dest_hbm.at[tile], idx_smem.at[s], isem)
        idx_cp.start()
        idx_cp.wait()

        def issue(j, _):
            for k in range(TOP_K):
                pltpu.make_async_copy(ye_hbm.at[pl.ds(idx_smem[s, k * tm + j], 1)],
                                      rows_ref.at[s, k, pl.ds(j, 1)], rsem.at[s]).start()
            return 0

        lax.fori_loop(0, tm, issue, 0)

    @pl.when(i == 0)
    def _():
        gather_tile(0, 0)

    @pl.when(i + 1 < n_tiles)
    def _():
        gather_tile(i + 1, 1 - slot)

    xb = x1b_ref[...]
    su = jnp.dot(xb, wsgu_ref[...], preferred_element_type=F32)
    sg, uu = su[:, :D_SHARED], su[:, D_SHARED:]
    shared = jnp.dot((sg * jax.nn.sigmoid(sg) * uu).astype(BF16), wsd_ref[...], preferred_element_type=F32)
    pg = jax.nn.sigmoid(jnp.dot(xb, wpg_ref[...], preferred_element_type=F32))
    ple = pg * jnp.dot(p_ref[...].astype(BF16), wpp_ref[...], preferred_element_type=F32)

    g8 = gate_ref[...]
    gate_cols = jnp.concatenate([g8, jnp.zeros((LANES - TOP_K, tm), F32)], axis=0).T
    for k in range(TOP_K):
        pltpu.make_async_copy(ye_hbm.at[pl.ds(0, tm)], rows_ref.at[slot, k], rsem.at[slot]).wait()
    routed = gate_cols[:, 0:1] * rows_ref[slot, 0]
    for k in range(1, TOP_K):
        routed = routed + gate_cols[:, k:k + 1] * rows_ref[slot, k]
    y = alpha * x1_ref[...] + (routed + shared) + ple
    o_ref[...] = _layer_norm(y, g_ref[...], b_ref[...])


def _ffn_out(dest_tiles, ye, gate_t, x1, x1b, p2d, w_sgu, w_sd, w_pg, w_pp, ln_g, ln_b, alpha):
    t, d = x1.shape
    tm = ROW_TILE
    row = lambda w: pl.BlockSpec((tm, w), lambda i: (i, 0))
    const = lambda a: pl.BlockSpec(a.shape, lambda i: (0, 0))
    anyspec = pl.BlockSpec(memory_space=pl.ANY)
    ws = [w.astype(BF16) for w in (w_sgu, w_sd, w_pg, w_pp)]
    g = ln_g.reshape(1, d)
    b = ln_b.reshape(1, d)
    return pl.pallas_call(
        functools.partial(_ffn_out_body, alpha=alpha),
        grid=(t // tm,),
        in_specs=[anyspec, anyspec, pl.BlockSpec((TOP_K, tm), lambda i: (0, i)), row(d), row(d), row(PLE_DIM)]
                 + [const(w) for w in ws] + [const(g), const(b)],
        out_specs=row(d),
        out_shape=jax.ShapeDtypeStruct((t, d), F32),
        scratch_shapes=[pltpu.SMEM((2, TOP_K * tm), I32), pltpu.VMEM((2, TOP_K, tm, d), F32),
                        pltpu.SemaphoreType.DMA, pltpu.SemaphoreType.DMA((2,))],
        compiler_params=_cparams("arbitrary"),
        name="ffn_out_ln",
    )(dest_tiles, ye, gate_t, x1, x1b, p2d, *ws, g, b)


def kernel(x, p, w_in, w_dsa_uv, w_cmp_k, pe_cmp_k, w_cmp_v, pe_cmp_v, conv_w, a_log, dt_bias, gdn_norm, w_branch, w_out, ln1_g, ln1_b, w_router, router_bias, w_gate_up, w_down, w_sh_gate_up, w_sh_down, w_ple_proj, w_ple_gate, ln2_g, ln2_b):
    bsz, seq, d = x.shape
    depth = w_in.shape[0]
    alpha = (2 * depth) ** 0.25
    t = bsz * seq
    tabs64, _ = _rope_lane_tables(seq, HEAD_DIM)
    tabs32, _ = _rope_lane_tables(seq, IDX_DIM)
    x2d = x.reshape(t, d)
    for i in range(depth):
        c = _in_projection(x2d, _split_w_in(w_in[i]), tabs64, tabs32, seq)
        ya = _dsa_attention(c, w_dsa_uv[i], bsz, seq)
        kcmp, vcmpt = _nsa_compress(c, w_cmp_k[i], pe_cmp_k[i], w_cmp_v[i], pe_cmp_v[i], bsz, seq)
        yb = _nsa_attention(c, kcmp, vcmpt, bsz, seq)
        yc = _gdn_mixer(c, conv_w[i], a_log[i], dt_bias[i], gdn_norm[i], bsz, seq)
        x1, x1b, xpk, sct = _merge_out(ya, yb, yc, c["mg"], x2d, w_branch[i], w_out[i], ln1_g[i], ln1_b[i],
                                       w_router[i], alpha)
        eidx_t, gate_t, rank_t, counts = _route(sct, router_bias[i])
        dest, blk_e, n_used, n_pad = _moe_plan(eidx_t, rank_t, counts, t)
        xs = _dispatch(xpk, _tile_indices(dest, DISPATCH_TILE), n_pad)
        ye = _expert_ffn(xs, blk_e, n_used, w_gate_up[i].astype(BF16), w_down[i].astype(BF16))
        x2d = _ffn_out(_tile_indices(dest, ROW_TILE), ye, gate_t, x1, x1b, p[i].reshape(t, PLE_DIM),
                       w_sh_gate_up[i], w_sh_down[i], w_ple_gate[i], w_ple_proj[i], ln2_g[i], ln2_b[i], alpha)
    return x2d.reshape(bsz, seq, d)
```

```python
import functools
import math

import numpy as np
import jax
import jax.numpy as jnp
from jax import lax
from jax.experimental import pallas as pl
from jax.experimental.pallas import tpu as pltpu

F32 = jnp.float32
BF16 = jnp.bfloat16
I32 = jnp.int32

HEAD_DIM = 64
ROPE_THETA = 500000.0
ROPE_FRACTION = 4
DSA_HEADS = 8
DSA_LATENT = 64
IDX_HEADS = 4
IDX_DIM = 32
DSA_TOPK = 256
NSA_HEADS = 8
CMP_BLOCK = 32
CMP_STRIDE = 16
SEL_BLOCK = 32
N_SEL_BLOCKS = 8
WINDOW = 256
FORCED_SCORE = 1.0e4
GDN_HEADS = 4
GDN_DIM = 128
CONV_WIDTH = 4
GDN_CHUNK = 64
BRANCH_WIDTH = 512
N_BRANCHES = 3
GDN_QKV = 3 * GDN_HEADS * GDN_DIM
N_EXPERTS = 64
TOP_K = 8
N_GROUPS = 8
TOPK_GROUPS = 4
D_EXPERT = 256
D_SHARED = 256
ROUTED_SCALE = 2.5
EXPERT_BLOCK = 512
PLE_DIM = 256
LN_EPS = 1e-5
RMS_EPS = 1e-6

IN_SPLITS = (
    ("dsa_q", DSA_HEADS * DSA_LATENT), ("dsa_kv", DSA_LATENT),
    ("idx_q", IDX_HEADS * IDX_DIM), ("idx_k", IDX_DIM), ("idx_w", IDX_HEADS),
    ("nsa_q", NSA_HEADS * HEAD_DIM),
    ("nsa_kc", HEAD_DIM), ("nsa_vc", HEAD_DIM),
    ("nsa_ks", HEAD_DIM), ("nsa_vs", HEAD_DIM),
    ("nsa_kw", HEAD_DIM), ("nsa_vw", HEAD_DIM),
    ("nsa_g", NSA_HEADS * 3),
    ("gdn_qkv", GDN_QKV), ("gdn_a", GDN_HEADS), ("gdn_b", GDN_HEADS), ("gdn_z", GDN_HEADS * GDN_DIM),
    ("merge_g", N_BRANCHES * 1024),
)

LANES = 128
Q_TILE = 128
KEY_CHUNK = 256
VMEM_LIMIT = 56 * 1024 * 1024
INT_MIN = -2 ** 31
NEG_BIG = -1e30


def _cparams(*sem):
    return pltpu.CompilerParams(dimension_semantics=sem, vmem_limit_bytes=VMEM_LIMIT)


def _rope_lane_tables(seq, head_dim):
    rot = head_dim // ROPE_FRACTION
    half = rot // 2
    inv_freq = ROPE_THETA ** (-(jnp.arange(half, dtype=F32) * 2.0 / rot))
    ang = jnp.arange(seq, dtype=F32)[:, None] * inv_freq[None, :]
    cos, sin = jnp.cos(ang), jnp.sin(ang)
    one = jnp.ones((seq, head_dim - rot), F32)
    c = jnp.concatenate([cos, cos, one], axis=-1)
    sa = jnp.concatenate([-sin, jnp.zeros((seq, head_dim - half), F32)], axis=-1)
    sb = jnp.concatenate([jnp.zeros((seq, half), F32), sin, jnp.zeros((seq, head_dim - rot), F32)], axis=-1)
    rep = LANES // head_dim
    return tuple(jnp.tile(a, (1, rep)) for a in (c, sa, sb)), half


def _rope_apply(y, c, sa, sb, half):
    n = y.shape[-1]
    rep = n // LANES
    if rep > 1:
        c, sa, sb = (jnp.concatenate([a] * rep, axis=-1) for a in (c, sa, sb))
    up = pltpu.roll(y, n - half, 1)
    dn = pltpu.roll(y, half, 1)
    return y * c + up * sa + dn * sb


IN_TILE = 256


def _inproj_body(x_ref, w64_ref, w32_ref, wpl_ref, wvt_ref, wmg_ref,
                 c64_ref, sa64_ref, sb64_ref, c32_ref, sa32_ref, sb32_ref,
                 dq_ref, nq_ref, krot_ref, nkc_ref, nks_ref, nkw_ref, iq_ref, ik_ref,
                 nvc_ref, misc_ref, gz_ref, gqkv_ref, vt_ref, mg_ref):
    xb = x_ref[...].astype(BF16)
    c64, sa64, sb64 = c64_ref[...], sa64_ref[...], sb64_ref[...]
    half64 = HEAD_DIM // ROPE_FRACTION // 2
    half32 = IDX_DIM // ROPE_FRACTION // 2

    def proj(w_ref, lo, hi):
        return jnp.dot(xb, w_ref[:, lo:hi], preferred_element_type=F32)

    dq_ref[...] = _rope_apply(proj(w64_ref, 0, 512), c64, sa64, sb64, half64).astype(BF16)
    nq_ref[...] = _rope_apply(proj(w64_ref, 512, 1024), c64, sa64, sb64, half64).astype(BF16)
    k1 = _rope_apply(proj(w64_ref, 1024, 1152), c64, sa64, sb64, half64).astype(BF16)
    krot_ref[...] = k1[:, :64]
    nkc_ref[...] = k1[:, 64:]
    k2 = _rope_apply(proj(w64_ref, 1152, 1280), c64, sa64, sb64, half64).astype(BF16)
    nks_ref[...] = k2[:, :64]
    nkw_ref[...] = k2[:, 64:]
    c32, sa32, sb32 = c32_ref[...], sa32_ref[...], sb32_ref[...]
    iq_ref[...] = _rope_apply(proj(w32_ref, 0, 128), c32, sa32, sb32, half32).astype(BF16)
    ik = _rope_apply(proj(w32_ref, 128, 256), c32, sa32, sb32, half32).astype(BF16)
    ik_ref[...] = ik[:, :IDX_DIM]
    pl0 = proj(wpl_ref, 0, 128)
    nvc_ref[...] = pl0[:, :64].astype(BF16)
    misc_ref[...] = proj(wpl_ref, 128, 256)
    gz_ref[...] = proj(wpl_ref, 256, 768)
    for j in range(3):
        gqkv_ref[:, j * 512:(j + 1) * 512] = proj(wpl_ref, 768 + j * 512, 768 + (j + 1) * 512)
    vt_ref[...] = lax.dot_general(wvt_ref[...], xb, (((1,), (1,)), ((), ())),
                                  preferred_element_type=F32).astype(BF16)
    for j in range(6):
        g = proj(wmg_ref, j * 512, (j + 1) * 512)
        mg_ref[:, j * 512:(j + 1) * 512] = jax.nn.sigmoid(g)


def _split_w_in(w):
    cols = {}
    off = 0
    for name, size in IN_SPLITS:
        cols[name] = w[:, off:off + size]
        off += size
    d = w.shape[0]
    z = lambda n: jnp.zeros((d, n), w.dtype)
    w64 = jnp.concatenate([cols["dsa_q"], cols["nsa_q"], cols["dsa_kv"], cols["nsa_kc"],
                           cols["nsa_ks"], cols["nsa_kw"]], axis=1)
    w32 = jnp.concatenate([cols["idx_q"], cols["idx_k"], z(LANES - IDX_DIM)], axis=1)
    misc = jnp.concatenate([cols["idx_w"], cols["gdn_a"], cols["gdn_b"], cols["nsa_g"]], axis=1)
    misc = jnp.concatenate([misc, z(LANES - misc.shape[1])], axis=1)
    wpl = jnp.concatenate([cols["nsa_vc"], z(64), misc, cols["gdn_z"], cols["gdn_qkv"]], axis=1)
    wvt = jnp.concatenate([cols["dsa_kv"], cols["nsa_vs"], cols["nsa_vw"], z(64)], axis=1).T
    return [a.astype(BF16) for a in (w64, w32, wpl, wvt, cols["merge_g"])]


MISC_IW = 0
MISC_GA = IDX_HEADS
MISC_GB = MISC_GA + GDN_HEADS
MISC_NG = MISC_GB + GDN_HEADS


def _in_projection(x2d, wparts, tabs64, tabs32, seq):
    t, d = x2d.shape
    tm = IN_TILE
    nt = seq // tm
    w64, w32, wpl, wvt, wmg = wparts
    row = lambda w: pl.BlockSpec((tm, w), lambda i: (i, 0))
    full = lambda a: pl.BlockSpec(a.shape, lambda i: (0, 0))
    tab = pl.BlockSpec((tm, LANES), lambda i: (i % nt, 0))
    outs = [
        ("dq", 512, BF16), ("nq", 512, BF16), ("krot", 64, BF16), ("nkc", 64, BF16), ("nks", 64, BF16),
        ("nkw", 64, BF16), ("iq", 128, BF16), ("ik", IDX_DIM, BF16), ("nvc", 64, BF16),
        ("misc", 128, F32), ("gz", 512, F32), ("gqkv", GDN_QKV, F32),
    ]
    out_shape = [jax.ShapeDtypeStruct((t, w), dt) for _, w, dt in outs]
    out_specs = [row(w) for _, w, _ in outs]
    out_shape += [jax.ShapeDtypeStruct((256, t), BF16), jax.ShapeDtypeStruct((t, 3 * d), F32)]
    out_specs += [pl.BlockSpec((256, tm), lambda i: (0, i)), row(3 * d)]
    res = pl.pallas_call(
        _inproj_body,
        grid=(t // tm,),
        in_specs=[row(d), full(w64), full(w32), full(wpl), full(wvt), full(wmg)] + [tab] * 6,
        out_specs=out_specs,
        out_shape=out_shape,
        compiler_params=_cparams("parallel"),
        name="in_projection",
    )(x2d, w64, w32, wpl, wvt, wmg, *tabs64, *tabs32)
    names = [n for n, _, _ in outs] + ["vt", "mg"]
    return dict(zip(names, res))


_NT = (((1,), (1,)), ((), ()))


def _stack_heads(q, n_heads, width):
    return jnp.concatenate([q[:, h * width:(h + 1) * width] for h in range(n_heads)], axis=0)


def _unstack_heads_t(o_t, n_heads):
    q = o_t.shape[1] // n_heads
    rows = jnp.concatenate([o_t[:, h * q:(h + 1) * q] for h in range(n_heads)], axis=0)
    return rows.T


def _flash_chunks(qs, k_ref, vt_ref, lo, hi, rows, mask_fn, carry0, acc_ref, m_ref, l_ref, n_heads):
    m_ref[...] = jnp.full(m_ref.shape, NEG_BIG, F32)
    l_ref[...] = jnp.zeros(l_ref.shape, F32)
    acc_ref[...] = jnp.zeros(acc_ref.shape, F32)

    def body(c, carry):
        r0 = pl.multiple_of(c * rows, rows)
        bias, carry = mask_fn(r0, carry)
        kc = k_ref[pl.ds(r0, rows), :]
        s = lax.dot_general(kc, qs, _NT, preferred_element_type=F32)
        s = s + jnp.concatenate([bias] * n_heads, axis=1)
        m_old = m_ref[...]
        m_new = jnp.maximum(m_old, jnp.max(s, axis=0, keepdims=True))
        alpha = jnp.exp(m_old - m_new)
        p = jnp.exp(s - m_new)
        l_ref[...] = alpha * l_ref[...] + jnp.sum(p, axis=0, keepdims=True)
        pv = jnp.dot(vt_ref[:, pl.ds(r0, rows)], p.astype(BF16), preferred_element_type=F32)
        acc_ref[...] = acc_ref[...] * alpha + pv
        m_ref[...] = m_new
        return carry

    lax.fori_loop(lo, hi, body, carry0)
    return acc_ref[...] / jnp.maximum(l_ref[...], 1e-30)


def _dsa_body(iq_ref, dq_ref, misc_ref, ik_ref, krot_ref, vt_ref, wuvt_ref, o_ref,
              keys_ref, acc_ref, m_ref, l_ref, *, n_sel):
    rows = KEY_CHUNK
    q0 = pl.program_id(1) * Q_TILE
    nch = (q0 + Q_TILE + rows - 1) // rows
    tq = q0 + lax.broadcasted_iota(I32, (1, Q_TILE), 1)
    misc_t = misc_ref[...].T
    coef = (IDX_HEADS ** -0.5) * (IDX_DIM ** -0.5)
    iq = iq_ref[...]
    int_min = jnp.int32(INT_MIN)

    def score_chunk(c, _):
        r0 = pl.multiple_of(c * rows, rows)
        ikc = ik_ref[pl.ds(r0, rows), :]
        acc = jnp.zeros((rows, Q_TILE), F32)
        for h in range(IDX_HEADS):
            s = lax.dot_general(ikc, iq[:, h * IDX_DIM:(h + 1) * IDX_DIM], _NT, preferred_element_type=F32)
            acc = acc + jnp.maximum(s, 0.0) * (misc_t[MISC_IW + h:MISC_IW + h + 1, :] * coef)
        acc = jnp.where(acc == 0.0, 0.0, acc)
        bits = pltpu.bitcast(acc, I32)
        key = bits ^ ((bits >> 31) & jnp.int32(0x7FFFFFFF))
        kpos = r0 + lax.broadcasted_iota(I32, (rows, Q_TILE), 0)
        keys_ref[pl.ds(r0, rows), :] = jnp.where(kpos <= tq, key, int_min)
        return 0

    big = 2 * rows
    nbig = (q0 + Q_TILE + big - 1) // big
    lax.fori_loop(0, 2 * nbig, score_chunk, 0)

    def count(pred):
        def body(c, acc):
            r0 = pl.multiple_of(c * big, big)
            hit = jnp.where(pred(keys_ref[pl.ds(r0, big), :]), 1, 0).astype(I32)
            return acc + jnp.sum(hit.reshape(big // 8, 8, Q_TILE), axis=0)
        acc = lax.fori_loop(0, nbig, body, jnp.zeros((8, Q_TILE), I32))
        return jnp.sum(acc, axis=0, keepdims=True)

    kq = jnp.minimum(tq + 1, n_sel)

    def bit_step(i, tu):
        cand_u = tu | lax.shift_left(jnp.int32(1), 31 - i)
        cand = cand_u ^ int_min
        cnt = count(lambda kk: kk >= cand)
        return jnp.where(cnt >= kq, cand_u, tu)

    thr = lax.fori_loop(0, 32, bit_step, jnp.zeros((1, Q_TILE), I32)) ^ int_min
    need = (kq - count(lambda kk: kk > thr)).astype(F32)

    ri = lax.broadcasted_iota(I32, (rows, rows), 0)
    ci = lax.broadcasted_iota(I32, (rows, rows), 1)
    tri = jnp.where(ri > ci, 1.0, 0.0).astype(BF16)

    def mask_fn(r0, seen):
        kk = keys_ref[pl.ds(r0, rows), :]
        tie = jnp.where(kk == thr, 1.0, 0.0)
        before = jnp.dot(tri, tie.astype(BF16), preferred_element_type=F32) + seen
        take = (kk > thr) | ((kk == thr) & (before < need))
        return jnp.where(take, 0.0, -jnp.inf), seen + jnp.sum(tie, axis=0, keepdims=True)

    qs = _stack_heads(dq_ref[...], DSA_HEADS, DSA_LATENT) * jnp.asarray(DSA_LATENT ** -0.5, BF16)
    o_lat = _flash_chunks(qs, krot_ref, vt_ref, 0, nch, rows, mask_fn, jnp.zeros((1, Q_TILE), F32),
                          acc_ref, m_ref, l_ref, DSA_HEADS)
    outs = []
    for h in range(DSA_HEADS):
        oh = o_lat[:, h * Q_TILE:(h + 1) * Q_TILE].astype(BF16)
        outs.append(jnp.dot(wuvt_ref[h], oh, preferred_element_type=F32))
    o_ref[...] = jnp.concatenate(outs, axis=0).T.astype(o_ref.dtype)


def _dsa_attention(c, w_uv, bsz, seq):
    t = bsz * seq
    nq = seq // Q_TILE
    n_sel = min(DSA_TOPK, seq // 4)
    wuvt = jnp.swapaxes(w_uv, 1, 2).astype(BF16)
    qrow = lambda w: pl.BlockSpec((Q_TILE, w), lambda b, n: (b * nq + n, 0))
    seqrow = lambda w: pl.BlockSpec((seq, w), lambda b, n: (b, 0))
    nqh = DSA_HEADS * Q_TILE
    return pl.pallas_call(
        functools.partial(_dsa_body, n_sel=n_sel),
        grid=(bsz, nq),
        in_specs=[qrow(128), qrow(512), qrow(128), seqrow(IDX_DIM), seqrow(DSA_LATENT),
                  pl.BlockSpec((DSA_LATENT, seq), lambda b, n: (0, b)),
                  pl.BlockSpec(wuvt.shape, lambda b, n: (0, 0, 0))],
        out_specs=qrow(512),
        out_shape=jax.ShapeDtypeStruct((t, 512), BF16),
        scratch_shapes=[pltpu.VMEM((seq, Q_TILE), I32), pltpu.VMEM((DSA_LATENT, nqh), F32),
                        pltpu.VMEM((1, nqh), F32), pltpu.VMEM((1, nqh), F32)],
        compiler_params=_cparams("parallel", "arbitrary"),
        name="dsa_attention",
    )(c["iq"], c["dq"], c["misc"], c["ik"], c["krot"], c["vt"], wuvt)


CMP_PER_ROW = CMP_STRIDE * HEAD_DIM


def _nsa_cmp_body(kc_ref, vc_ref, wk_ref, wv_ref, pek_ref, pev_ref, kcmp_ref, vcmpt_ref):
    n = kc_ref.shape[0]

    def compress(x_ref, w_ref, pe_ref):
        xf = x_ref[...].astype(F32)
        lo = jnp.dot((xf + pe_ref[0:1, :]).astype(BF16), w_ref[0], preferred_element_type=F32)
        hi = jnp.dot((xf + pe_ref[1:2, :]).astype(BF16), w_ref[1], preferred_element_type=F32)
        return lo + pltpu.roll(hi, n - 1, 0)

    kcmp_ref[...] = compress(kc_ref, wk_ref, pek_ref).astype(BF16)
    vc = compress(vc_ref, wv_ref, pev_ref)
    vpad = jnp.concatenate([vc, jnp.zeros_like(vc)], axis=1)
    vcmpt_ref[...] = vpad.T[:HEAD_DIM, :].astype(BF16)


def _nsa_compress(c, w_ck, pe_k, w_cv, pe_v, bsz, seq):
    ng = seq // CMP_STRIDE
    kc2 = c["nkc"].reshape(bsz * ng, CMP_PER_ROW)
    vc2 = c["nvc"].reshape(bsz * ng, CMP_PER_ROW)
    wk = w_ck.reshape(2, CMP_PER_ROW, HEAD_DIM).astype(BF16)
    wv = w_cv.reshape(2, CMP_PER_ROW, HEAD_DIM).astype(BF16)
    pek = pe_k.reshape(2, CMP_PER_ROW)
    pev = pe_v.reshape(2, CMP_PER_ROW)
    grp = pl.BlockSpec((ng, CMP_PER_ROW), lambda b: (b, 0))
    wsp = pl.BlockSpec((2, CMP_PER_ROW, HEAD_DIM), lambda b: (0, 0, 0))
    psp = pl.BlockSpec((2, CMP_PER_ROW), lambda b: (0, 0))
    return pl.pallas_call(
        _nsa_cmp_body,
        grid=(bsz,),
        in_specs=[grp, grp, wsp, wsp, psp, psp],
        out_specs=[pl.BlockSpec((ng, HEAD_DIM), lambda b: (b, 0)), pl.BlockSpec((HEAD_DIM, ng), lambda b: (0, b))],
        out_shape=[jax.ShapeDtypeStruct((bsz * ng, HEAD_DIM), BF16), jax.ShapeDtypeStruct((HEAD_DIM, bsz * ng), BF16)],
        compiler_params=_cparams("parallel"),
        name="nsa_compress",
    )(kc2, vc2, wk, wv, pek, pev)


def _nsa_body(nq_ref, misc_ref, kcmp_ref, vcmpt_ref, ovl_ref, exp_ref, ks_ref, vst_ref, kw_ref, vwt_ref,
              o_ref, sel_ref, acc_ref, m_ref, l_ref, *, n_sel, n_cmp):
    rows = KEY_CHUNK
    n = pl.program_id(1)
    q0 = n * Q_TILE
    tq = q0 + lax.broadcasted_iota(I32, (1, Q_TILE), 1)
    qs = _stack_heads(nq_ref[...], NSA_HEADS, HEAD_DIM) * jnp.asarray(HEAD_DIM ** -0.5, BF16)
    gate_t = jax.nn.sigmoid(misc_ref[...]).T

    def gate_row(branch):
        return jnp.concatenate([gate_t[MISC_NG + 3 * h + branch:MISC_NG + 3 * h + branch + 1, :]
                                for h in range(NSA_HEADS)], axis=1)

    ng = kcmp_ref.shape[0]
    jrow = lax.broadcasted_iota(I32, (ng, Q_TILE), 0)
    ok_c = (jrow * CMP_STRIDE + (CMP_BLOCK - 1) <= tq) & (jrow < n_cmp)
    bias_c = jnp.where(ok_c, 0.0, -jnp.inf)
    s = lax.dot_general(kcmp_ref[...], qs, _NT, preferred_element_type=F32)
    s = s + jnp.concatenate([bias_c] * NSA_HEADS, axis=1)
    mx = jnp.maximum(jnp.max(s, axis=0, keepdims=True), NEG_BIG)
    e = jnp.exp(s - mx)
    pc = (e / jnp.maximum(jnp.sum(e, axis=0, keepdims=True), 1e-30)).astype(BF16)
    out = gate_row(0) * jnp.dot(vcmpt_ref[...], pc, preferred_element_type=F32)

    imp8 = jnp.dot(ovl_ref[...], pc, preferred_element_type=F32)
    imp = imp8[:, 0:Q_TILE]
    for h in range(1, NSA_HEADS):
        imp = imp + imp8[:, h * Q_TILE:(h + 1) * Q_TILE]
    n_blk = imp.shape[0]
    blk = lax.broadcasted_iota(I32, (n_blk, Q_TILE), 0)
    cur = tq // SEL_BLOCK
    forced = (blk == 0) | (blk == cur) | (blk == cur - 1)
    imp = jnp.where(blk <= cur, jnp.where(forced, FORCED_SCORE, imp), -jnp.inf)
    chosen = jnp.zeros((n_blk, Q_TILE), F32)
    for _ in range(n_sel):
        top = jnp.max(imp, axis=0, keepdims=True)
        first = jnp.min(jnp.where(imp == top, blk, n_blk), axis=0, keepdims=True)
        pick = (blk == first) & (top > -jnp.inf)
        chosen = jnp.where(pick, 1.0, chosen)
        imp = jnp.where(pick, -jnp.inf, imp)
    sel_ref[...] = chosen.astype(BF16)

    def sel_mask(r0, carry):
        hit = jnp.dot(exp_ref[pl.ds(r0, rows), :], sel_ref[...], preferred_element_type=F32)
        kpos = r0 + lax.broadcasted_iota(I32, (rows, Q_TILE), 0)
        return jnp.where((hit > 0.5) & (kpos <= tq), 0.0, -jnp.inf), carry

    nch = (q0 + Q_TILE + rows - 1) // rows
    o_s = _flash_chunks(qs, ks_ref, vst_ref, 0, nch, rows, sel_mask, 0, acc_ref, m_ref, l_ref, NSA_HEADS)
    out = out + gate_row(1) * o_s

    def win_mask(r0, carry):
        kpos = r0 + lax.broadcasted_iota(I32, (Q_TILE, Q_TILE), 0)
        return jnp.where((kpos <= tq) & (kpos > tq - WINDOW), 0.0, -jnp.inf), carry

    lo = jnp.maximum(n - WINDOW // Q_TILE, 0)
    o_w = _flash_chunks(qs, kw_ref, vwt_ref, lo, n + 1, Q_TILE, win_mask, 0, acc_ref, m_ref, l_ref, NSA_HEADS)
    out = out + gate_row(2) * o_w
    o_ref[...] = _unstack_heads_t(out, NSA_HEADS).astype(o_ref.dtype)


def _nsa_attention(c, kcmp, vcmpt, bsz, seq):
    t = bsz * seq
    nq = seq // Q_TILE
    ng = seq // CMP_STRIDE
    n_cmp = (seq - CMP_BLOCK) // CMP_STRIDE + 1
    n_blk = seq // SEL_BLOCK
    n_sel = min(N_SEL_BLOCKS, n_blk)
    j = np.arange(ng)[None, :] * CMP_STRIDE
    b0 = np.arange(n_blk)[:, None] * SEL_BLOCK
    ovl = ((j <= b0 + SEL_BLOCK - 1) & (j + CMP_BLOCK - 1 >= b0) & (np.arange(ng)[None, :] < n_cmp))
    ovl = jnp.asarray(ovl.astype(np.float32), BF16)
    expand = jnp.asarray((np.arange(seq)[:, None] // SEL_BLOCK == np.arange(n_blk)[None, :]).astype(np.float32), BF16)
    qrow = lambda w: pl.BlockSpec((Q_TILE, w), lambda b, n: (b * nq + n, 0))
    seqrow = lambda w: pl.BlockSpec((seq, w), lambda b, n: (b, 0))
    vrow = lambda r: pl.BlockSpec((HEAD_DIM, seq), lambda b, n: (r, b))
    const = lambda a: pl.BlockSpec(a.shape, lambda b, n: (0, 0))
    nqh = NSA_HEADS * Q_TILE
    return pl.pallas_call(
        functools.partial(_nsa_body, n_sel=n_sel, n_cmp=n_cmp),
        grid=(bsz, nq),
        in_specs=[qrow(512), qrow(128),
                  pl.BlockSpec((ng, HEAD_DIM), lambda b, n: (b, 0)), pl.BlockSpec((HEAD_DIM, ng), lambda b, n: (0, b)),
                  const(ovl), const(expand), seqrow(HEAD_DIM), vrow(1), seqrow(HEAD_DIM), vrow(2)],
        out_specs=qrow(512),
        out_shape=jax.ShapeDtypeStruct((t, 512), BF16),
        scratch_shapes=[pltpu.VMEM((n_blk, Q_TILE), BF16), pltpu.VMEM((HEAD_DIM, nqh), F32),
                        pltpu.VMEM((1, nqh), F32), pltpu.VMEM((1, nqh), F32)],
        compiler_params=_cparams("parallel", "arbitrary"),
        name="nsa_attention",
    )(c["nq"], c["misc"], kcmp, vcmpt, ovl, expand, c["nks"], c["vt"], c["nkw"], c["vt"])


GDN_TILE = 512


def _split_bf16(a):
    hi = a.astype(BF16)
    return hi, (a - hi.astype(F32)).astype(BF16)


def _dot3(a, b, dims=None):
    ah, al = _split_bf16(a)
    bh, bl = _split_bf16(b)
    if dims is None:
        f = lambda u, v: jnp.dot(u, v, preferred_element_type=F32)
    else:
        f = lambda u, v: lax.dot_general(u, v, dims, preferred_element_type=F32)
    return f(ah, bh) + (f(ah, bl) + f(al, bh))


def _bdot(a, b, dims=None):
    a, b = a.astype(BF16), b.astype(BF16)
    if dims is None:
        return jnp.dot(a, b, preferred_element_type=F32)
    return lax.dot_general(a, b, dims, preferred_element_type=F32)


def _gdn_body(qkv_ref, misc_ref, z_ref, convw_ref, alog_ref, dtb_ref, normg_ref, o_ref,
              state_ref, tail_ref, q_s, k_s, v_s, gc_s, beta_s):
    tt = qkv_ref.shape[0]
    ch = GDN_CHUNK
    hw = GDN_HEADS * GDN_DIM

    @pl.when(pl.program_id(1) == 0)
    def _():
        state_ref[...] = jnp.zeros(state_ref.shape, F32)
        tail_ref[...] = jnp.zeros(tail_ref.shape, F32)

    x = qkv_ref[...]
    w = convw_ref[...]
    tail = tail_ref[...]
    x8 = x[0:8, :]
    row8 = lax.broadcasted_iota(I32, (8, 1), 0)
    acc = x * w[CONV_WIDTH - 1:CONV_WIDTH, :]
    acc8 = x8 * w[CONV_WIDTH - 1:CONV_WIDTH, :]
    for k in range(1, CONV_WIDTH):
        wk = w[CONV_WIDTH - 1 - k:CONV_WIDTH - k, :]
        acc = acc + pltpu.roll(x, k, 0) * wk
        acc8 = acc8 + jnp.where(row8 < k, pltpu.roll(tail, k, 0), pltpu.roll(x8, k, 0)) * wk
    tail_ref[...] = x[tt - 8:tt, :]
    rowt = lax.broadcasted_iota(I32, (tt, 1), 0)
    acc = jnp.where(rowt < 8, jnp.concatenate([acc8, acc[8:, :]], axis=0), acc)
    xc = acc * jax.nn.sigmoid(acc)

    def l2n(a):
        return a * lax.rsqrt(jnp.sum(a * a, axis=-1, keepdims=True) + RMS_EPS)

    for h in range(GDN_HEADS):
        q_s[h] = l2n(xc[:, h * GDN_DIM:(h + 1) * GDN_DIM]) * (GDN_DIM ** -0.5)
        k_s[h] = l2n(xc[:, hw + h * GDN_DIM:hw + (h + 1) * GDN_DIM])
        v_s[h] = xc[:, 2 * hw + h * GDN_DIM:2 * hw + (h + 1) * GDN_DIM]

    misc = misc_ref[...]
    sp = misc + dtb_ref[...]
    softplus = jnp.maximum(sp, 0.0) + jnp.log(1.0 + jnp.exp(-jnp.abs(sp)))
    g = -jnp.exp(alog_ref[...]) * softplus
    beta_s[...] = jax.nn.sigmoid(misc)
    ri = lax.broadcasted_iota(I32, (tt, tt), 0)
    ci = lax.broadcasted_iota(I32, (tt, tt), 1)
    blk_tril = jnp.where((ri >= ci) & (ri // ch == ci // ch), 1.0, 0.0)
    gc_s[...] = jnp.dot(blk_tril, g, preferred_element_type=F32, precision=lax.Precision.HIGHEST)

    r64 = lax.broadcasted_iota(I32, (ch, ch), 0)
    c64 = lax.broadcasted_iota(I32, (ch, ch), 1)
    tri = r64 >= c64
    strict = r64 > c64
    eye = jnp.where(r64 == c64, 1.0, 0.0)
    norm_g = normg_ref[...]

    def chunk_step(c, _):
        r0 = pl.multiple_of(c * ch, ch)
        gcs = gc_s[pl.ds(r0, ch), :]
        gct = jnp.concatenate([gcs, jnp.zeros_like(gcs)], axis=0).T
        bts = beta_s[pl.ds(r0, ch), :]
        heads = range(GDN_HEADS)
        q = [q_s[h, pl.ds(r0, ch), :] for h in heads]
        k = [k_s[h, pl.ds(r0, ch), :] for h in heads]
        gcol = [gcs[:, MISC_GA + h:MISC_GA + h + 1] for h in heads]
        decay = [jnp.exp(jnp.where(tri, gcol[h] - gct[MISC_GA + h:MISC_GA + h + 1, 0:ch], -jnp.inf)) for h in heads]
        beta = [bts[:, MISC_GB + h:MISC_GB + h + 1] for h in heads]
        kb = [k[h] * beta[h] for h in heads]
        eg = [jnp.exp(gcol[h]) for h in heads]
        nmat = [-jnp.where(strict, _bdot(kb[h], k[h], _NT) * decay[h], 0.0) for h in heads]
        y = [jnp.concatenate([v_s[h, pl.ds(r0, ch), :] * beta[h], kb[h] * eg[h]], axis=1) for h in heads]
        for level in range(6):
            y = [y[h] + _dot3(nmat[h], y[h]) for h in heads]
            if level < 5:
                nmat = [_dot3(nmat[h], nmat[h]) for h in heads]
        attn = [jnp.where(tri, _bdot(q[h], k[h], _NT) * decay[h], 0.0) for h in heads]
        glast = [gcol[h][ch - 1:ch, :] for h in heads]
        state = [state_ref[h] for h in heads]
        v_new = [y[h][:, :GDN_DIM] - _bdot(y[h][:, GDN_DIM:], state[h]) for h in heads]
        o = [_bdot(q[h] * eg[h], state[h]) + _bdot(attn[h], v_new[h]) for h in heads]
        for h in heads:
            k_dec = k[h] * jnp.exp(glast[h] - gcol[h])
            state_ref[h] = state[h] * jnp.exp(glast[h]) + _bdot(k_dec, v_new[h], (((0,), (0,)), ((), ())))
        for h in heads:
            on = o[h] * lax.rsqrt(jnp.mean(o[h] * o[h], axis=-1, keepdims=True) + RMS_EPS) * norm_g
            zz = z_ref[pl.ds(r0, ch), h * GDN_DIM:(h + 1) * GDN_DIM]
            o_ref[pl.ds(r0, ch), h * GDN_DIM:(h + 1) * GDN_DIM] = (on * (zz * jax.nn.sigmoid(zz))).astype(o_ref.dtype)
        return 0

    lax.fori_loop(0, tt // ch, chunk_step, 0)


def _gdn_mixer(c, conv_w, a_log, dt_bias, norm_g, bsz, seq):
    t = bsz * seq
    tt = min(GDN_TILE, seq)
    nt = seq // tt
    lane_row = lambda vals, off: jnp.zeros((1, LANES), F32).at[0, off:off + GDN_HEADS].set(vals.astype(F32))
    alog = lane_row(a_log, MISC_GA)
    dtb = lane_row(dt_bias, MISC_GA)
    row = lambda w: pl.BlockSpec((tt, w), lambda b, j: (b * nt + j, 0))
    const = lambda a: pl.BlockSpec(a.shape, lambda b, j: (0, 0))
    ng = norm_g.reshape(1, GDN_DIM).astype(F32)
    hs = pltpu.VMEM((GDN_HEADS, tt, GDN_DIM), F32)
    return pl.pallas_call(
        _gdn_body,
        grid=(bsz, nt),
        in_specs=[row(GDN_QKV), row(LANES), row(GDN_HEADS * GDN_DIM), const(conv_w), const(alog), const(dtb), const(ng)],
        out_specs=row(GDN_HEADS * GDN_DIM),
        out_shape=jax.ShapeDtypeStruct((t, GDN_HEADS * GDN_DIM), BF16),
        scratch_shapes=[pltpu.VMEM((GDN_HEADS, GDN_DIM, GDN_DIM), F32), pltpu.VMEM((8, GDN_QKV), F32),
                        hs, hs, hs, pltpu.VMEM((tt, LANES), F32), pltpu.VMEM((tt, LANES), F32)],
        compiler_params=_cparams("parallel", "arbitrary"),
        name="gdn_mixer",
    )(c["gqkv"], c["misc"], c["gz"], conv_w.astype(F32), alog, dtb, ng)


ROW_TILE = 256


def _layer_norm(v, g, b):
    mu = jnp.mean(v, axis=-1, keepdims=True)
    vc = v - mu
    var = jnp.mean(vc * vc, axis=-1, keepdims=True)
    return vc * lax.rsqrt(var + LN_EPS) * g + b


def _merge_body(ya_ref, yb_ref, yc_ref, mg_ref, x_ref, wb_ref, wo_ref, g_ref, b_ref, wrt_ref,
                x1_ref, x1b_ref, xpk_ref, sct_ref, *, alpha):
    d = x_ref.shape[1]
    merged = mg_ref[:, 0:d] * jnp.dot(ya_ref[...], wb_ref[0], preferred_element_type=F32)
    merged = merged + mg_ref[:, d:2 * d] * jnp.dot(yb_ref[...], wb_ref[1], preferred_element_type=F32)
    merged = merged + mg_ref[:, 2 * d:3 * d] * jnp.dot(yc_ref[...], wb_ref[2], preferred_element_type=F32)
    y = alpha * x_ref[...] + jnp.dot(merged.astype(BF16), wo_ref[...], preferred_element_type=F32)
    x1 = _layer_norm(y, g_ref[...], b_ref[...])
    x1_ref[...] = x1
    x1b = x1.astype(BF16)
    x1b_ref[...] = x1b
    xpk_ref[...] = _pack_bf16_pairs(x1b)
    logits_t = lax.dot_general(wrt_ref[...], x1b, _NT, preferred_element_type=F32)
    sct_ref[...] = jax.nn.sigmoid(logits_t)


def _merge_out(ya, yb, yc, mg, x2d, w_branch, w_out, ln_g, ln_b, w_router, alpha):
    t, d = x2d.shape
    tm = ROW_TILE
    row = lambda w: pl.BlockSpec((tm, w), lambda i: (i, 0))
    wb = w_branch.astype(BF16)
    wo = w_out.astype(BF16)
    wrt = w_router.T.astype(BF16)
    g = ln_g.reshape(1, d)
    b = ln_b.reshape(1, d)
    return pl.pallas_call(
        functools.partial(_merge_body, alpha=alpha),
        grid=(t // tm,),
        in_specs=[row(BRANCH_WIDTH)] * 3 + [row(3 * d), row(d),
                  pl.BlockSpec(wb.shape, lambda i: (0, 0, 0)), pl.BlockSpec(wo.shape, lambda i: (0, 0)),
                  pl.BlockSpec((1, d), lambda i: (0, 0)), pl.BlockSpec((1, d), lambda i: (0, 0)),
                  pl.BlockSpec(wrt.shape, lambda i: (0, 0))],
        out_specs=[row(d), row(d), row(d // 2), pl.BlockSpec((N_EXPERTS, tm), lambda i: (0, i))],
        out_shape=[jax.ShapeDtypeStruct((t, d), F32), jax.ShapeDtypeStruct((t, d), BF16),
                   jax.ShapeDtypeStruct((t, d // 2), jnp.uint32), jax.ShapeDtypeStruct((N_EXPERTS, t), F32)],
        compiler_params=_cparams("parallel"),
        name="merge_out_ln",
    )(ya, yb, yc, mg, x2d, wb, wo, g, b, wrt)


ROUTE_TILE = 512


def _route_body(sct_ref, bias_ref, eidx_ref, gate_ref, rank_ref, cnt_ref):
    sc = sct_ref[...]
    n = sc.shape[1]
    per = N_EXPERTS // N_GROUPS
    biased = sc + bias_ref[...]
    b3 = biased.reshape(N_GROUPS, per, n)
    sub = lax.broadcasted_iota(I32, (N_GROUPS, per, n), 1)
    m1 = jnp.max(b3, axis=1, keepdims=True)
    first = jnp.min(jnp.where(b3 == m1, sub, per), axis=1, keepdims=True)
    m2 = jnp.max(jnp.where(sub == first, -jnp.inf, b3), axis=1, keepdims=True)
    gs = (m1 + m2).reshape(N_GROUPS, n)
    gi = lax.broadcasted_iota(I32, (N_GROUPS, n), 0)
    gmask = jnp.zeros((N_GROUPS, n), F32)
    for _ in range(TOPK_GROUPS):
        top = jnp.max(gs, axis=0, keepdims=True)
        pick = gi == jnp.min(jnp.where(gs == top, gi, N_GROUPS), axis=0, keepdims=True)
        gmask = jnp.where(pick, 1.0, gmask)
        gs = jnp.where(pick, -jnp.inf, gs)
    emask = jnp.broadcast_to(gmask.reshape(N_GROUPS, 1, n), (N_GROUPS, per, n)).reshape(N_EXPERTS, n)
    cand = jnp.where(emask > 0.5, biased, -jnp.inf)
    ei = lax.broadcasted_iota(I32, (N_EXPERTS, n), 0)
    ids, gates, picks = [], [], []
    for _ in range(TOP_K):
        top = jnp.max(cand, axis=0, keepdims=True)
        idx = jnp.min(jnp.where(cand == top, ei, N_EXPERTS), axis=0, keepdims=True)
        pick = ei == idx
        ids.append(idx)
        picks.append(pick)
        gates.append(jnp.sum(jnp.where(pick, sc, 0.0), axis=0, keepdims=True))
        cand = jnp.where(pick, -jnp.inf, cand)
    gate = jnp.concatenate(gates, axis=0)
    gate = gate / jnp.sum(gate, axis=0, keepdims=True) * ROUTED_SCALE
    eidx_ref[...] = jnp.concatenate(ids, axis=0)
    gate_ref[...] = gate

    @pl.when(pl.program_id(0) == 0)
    def _():
        cnt_ref[...] = jnp.zeros(cnt_ref.shape, F32)

    sel = jnp.where(cand == -jnp.inf, 1.0, 0.0) * jnp.where(emask > 0.5, 1.0, 0.0)
    ri = lax.broadcasted_iota(I32, (n, n), 0)
    ci = lax.broadcasted_iota(I32, (n, n), 1)
    before = jnp.dot(sel.astype(BF16), jnp.where(ri < ci, 1.0, 0.0).astype(BF16), preferred_element_type=F32)
    pos = before + cnt_ref[...]
    rank_ref[...] = jnp.concatenate(
        [jnp.sum(jnp.where(pk, pos, 0.0), axis=0, keepdims=True) for pk in picks], axis=0).astype(I32)
    cnt_ref[...] = cnt_ref[...] + jnp.sum(sel, axis=1, keepdims=True)


def _route(sct, router_bias):
    e, t = sct.shape
    tn = ROUTE_TILE
    col = lambda r: pl.BlockSpec((r, tn), lambda i: (0, i))
    return pl.pallas_call(
        _route_body,
        grid=(t // tn,),
        in_specs=[col(e), pl.BlockSpec((e, 1), lambda i: (0, 0))],
        out_specs=[col(TOP_K), col(TOP_K), col(TOP_K), pl.BlockSpec((e, 1), lambda i: (0, 0))],
        out_shape=[jax.ShapeDtypeStruct((TOP_K, t), I32), jax.ShapeDtypeStruct((TOP_K, t), F32),
                   jax.ShapeDtypeStruct((TOP_K, t), I32), jax.ShapeDtypeStruct((e, 1), F32)],
        compiler_params=_cparams("arbitrary"),
        name="route_topk",
    )(sct, router_bias.reshape(e, 1).astype(F32))


U32 = jnp.uint32
DISPATCH_TILE = 512


def _pack_bf16_pairs(xb):
    n = xb.shape[1] // 2
    bits = pltpu.bitcast(xb.astype(F32), U32)
    return (bits[:, :n] >> 16) | (bits[:, n:] & jnp.uint32(0xFFFF0000))


def _unpack_bf16_pairs(packed):
    lo = pltpu.bitcast(packed << 16, F32).astype(BF16)
    hi = pltpu.bitcast(packed & jnp.uint32(0xFFFF0000), F32).astype(BF16)
    return lo, hi


def _dispatch_body(dest_hbm, x_ref, init_hbm, xs_hbm, idx_smem, isem, rsem):
    del init_hbm
    i = pl.program_id(0)
    tn = DISPATCH_TILE
    idx_cp = pltpu.make_async_copy(dest_hbm.at[i], idx_smem, isem)
    idx_cp.start()
    idx_cp.wait()

    def issue(j, _):
        src = x_ref.at[pl.ds(j, 1)]
        for k in range(TOP_K):
            pltpu.make_async_copy(src, xs_hbm.at[pl.ds(idx_smem[k * tn + j], 1)], rsem).start()
        return 0

    lax.fori_loop(0, tn, issue, 0)
    for k in range(TOP_K):
        pltpu.make_async_copy(x_ref, xs_hbm.at[pl.ds(0, tn)], rsem).wait()


def _dispatch(xpk, dest_tiles, n_pad):
    t, w = xpk.shape
    tn = DISPATCH_TILE
    anyspec = pl.BlockSpec(memory_space=pl.ANY)
    return pl.pallas_call(
        _dispatch_body,
        grid=(t // tn,),
        in_specs=[anyspec, pl.BlockSpec((tn, w), lambda i: (i, 0)), anyspec],
        out_specs=anyspec,
        out_shape=jax.ShapeDtypeStruct((n_pad, w), xpk.dtype),
        scratch_shapes=[pltpu.SMEM((TOP_K * tn,), I32), pltpu.SemaphoreType.DMA, pltpu.SemaphoreType.DMA],
        input_output_aliases={2: 0},
        compiler_params=_cparams("arbitrary"),
        name="moe_dispatch",
    )(dest_tiles, xpk, jnp.zeros((n_pad, w), xpk.dtype))


def _expert_body(blk_e_ref, nused_ref, xb_ref, wgu_ref, wd_ref, o_ref):
    i = pl.program_id(0)
    half = xb_ref.shape[1]

    @pl.when(i < nused_ref[0])
    def _():
        lo, hi = _unpack_bf16_pairs(xb_ref[...])
        gu = (jnp.dot(lo, wgu_ref[0, :half, :], preferred_element_type=F32)
              + jnp.dot(hi, wgu_ref[0, half:, :], preferred_element_type=F32))
        gg, uu = gu[:, :D_EXPERT], gu[:, D_EXPERT:]
        hmid = (gg * jax.nn.sigmoid(gg) * uu).astype(BF16)
        o_ref[...] = jnp.dot(hmid, wd_ref[0], preferred_element_type=F32)

    @pl.when(i >= nused_ref[0])
    def _():
        o_ref[...] = jnp.zeros(o_ref.shape, o_ref.dtype)


def _expert_ffn(xs, blk_e, n_used, wgu, wd):
    n_pad, half = xs.shape
    d = 2 * half
    nb = n_pad // EXPERT_BLOCK
    grid_spec = pltpu.PrefetchScalarGridSpec(
        num_scalar_prefetch=2,
        grid=(nb,),
        in_specs=[pl.BlockSpec((EXPERT_BLOCK, half), lambda i, be, nu: (i, 0)),
                  pl.BlockSpec((1, d, 2 * D_EXPERT), lambda i, be, nu: (be[i], 0, 0)),
                  pl.BlockSpec((1, D_EXPERT, d), lambda i, be, nu: (be[i], 0, 0))],
        out_specs=pl.BlockSpec((EXPERT_BLOCK, d), lambda i, be, nu: (i, 0)),
    )
    return pl.pallas_call(
        _expert_body,
        grid_spec=grid_spec,
        out_shape=jax.ShapeDtypeStruct((n_pad, d), F32),
        compiler_params=_cparams("arbitrary"),
        name="expert_ffn",
    )(blk_e, n_used, xs, wgu, wd)


def _dest_body(eidx_ref, rank_ref, start_ref, dest_ref):
    n = eidx_ref.shape[1]
    ei = lax.broadcasted_iota(I32, (N_EXPERTS, n), 0)
    start = start_ref[...]
    rows = [jnp.sum(jnp.where(ei == eidx_ref[k:k + 1, :], start, 0.0), axis=0, keepdims=True)
            for k in range(TOP_K)]
    dest_ref[...] = rank_ref[...] + jnp.concatenate(rows, axis=0).astype(I32)


def _moe_plan(eidx_t, rank_t, counts, n_tok):
    n_rows = n_tok * TOP_K
    n_blocks = -(-n_rows // EXPERT_BLOCK) + N_EXPERTS
    cnt = counts.reshape(-1).astype(I32)
    padded = (cnt + EXPERT_BLOCK - 1) // EXPERT_BLOCK * EXPERT_BLOCK
    pad_end = jnp.cumsum(padded)
    pad_start = pad_end - padded
    tn = ROUTE_TILE
    col = pl.BlockSpec((TOP_K, tn), lambda i: (0, i))
    dest = pl.pallas_call(
        _dest_body,
        grid=(n_tok // tn,),
        in_specs=[col, col, pl.BlockSpec((N_EXPERTS, 1), lambda i: (0, 0))],
        out_specs=col,
        out_shape=jax.ShapeDtypeStruct((TOP_K, n_tok), I32),
        compiler_params=_cparams("parallel"),
        name="route_dest",
    )(eidx_t, rank_t, pad_start.astype(F32).reshape(N_EXPERTS, 1))
    blk_first = jnp.arange(n_blocks, dtype=I32) * EXPERT_BLOCK
    blk_e = jnp.minimum(jnp.sum((pad_end[None, :] <= blk_first[:, None]).astype(I32), axis=1), N_EXPERTS - 1)
    n_used = (pad_end[-1] // EXPERT_BLOCK).astype(I32).reshape(1)
    return dest, blk_e, n_used, n_blocks * EXPERT_BLOCK


def _tile_indices(dest, tn):
    k, t = dest.shape
    return dest.reshape(k, t // tn, tn).transpose(1, 0, 2).reshape(t // tn, k * tn)


def _ffn_out_body(dest_hbm, ye_hbm, gate_ref, x1_ref, x1b_ref, p_ref, wsgu_ref, wsd_ref, wpg_ref, wpp_ref,
                  g_ref, b_ref, o_ref, idx_smem, rows_ref, isem, rsem, *, alpha):
    i = pl.program_id(0)
    n_tiles = pl.num_programs(0)
    tm = x1_ref.shape[0]
    slot = i % 2

    def gather_tile(tile, s):
        idx_cp = pltpu.make_async_copy(dest_hbm.at[tile], idx_smem.at[s], isem)
        idx_cp.start()
        idx_cp.wait()

        def issue(j, _):
            for k in range(TOP_K):
                pltpu.make_async_copy(ye_hbm.at[pl.ds(idx_smem[s, k * tm + j], 1)],
                                      rows_ref.at[s, k, pl.ds(j, 1)], rsem.at[s]).start()
            return 0

        lax.fori_loop(0, tm, issue, 0)

    @pl.when(i == 0)
    def _():
        gather_tile(0, 0)

    @pl.when(i + 1 < n_tiles)
    def _():
        gather_tile(i + 1, 1 - slot)

    xb = x1b_ref[...]
    su = jnp.dot(xb, wsgu_ref[...], preferred_element_type=F32)
    sg, uu = su[:, :D_SHARED], su[:, D_SHARED:]
    shared = jnp.dot((sg * jax.nn.sigmoid(sg) * uu).astype(BF16), wsd_ref[...], preferred_element_type=F32)
    pg = jax.nn.sigmoid(jnp.dot(xb, wpg_ref[...], preferred_element_type=F32))
    ple = pg * jnp.dot(p_ref[...].astype(BF16), wpp_ref[...], preferred_element_type=F32)

    g8 = gate_ref[...]
    gate_cols = jnp.concatenate([g8, jnp.zeros((LANES - TOP_K, tm), F32)], axis=0).T
    for k in range(TOP_K):
        pltpu.make_async_copy(ye_hbm.at[pl.ds(0, tm)], rows_ref.at[slot, k], rsem.at[slot]).wait()
    routed = gate_cols[:, 0:1] * rows_ref[slot, 0]
    for k in range(1, TOP_K):
        routed = routed + gate_cols[:, k:k + 1] * rows_ref[slot, k]
    y = alpha * x1_ref[...] + (routed + shared) + ple
    o_ref[...] = _layer_norm(y, g_ref[...], b_ref[...])


def _ffn_out(dest_tiles, ye, gate_t, x1, x1b, p2d, w_sgu, w_sd, w_pg, w_pp, ln_g, ln_b, alpha):
    t, d = x1.shape
    tm = ROW_TILE
    row = lambda w: pl.BlockSpec((tm, w), lambda i: (i, 0))
    const = lambda a: pl.BlockSpec(a.shape, lambda i: (0, 0))
    anyspec = pl.BlockSpec(memory_space=pl.ANY)
    ws = [w.astype(BF16) for w in (w_sgu, w_sd, w_pg, w_pp)]
    g = ln_g.reshape(1, d)
    b = ln_b.reshape(1, d)
    return pl.pallas_call(
        functools.partial(_ffn_out_body, alpha=alpha),
        grid=(t // tm,),
        in_specs=[anyspec, anyspec, pl.BlockSpec((TOP_K, tm), lambda i: (0, i)), row(d), row(d), row(PLE_DIM)]
                 + [const(w) for w in ws] + [const(g), const(b)],
        out_specs=row(d),
        out_shape=jax.ShapeDtypeStruct((t, d), F32),
        scratch_shapes=[pltpu.SMEM((2, TOP_K * tm), I32), pltpu.VMEM((2, TOP_K, tm, d), F32),
                        pltpu.SemaphoreType.DMA, pltpu.SemaphoreType.DMA((2,))],
        compiler_params=_cparams("arbitrary"),
        name="ffn_out_ln",
    )(dest_tiles, ye, gate_t, x1, x1b, p2d, *ws, g, b)


def kernel(x, p, w_in, w_dsa_uv, w_cmp_k, pe_cmp_k, w_cmp_v, pe_cmp_v, conv_w, a_log, dt_bias, gdn_norm, w_branch, w_out, ln1_g, ln1_b, w_router, router_bias, w_gate_up, w_down, w_sh_gate_up, w_sh_down, w_ple_proj, w_ple_gate, ln2_g, ln2_b):
    bsz, seq, d = x.shape
    depth = w_in.shape[0]
    alpha = (2 * depth) ** 0.25
    t = bsz * seq
    tabs64, _ = _rope_lane_tables(seq, HEAD_DIM)
    tabs32, _ = _rope_lane_tables(seq, IDX_DIM)
    x2d = x.reshape(t, d)
    for i in range(depth):
        c = _in_projection(x2d, _split_w_in(w_in[i]), tabs64, tabs32, seq)
        ya = _dsa_attention(c, w_dsa_uv[i], bsz, seq)
        kcmp, vcmpt = _nsa_compress(c, w_cmp_k[i], pe_cmp_k[i], w_cmp_v[i], pe_cmp_v[i], bsz, seq)
        yb = _nsa_attention(c, kcmp, vcmpt, bsz, seq)
        yc = _gdn_mixer(c, conv_w[i], a_log[i], dt_bias[i], gdn_norm[i], bsz, seq)
        x1, x1b, xpk, sct = _merge_out(ya, yb, yc, c["mg"], x2d, w_branch[i], w_out[i], ln1_g[i], ln1_b[i],
                                       w_router[i], alpha)
        eidx_t, gate_t, rank_t, counts = _route(sct, router_bias[i])
        dest, blk_e, n_used, n_pad = _moe_plan(eidx_t, rank_t, counts, t)
        xs = _dispatch(xpk, _tile_indices(dest, DISPATCH_TILE), n_pad)
        ye = _expert_ffn(xs, blk_e, n_used, w_gate_up[i].astype(BF16), w_down[i].astype(BF16))
        x2d = _ffn_out(_tile_indices(dest, ROW_TILE), ye, gate_t, x1, x1b, p[i].reshape(t, PLE_DIM),
                       w_sh_gate_up[i], w_sh_down[i], w_ple_gate[i], w_ple_proj[i], ln2_g[i], ln2_b[i], alpha)
    return x2d.reshape(bsz, seq, d)
```

```python
import functools
import math

import numpy as np
import jax
import jax.numpy as jnp
from jax import lax
from jax.experimental import pallas as pl
from jax.experimental.pallas import tpu as pltpu

F32 = jnp.float32
BF16 = jnp.bfloat16
I32 = jnp.int32

HEAD_DIM = 64
ROPE_THETA = 500000.0
ROPE_FRACTION = 4
DSA_HEADS = 8
DSA_LATENT = 64
IDX_HEADS = 4
IDX_DIM = 32
DSA_TOPK = 256
NSA_HEADS = 8
CMP_BLOCK = 32
CMP_STRIDE = 16
SEL_BLOCK = 32
N_SEL_BLOCKS = 8
WINDOW = 256
FORCED_SCORE = 1.0e4
GDN_HEADS = 4
GDN_DIM = 128
CONV_WIDTH = 4
GDN_CHUNK = 64
BRANCH_WIDTH = 512
N_BRANCHES = 3
GDN_QKV = 3 * GDN_HEADS * GDN_DIM
N_EXPERTS = 64
TOP_K = 8
N_GROUPS = 8
TOPK_GROUPS = 4
D_EXPERT = 256
D_SHARED = 256
ROUTED_SCALE = 2.5
EXPERT_BLOCK = 512
PLE_DIM = 256
LN_EPS = 1e-5
RMS_EPS = 1e-6

IN_SPLITS = (
    ("dsa_q", DSA_HEADS * DSA_LATENT), ("dsa_kv", DSA_LATENT),
    ("idx_q", IDX_HEADS * IDX_DIM), ("idx_k", IDX_DIM), ("idx_w", IDX_HEADS),
    ("nsa_q", NSA_HEADS * HEAD_DIM),
    ("nsa_kc", HEAD_DIM), ("nsa_vc", HEAD_DIM),
    ("nsa_ks", HEAD_DIM), ("nsa_vs", HEAD_DIM),
    ("nsa_kw", HEAD_DIM), ("nsa_vw", HEAD_DIM),
    ("nsa_g", NSA_HEADS * 3),
    ("gdn_qkv", GDN_QKV), ("gdn_a", GDN_HEADS), ("gdn_b", GDN_HEADS), ("gdn_z", GDN_HEADS * GDN_DIM),
    ("merge_g", N_BRANCHES * 1024),
)

LANES = 128
Q_TILE = 128
KEY_CHUNK = 512
TIE_BLOCK = 256
VMEM_LIMIT = 56 * 1024 * 1024
INT_MIN = -2 ** 31
NEG_BIG = -1e30


def _cparams(*sem):
    return pltpu.CompilerParams(dimension_semantics=sem, vmem_limit_bytes=VMEM_LIMIT)


def _rope_lane_tables(seq, head_dim):
    rot = head_dim // ROPE_FRACTION
    half = rot // 2
    inv_freq = ROPE_THETA ** (-(jnp.arange(half, dtype=F32) * 2.0 / rot))
    ang = jnp.arange(seq, dtype=F32)[:, None] * inv_freq[None, :]
    cos, sin = jnp.cos(ang), jnp.sin(ang)
    one = jnp.ones((seq, head_dim - rot), F32)
    c = jnp.concatenate([cos, cos, one], axis=-1)
    sa = jnp.concatenate([-sin, jnp.zeros((seq, head_dim - half), F32)], axis=-1)
    sb = jnp.concatenate([jnp.zeros((seq, half), F32), sin, jnp.zeros((seq, head_dim - rot), F32)], axis=-1)
    rep = LANES // head_dim
    return tuple(jnp.tile(a, (1, rep)) for a in (c, sa, sb)), half


def _rope_apply(y, c, sa, sb, half):
    n = y.shape[-1]
    rep = n // LANES
    if rep > 1:
        c, sa, sb = (jnp.concatenate([a] * rep, axis=-1) for a in (c, sa, sb))
    up = pltpu.roll(y, n - half, 1)
    dn = pltpu.roll(y, half, 1)
    return y * c + up * sa + dn * sb


IN_TILE = 256


def _inproj_body(x_ref, w64_ref, w32_ref, wpl_ref, wvt_ref, wmg_ref,
                 c64_ref, sa64_ref, sb64_ref, c32_ref, sa32_ref, sb32_ref,
                 dq_ref, nq_ref, krot_ref, nkc_ref, nks_ref, nkw_ref, iq_ref, ik_ref,
                 nvc_ref, misc_ref, gz_ref, gqkv_ref, vt_ref, mg_ref):
    xb = x_ref[...].astype(BF16)
    c64, sa64, sb64 = c64_ref[...], sa64_ref[...], sb64_ref[...]
    half64 = HEAD_DIM // ROPE_FRACTION // 2
    half32 = IDX_DIM // ROPE_FRACTION // 2

    def proj(w_ref, lo, hi):
        return jnp.dot(xb, w_ref[:, lo:hi], preferred_element_type=F32)

    dq_ref[...] = _rope_apply(proj(w64_ref, 0, 512), c64, sa64, sb64, half64).astype(BF16)
    nq_ref[...] = _rope_apply(proj(w64_ref, 512, 1024), c64, sa64, sb64, half64).astype(BF16)
    k1 = _rope_apply(proj(w64_ref, 1024, 1152), c64, sa64, sb64, half64).astype(BF16)
    krot_ref[...] = k1[:, :64]
    nkc_ref[...] = k1[:, 64:]
    k2 = _rope_apply(proj(w64_ref, 1152, 1280), c64, sa64, sb64, half64).astype(BF16)
    nks_ref[...] = k2[:, :64]
    nkw_ref[...] = k2[:, 64:]
    c32, sa32, sb32 = c32_ref[...], sa32_ref[...], sb32_ref[...]
    iq_ref[...] = _rope_apply(proj(w32_ref, 0, 128), c32, sa32, sb32, half32).astype(BF16)
    ik = _rope_apply(proj(w32_ref, 128, 256), c32, sa32, sb32, half32).astype(BF16)
    ik_ref[...] = ik[:, :IDX_DIM]
    pl0 = proj(wpl_ref, 0, 128)
    nvc_ref[...] = pl0[:, :64].astype(BF16)
    misc_ref[...] = proj(wpl_ref, 128, 256)
    gz_ref[...] = proj(wpl_ref, 256, 768)
    for j in range(3):
        gqkv_ref[:, j * 512:(j + 1) * 512] = proj(wpl_ref, 768 + j * 512, 768 + (j + 1) * 512)
    vt_ref[...] = lax.dot_general(wvt_ref[...], xb, (((1,), (1,)), ((), ())),
                                  preferred_element_type=F32).astype(BF16)
    for j in range(6):
        g = proj(wmg_ref, j * 512, (j + 1) * 512)
        mg_ref[:, j * 512:(j + 1) * 512] = jax.nn.sigmoid(g)


def _split_w_in(w):
    cols = {}
    off = 0
    for name, size in IN_SPLITS:
        cols[name] = w[:, off:off + size]
        off += size
    d = w.shape[0]
    z = lambda n: jnp.zeros((d, n), w.dtype)
    w64 = jnp.concatenate([cols["dsa_q"], cols["nsa_q"], cols["dsa_kv"], cols["nsa_kc"],
                           cols["nsa_ks"], cols["nsa_kw"]], axis=1)
    w32 = jnp.concatenate([cols["idx_q"], cols["idx_k"], z(LANES - IDX_DIM)], axis=1)
    misc = jnp.concatenate([cols["idx_w"], cols["gdn_a"], cols["gdn_b"], cols["nsa_g"]], axis=1)
    misc = jnp.concatenate([misc, z(LANES - misc.shape[1])], axis=1)
    wpl = jnp.concatenate([cols["nsa_vc"], z(64), misc, cols["gdn_z"], cols["gdn_qkv"]], axis=1)
    wvt = jnp.concatenate([cols["dsa_kv"], cols["nsa_vs"], cols["nsa_vw"], z(64)], axis=1).T
    return [a.astype(BF16) for a in (w64, w32, wpl, wvt, cols["merge_g"])]


MISC_IW = 0
MISC_GA = IDX_HEADS
MISC_GB = MISC_GA + GDN_HEADS
MISC_NG = MISC_GB + GDN_HEADS


def _in_projection(x2d, wparts, tabs64, tabs32, seq):
    t, d = x2d.shape
    tm = IN_TILE
    nt = seq // tm
    w64, w32, wpl, wvt, wmg = wparts
    row = lambda w: pl.BlockSpec((tm, w), lambda i: (i, 0))
    full = lambda a: pl.BlockSpec(a.shape, lambda i: (0, 0))
    tab = pl.BlockSpec((tm, LANES), lambda i: (i % nt, 0))
    outs = [
        ("dq", 512, BF16), ("nq", 512, BF16), ("krot", 64, BF16), ("nkc", 64, BF16), ("nks", 64, BF16),
        ("nkw", 64, BF16), ("iq", 128, BF16), ("ik", IDX_DIM, BF16), ("nvc", 64, BF16),
        ("misc", 128, F32), ("gz", 512, F32), ("gqkv", GDN_QKV, F32),
    ]
    out_shape = [jax.ShapeDtypeStruct((t, w), dt) for _, w, dt in outs]
    out_specs = [row(w) for _, w, _ in outs]
    out_shape += [jax.ShapeDtypeStruct((256, t), BF16), jax.ShapeDtypeStruct((t, 3 * d), F32)]
    out_specs += [pl.BlockSpec((256, tm), lambda i: (0, i)), row(3 * d)]
    res = pl.pallas_call(
        _inproj_body,
        grid=(t // tm,),
        in_specs=[row(d), full(w64), full(w32), full(wpl), full(wvt), full(wmg)] + [tab] * 6,
        out_specs=out_specs,
        out_shape=out_shape,
        compiler_params=_cparams("parallel"),
        name="in_projection",
    )(x2d, w64, w32, wpl, wvt, wmg, *tabs64, *tabs32)
    names = [n for n, _, _ in outs] + ["vt", "mg"]
    return dict(zip(names, res))


_NT = (((1,), (1,)), ((), ()))


def _stack_heads(q, n_heads, width):
    return jnp.concatenate([q[:, h * width:(h + 1) * width] for h in range(n_heads)], axis=0)


def _unstack_heads_t(o_t, n_heads):
    q = o_t.shape[1] // n_heads
    rows = jnp.concatenate([o_t[:, h * q:(h + 1) * q] for h in range(n_heads)], axis=0)
    return rows.T


def _flash_chunks(qs, k_ref, vt_ref, lo, hi, rows, mask_fn, carry0, acc_ref, m_ref, l_ref, n_heads):
    m_ref[...] = jnp.full(m_ref.shape, NEG_BIG, F32)
    l_ref[...] = jnp.zeros(l_ref.shape, F32)
    acc_ref[...] = jnp.zeros(acc_ref.shape, F32)

    def body(c, carry):
        r0 = pl.multiple_of(c * rows, rows)
        bias, carry = mask_fn(r0, carry)
        kc = k_ref[pl.ds(r0, rows), :]
        s = lax.dot_general(kc, qs, _NT, preferred_element_type=F32)
        s = s + jnp.concatenate([bias] * n_heads, axis=1)
        m_old = m_ref[...]
        m_new = jnp.maximum(m_old, jnp.max(s, axis=0, keepdims=True))
        alpha = jnp.exp(m_old - m_new)
        p = jnp.exp(s - m_new)
        l_ref[...] = alpha * l_ref[...] + jnp.sum(p, axis=0, keepdims=True)
        pv = jnp.dot(vt_ref[:, pl.ds(r0, rows)], p.astype(BF16), preferred_element_type=F32)
        acc_ref[...] = acc_ref[...] * alpha + pv
        m_ref[...] = m_new
        return carry

    lax.fori_loop(lo, hi, body, carry0)
    return acc_ref[...] / jnp.maximum(l_ref[...], 1e-30)


def _dsa_body(iq_ref, dq_ref, misc_ref, ik_ref, krot_ref, vt_ref, wuvt_ref, o_ref,
              keys_ref, acc_ref, m_ref, l_ref, *, n_sel):
    rows = KEY_CHUNK
    sub = TIE_BLOCK
    q0 = pl.program_id(1) * Q_TILE
    nch = (q0 + Q_TILE + rows - 1) // rows
    tq = q0 + lax.broadcasted_iota(I32, (1, Q_TILE), 1)
    misc_t = misc_ref[...].T
    coef = (IDX_HEADS ** -0.5) * (IDX_DIM ** -0.5)
    iq = iq_ref[...]
    int_min = jnp.int32(INT_MIN)
    wrow = [misc_t[MISC_IW + h:MISC_IW + h + 1, :] * coef for h in range(IDX_HEADS)]

    def score_block(r0):
        ikc = ik_ref[pl.ds(r0, sub), :]
        acc = jnp.zeros((sub, Q_TILE), F32)
        for h in range(IDX_HEADS):
            s = lax.dot_general(ikc, iq[:, h * IDX_DIM:(h + 1) * IDX_DIM], _NT, preferred_element_type=F32)
            acc = acc + jnp.maximum(s, 0.0) * wrow[h]
        acc = jnp.where(acc == 0.0, 0.0, acc)
        bits = pltpu.bitcast(acc, I32)
        key = bits ^ ((bits >> 31) & jnp.int32(0x7FFFFFFF))
        kpos = r0 + lax.broadcasted_iota(I32, (sub, Q_TILE), 0)
        keys_ref[pl.ds(r0, sub), :] = jnp.where(kpos <= tq, key, int_min)

    def score_chunk(c, _):
        for j in range(rows // sub):
            score_block(pl.multiple_of(c * rows + j * sub, sub))
        return 0

    lax.fori_loop(0, nch, score_chunk, 0)

    def count(pred):
        def body(c, acc):
            r0 = pl.multiple_of(c * rows, rows)
            hit = jnp.where(pred(keys_ref[pl.ds(r0, rows), :]), 1, 0).astype(I32)
            return acc + jnp.sum(hit.reshape(rows // 8, 8, Q_TILE), axis=0)
        acc = lax.fori_loop(0, nch, body, jnp.zeros((8, Q_TILE), I32))
        return jnp.sum(acc, axis=0, keepdims=True)

    kq = jnp.minimum(tq + 1, n_sel)

    def bit_step(i, tu):
        cand_u = tu | lax.shift_left(jnp.int32(1), 31 - i)
        cand = cand_u ^ int_min
        cnt = count(lambda kk: kk >= cand)
        return jnp.where(cnt >= kq, cand_u, tu)

    thr = lax.fori_loop(0, 32, bit_step, jnp.zeros((1, Q_TILE), I32)) ^ int_min
    need = (kq - count(lambda kk: kk > thr)).astype(F32)

    ri = lax.broadcasted_iota(I32, (sub, sub), 0)
    ci = lax.broadcasted_iota(I32, (sub, sub), 1)
    tri = jnp.where(ri > ci, 1.0, 0.0).astype(BF16)

    def mask_fn(r0, seen):
        parts = []
        for j in range(rows // sub):
            kk = keys_ref[pl.ds(pl.multiple_of(r0 + j * sub, sub), sub), :]
            tie = jnp.where(kk == thr, 1.0, 0.0)
            before = jnp.dot(tri, tie.astype(BF16), preferred_element_type=F32) + seen
            take = (kk > thr) | ((kk == thr) & (before < need))
            parts.append(jnp.where(take, 0.0, -jnp.inf))
            seen = seen + jnp.sum(tie, axis=0, keepdims=True)
        return jnp.concatenate(parts, axis=0), seen

    qs = _stack_heads(dq_ref[...], DSA_HEADS, DSA_LATENT) * jnp.asarray(DSA_LATENT ** -0.5, BF16)
    o_lat = _flash_chunks(qs, krot_ref, vt_ref, 0, nch, rows, mask_fn, jnp.zeros((1, Q_TILE), F32),
                          acc_ref, m_ref, l_ref, DSA_HEADS)
    outs = []
    for h in range(DSA_HEADS):
        oh = o_lat[:, h * Q_TILE:(h + 1) * Q_TILE].astype(BF16)
        outs.append(jnp.dot(wuvt_ref[h], oh, preferred_element_type=F32))
    o_ref[...] = jnp.concatenate(outs, axis=0).T.astype(o_ref.dtype)


def _dsa_attention(c, w_uv, bsz, seq):
    t = bsz * seq
    nq = seq // Q_TILE
    n_sel = min(DSA_TOPK, seq // 4)
    wuvt = jnp.swapaxes(w_uv, 1, 2).astype(BF16)
    qrow = lambda w: pl.BlockSpec((Q_TILE, w), lambda b, n: (b * nq + n, 0))
    seqrow = lambda w: pl.BlockSpec((seq, w), lambda b, n: (b, 0))
    nqh = DSA_HEADS * Q_TILE
    return pl.pallas_call(
        functools.partial(_dsa_body, n_sel=n_sel),
        grid=(bsz, nq),
        in_specs=[qrow(128), qrow(512), qrow(128), seqrow(IDX_DIM), seqrow(DSA_LATENT),
                  pl.BlockSpec((DSA_LATENT, seq), lambda b, n: (0, b)),
                  pl.BlockSpec(wuvt.shape, lambda b, n: (0, 0, 0))],
        out_specs=qrow(512),
        out_shape=jax.ShapeDtypeStruct((t, 512), BF16),
        scratch_shapes=[pltpu.VMEM((seq, Q_TILE), I32), pltpu.VMEM((DSA_LATENT, nqh), F32),
                        pltpu.VMEM((1, nqh), F32), pltpu.VMEM((1, nqh), F32)],
        compiler_params=_cparams("parallel", "arbitrary"),
        name="dsa_attention",
    )(c["iq"], c["dq"], c["misc"], c["ik"], c["krot"], c["vt"], wuvt)


CMP_PER_ROW = CMP_STRIDE * HEAD_DIM


def _nsa_cmp_body(kc_ref, vc_ref, wk_ref, wv_ref, pek_ref, pev_ref, kcmp_ref, vcmpt_ref):
    n = kc_ref.shape[0]

    def compress(x_ref, w_ref, pe_ref):
        xf = x_ref[...].astype(F32)
        lo = jnp.dot((xf + pe_ref[0:1, :]).astype(BF16), w_ref[0], preferred_element_type=F32)
        hi = jnp.dot((xf + pe_ref[1:2, :]).astype(BF16), w_ref[1], preferred_element_type=F32)
        return lo + pltpu.roll(hi, n - 1, 0)

    kcmp_ref[...] = compress(kc_ref, wk_ref, pek_ref).astype(BF16)
    vc = compress(vc_ref, wv_ref, pev_ref)
    vpad = jnp.concatenate([vc, jnp.zeros_like(vc)], axis=1)
    vcmpt_ref[...] = vpad.T[:HEAD_DIM, :].astype(BF16)


def _nsa_compress(c, w_ck, pe_k, w_cv, pe_v, bsz, seq):
    ng = seq // CMP_STRIDE
    kc2 = c["nkc"].reshape(bsz * ng, CMP_PER_ROW)
    vc2 = c["nvc"].reshape(bsz * ng, CMP_PER_ROW)
    wk = w_ck.reshape(2, CMP_PER_ROW, HEAD_DIM).astype(BF16)
    wv = w_cv.reshape(2, CMP_PER_ROW, HEAD_DIM).astype(BF16)
    pek = pe_k.reshape(2, CMP_PER_ROW)
    pev = pe_v.reshape(2, CMP_PER_ROW)
    grp = pl.BlockSpec((ng, CMP_PER_ROW), lambda b: (b, 0))
    wsp = pl.BlockSpec((2, CMP_PER_ROW, HEAD_DIM), lambda b: (0, 0, 0))
    psp = pl.BlockSpec((2, CMP_PER_ROW), lambda b: (0, 0))
    return pl.pallas_call(
        _nsa_cmp_body,
        grid=(bsz,),
        in_specs=[grp, grp, wsp, wsp, psp, psp],
        out_specs=[pl.BlockSpec((ng, HEAD_DIM), lambda b: (b, 0)), pl.BlockSpec((HEAD_DIM, ng), lambda b: (0, b))],
        out_shape=[jax.ShapeDtypeStruct((bsz * ng, HEAD_DIM), BF16), jax.ShapeDtypeStruct((HEAD_DIM, bsz * ng), BF16)],
        compiler_params=_cparams("parallel"),
        name="nsa_compress",
    )(kc2, vc2, wk, wv, pek, pev)


def _nsa_body(nq_ref, misc_ref, kcmp_ref, vcmpt_ref, ovl_ref, exp_ref, ks_ref, vst_ref, kw_ref, vwt_ref,
              o_ref, sel_ref, acc_ref, m_ref, l_ref, *, n_sel, n_cmp):
    rows = KEY_CHUNK
    n = pl.program_id(1)
    q0 = n * Q_TILE
    tq = q0 + lax.broadcasted_iota(I32, (1, Q_TILE), 1)
    qs = _stack_heads(nq_ref[...], NSA_HEADS, HEAD_DIM) * jnp.asarray(HEAD_DIM ** -0.5, BF16)
    gate_t = jax.nn.sigmoid(misc_ref[...]).T

    def gate_row(branch):
        return jnp.concatenate([gate_t[MISC_NG + 3 * h + branch:MISC_NG + 3 * h + branch + 1, :]
                                for h in range(NSA_HEADS)], axis=1)

    ng = kcmp_ref.shape[0]
    jrow = lax.broadcasted_iota(I32, (ng, Q_TILE), 0)
    ok_c = (jrow * CMP_STRIDE + (CMP_BLOCK - 1) <= tq) & (jrow < n_cmp)
    bias_c = jnp.where(ok_c, 0.0, -jnp.inf)
    s = lax.dot_general(kcmp_ref[...], qs, _NT, preferred_element_type=F32)
    s = s + jnp.concatenate([bias_c] * NSA_HEADS, axis=1)
    mx = jnp.maximum(jnp.max(s, axis=0, keepdims=True), NEG_BIG)
    e = jnp.exp(s - mx)
    pc = (e / jnp.maximum(jnp.sum(e, axis=0, keepdims=True), 1e-30)).astype(BF16)
    out = gate_row(0) * jnp.dot(vcmpt_ref[...], pc, preferred_element_type=F32)

    imp8 = jnp.dot(ovl_ref[...], pc, preferred_element_type=F32)
    imp = imp8[:, 0:Q_TILE]
    for h in range(1, NSA_HEADS):
        imp = imp + imp8[:, h * Q_TILE:(h + 1) * Q_TILE]
    n_blk = imp.shape[0]
    blk = lax.broadcasted_iota(I32, (n_blk, Q_TILE), 0)
    cur = tq // SEL_BLOCK
    forced = (blk == 0) | (blk == cur) | (blk == cur - 1)
    imp = jnp.where(blk <= cur, jnp.where(forced, FORCED_SCORE, imp), -jnp.inf)
    chosen = jnp.zeros((n_blk, Q_TILE), F32)
    for _ in range(n_sel):
        top = jnp.max(imp, axis=0, keepdims=True)
        first = jnp.min(jnp.where(imp == top, blk, n_blk), axis=0, keepdims=True)
        pick = (blk == first) & (top > -jnp.inf)
        chosen = jnp.where(pick, 1.0, chosen)
        imp = jnp.where(pick, -jnp.inf, imp)
    sel_ref[...] = chosen.astype(BF16)

    def sel_mask(r0, carry):
        hit = jnp.dot(exp_ref[pl.ds(r0, rows), :], sel_ref[...], preferred_element_type=F32)
        kpos = r0 + lax.broadcasted_iota(I32, (rows, Q_TILE), 0)
        return jnp.where((hit > 0.5) & (kpos <= tq), 0.0, -jnp.inf), carry

    nch = (q0 + Q_TILE + rows - 1) // rows
    o_s = _flash_chunks(qs, ks_ref, vst_ref, 0, nch, rows, sel_mask, 0, acc_ref, m_ref, l_ref, NSA_HEADS)
    out = out + gate_row(1) * o_s

    band = WINDOW + Q_TILE
    w0 = pl.multiple_of(jnp.maximum(q0 - WINDOW, 0), Q_TILE)
    kpos = w0 + lax.broadcasted_iota(I32, (band, Q_TILE), 0)
    bias_w = jnp.where((kpos <= tq) & (kpos > tq - WINDOW), 0.0, -jnp.inf)
    sw = lax.dot_general(kw_ref[pl.ds(w0, band), :], qs, _NT, preferred_element_type=F32)
    sw = sw + jnp.concatenate([bias_w] * NSA_HEADS, axis=1)
    ew = jnp.exp(sw - jnp.max(sw, axis=0, keepdims=True))
    o_w = jnp.dot(vwt_ref[:, pl.ds(w0, band)], ew.astype(BF16), preferred_element_type=F32)
    out = out + gate_row(2) * (o_w / jnp.sum(ew, axis=0, keepdims=True))
    o_ref[...] = _unstack_heads_t(out, NSA_HEADS).astype(o_ref.dtype)


def _nsa_attention(c, kcmp, vcmpt, bsz, seq):
    t = bsz * seq
    nq = seq // Q_TILE
    ng = seq // CMP_STRIDE
    n_cmp = (seq - CMP_BLOCK) // CMP_STRIDE + 1
    n_blk = seq // SEL_BLOCK
    n_sel = min(N_SEL_BLOCKS, n_blk)
    j = np.arange(ng)[None, :] * CMP_STRIDE
    b0 = np.arange(n_blk)[:, None] * SEL_BLOCK
    ovl = ((j <= b0 + SEL_BLOCK - 1) & (j + CMP_BLOCK - 1 >= b0) & (np.arange(ng)[None, :] < n_cmp))
    ovl = jnp.asarray(ovl.astype(np.float32), BF16)
    expand = jnp.asarray((np.arange(seq)[:, None] // SEL_BLOCK == np.arange(n_blk)[None, :]).astype(np.float32), BF16)
    qrow = lambda w: pl.BlockSpec((Q_TILE, w), lambda b, n: (b * nq + n, 0))
    seqrow = lambda w: pl.BlockSpec((seq, w), lambda b, n: (b, 0))
    vrow = lambda r: pl.BlockSpec((HEAD_DIM, seq), lambda b, n: (r, b))
    const = lambda a: pl.BlockSpec(a.shape, lambda b, n: (0, 0))
    nqh = NSA_HEADS * Q_TILE
    return pl.pallas_call(
        functools.partial(_nsa_body, n_sel=n_sel, n_cmp=n_cmp),
        grid=(bsz, nq),
        in_specs=[qrow(512), qrow(128),
                  pl.BlockSpec((ng, HEAD_DIM), lambda b, n: (b, 0)), pl.BlockSpec((HEAD_DIM, ng), lambda b, n: (0, b)),
                  const(ovl), const(expand), seqrow(HEAD_DIM), vrow(1), seqrow(HEAD_DIM), vrow(2)],
        out_specs=qrow(512),
        out_shape=jax.ShapeDtypeStruct((t, 512), BF16),
        scratch_shapes=[pltpu.VMEM((n_blk, Q_TILE), BF16), pltpu.VMEM((HEAD_DIM, nqh), F32),
                        pltpu.VMEM((1, nqh), F32), pltpu.VMEM((1, nqh), F32)],
        compiler_params=_cparams("parallel", "arbitrary"),
        name="nsa_attention",
    )(c["nq"], c["misc"], kcmp, vcmpt, ovl, expand, c["nks"], c["vt"], c["nkw"], c["vt"])


GDN_TILE = 512


def _split_bf16(a):
    hi = a.astype(BF16)
    return hi, (a - hi.astype(F32)).astype(BF16)


def _dot3(a, b, dims=None):
    ah, al = _split_bf16(a)
    bh, bl = _split_bf16(b)
    if dims is None:
        f = lambda u, v: jnp.dot(u, v, preferred_element_type=F32)
    else:
        f = lambda u, v: lax.dot_general(u, v, dims, preferred_element_type=F32)
    return f(ah, bh) + (f(ah, bl) + f(al, bh))


def _bdot(a, b, dims=None):
    a, b = a.astype(BF16), b.astype(BF16)
    if dims is None:
        return jnp.dot(a, b, preferred_element_type=F32)
    return lax.dot_general(a, b, dims, preferred_element_type=F32)


def _gdn_body(qkv_ref, misc_ref, z_ref, convw_ref, alog_ref, dtb_ref, normg_ref, o_ref,
              state_ref, tail_ref, q_s, k_s, v_s, gc_s, beta_s):
    tt = qkv_ref.shape[0]
    ch = GDN_CHUNK
    hw = GDN_HEADS * GDN_DIM

    @pl.when(pl.program_id(1) == 0)
    def _():
        state_ref[...] = jnp.zeros(state_ref.shape, F32)
        tail_ref[...] = jnp.zeros(tail_ref.shape, F32)

    x = qkv_ref[...]
    w = convw_ref[...]
    tail = tail_ref[...]
    x8 = x[0:8, :]
    row8 = lax.broadcasted_iota(I32, (8, 1), 0)
    acc = x * w[CONV_WIDTH - 1:CONV_WIDTH, :]
    acc8 = x8 * w[CONV_WIDTH - 1:CONV_WIDTH, :]
    for k in range(1, CONV_WIDTH):
        wk = w[CONV_WIDTH - 1 - k:CONV_WIDTH - k, :]
        acc = acc + pltpu.roll(x, k, 0) * wk
        acc8 = acc8 + jnp.where(row8 < k, pltpu.roll(tail, k, 0), pltpu.roll(x8, k, 0)) * wk
    tail_ref[...] = x[tt - 8:tt, :]
    rowt = lax.broadcasted_iota(I32, (tt, 1), 0)
    acc = jnp.where(rowt < 8, jnp.concatenate([acc8, acc[8:, :]], axis=0), acc)
    xc = acc * jax.nn.sigmoid(acc)

    def l2n(a):
        return a * lax.rsqrt(jnp.sum(a * a, axis=-1, keepdims=True) + RMS_EPS)

    for h in range(GDN_HEADS):
        q_s[h] = l2n(xc[:, h * GDN_DIM:(h + 1) * GDN_DIM]) * (GDN_DIM ** -0.5)
        k_s[h] = l2n(xc[:, hw + h * GDN_DIM:hw + (h + 1) * GDN_DIM])
        v_s[h] = xc[:, 2 * hw + h * GDN_DIM:2 * hw + (h + 1) * GDN_DIM]

    misc = misc_ref[...]
    sp = misc + dtb_ref[...]
    softplus = jnp.maximum(sp, 0.0) + jnp.log(1.0 + jnp.exp(-jnp.abs(sp)))
    g = -jnp.exp(alog_ref[...]) * softplus
    beta_s[...] = jax.nn.sigmoid(misc)
    ri = lax.broadcasted_iota(I32, (tt, tt), 0)
    ci = lax.broadcasted_iota(I32, (tt, tt), 1)
    blk_tril = jnp.where((ri >= ci) & (ri // ch == ci // ch), 1.0, 0.0)
    gc_s[...] = jnp.dot(blk_tril, g, preferred_element_type=F32, precision=lax.Precision.HIGHEST)

    r64 = lax.broadcasted_iota(I32, (ch, ch), 0)
    c64 = lax.broadcasted_iota(I32, (ch, ch), 1)
    tri = r64 >= c64
    strict = r64 > c64
    eye = jnp.where(r64 == c64, 1.0, 0.0)
    norm_g = normg_ref[...]

    def chunk_step(c, _):
        r0 = pl.multiple_of(c * ch, ch)
        gcs = gc_s[pl.ds(r0, ch), :]
        gct = jnp.concatenate([gcs, jnp.zeros_like(gcs)], axis=0).T
        bts = beta_s[pl.ds(r0, ch), :]
        heads = range(GDN_HEADS)
        q = [q_s[h, pl.ds(r0, ch), :] for h in heads]
        k = [k_s[h, pl.ds(r0, ch), :] for h in heads]
        gcol = [gcs[:, MISC_GA + h:MISC_GA + h + 1] for h in heads]
        decay = [jnp.exp(jnp.where(tri, gcol[h] - gct[MISC_GA + h:MISC_GA + h + 1, 0:ch], -jnp.inf)) for h in heads]
        beta = [bts[:, MISC_GB + h:MISC_GB + h + 1] for h in heads]
        kb = [k[h] * beta[h] for h in heads]
        eg = [jnp.exp(gcol[h]) for h in heads]
        nmat = [-jnp.where(strict, _bdot(kb[h], k[h], _NT) * decay[h], 0.0) for h in heads]
        y = [jnp.concatenate([v_s[h, pl.ds(r0, ch), :] * beta[h], kb[h] * eg[h]], axis=1) for h in heads]
        for level in range(6):
            y = [y[h] + _dot3(nmat[h], y[h]) for h in heads]
            if level < 5:
                nmat = [_dot3(nmat[h], nmat[h]) for h in heads]
        attn = [jnp.where(tri, _bdot(q[h], k[h], _NT) * decay[h], 0.0) for h in heads]
        glast = [gcol[h][ch - 1:ch, :] for h in heads]
        state = [state_ref[h] for h in heads]
        v_new = [y[h][:, :GDN_DIM] - _bdot(y[h][:, GDN_DIM:], state[h]) for h in heads]
        o = [_bdot(q[h] * eg[h], state[h]) + _bdot(attn[h], v_new[h]) for h in heads]
        for h in heads:
            k_dec = k[h] * jnp.exp(glast[h] - gcol[h])
            state_ref[h] = state[h] * jnp.exp(glast[h]) + _bdot(k_dec, v_new[h], (((0,), (0,)), ((), ())))
        for h in heads:
            on = o[h] * lax.rsqrt(jnp.mean(o[h] * o[h], axis=-1, keepdims=True) + RMS_EPS) * norm_g
            zz = z_ref[pl.ds(r0, ch), h * GDN_DIM:(h + 1) * GDN_DIM]
            o_ref[pl.ds(r0, ch), h * GDN_DIM:(h + 1) * GDN_DIM] = (on * (zz * jax.nn.sigmoid(zz))).astype(o_ref.dtype)
        return 0

    lax.fori_loop(0, tt // ch, chunk_step, 0)


def _gdn_mixer(c, conv_w, a_log, dt_bias, norm_g, bsz, seq):
    t = bsz * seq
    tt = min(GDN_TILE, seq)
    nt = seq // tt
    lane_row = lambda vals, off: jnp.zeros((1, LANES), F32).at[0, off:off + GDN_HEADS].set(vals.astype(F32))
    alog = lane_row(a_log, MISC_GA)
    dtb = lane_row(dt_bias, MISC_GA)
    row = lambda w: pl.BlockSpec((tt, w), lambda b, j: (b * nt + j, 0))
    const = lambda a: pl.BlockSpec(a.shape, lambda b, j: (0, 0))
    ng = norm_g.reshape(1, GDN_DIM).astype(F32)
    hs = pltpu.VMEM((GDN_HEADS, tt, GDN_DIM), F32)
    return pl.pallas_call(
        _gdn_body,
        grid=(bsz, nt),
        in_specs=[row(GDN_QKV), row(LANES), row(GDN_HEADS * GDN_DIM), const(conv_w), const(alog), const(dtb), const(ng)],
        out_specs=row(GDN_HEADS * GDN_DIM),
        out_shape=jax.ShapeDtypeStruct((t, GDN_HEADS * GDN_DIM), BF16),
        scratch_shapes=[pltpu.VMEM((GDN_HEADS, GDN_DIM, GDN_DIM), F32), pltpu.VMEM((8, GDN_QKV), F32),
                        hs, hs, hs, pltpu.VMEM((tt, LANES), F32), pltpu.VMEM((tt, LANES), F32)],
        compiler_params=_cparams("parallel", "arbitrary"),
        name="gdn_mixer",
    )(c["gqkv"], c["misc"], c["gz"], conv_w.astype(F32), alog, dtb, ng)


ROW_TILE = 256


def _layer_norm(v, g, b):
    mu = jnp.mean(v, axis=-1, keepdims=True)
    vc = v - mu
    var = jnp.mean(vc * vc, axis=-1, keepdims=True)
    return vc * lax.rsqrt(var + LN_EPS) * g + b


def _merge_body(ya_ref, yb_ref, yc_ref, mg_ref, x_ref, wb_ref, wo_ref, g_ref, b_ref, wrt_ref,
                x1_ref, x1b_ref, xpk_ref, sct_ref, *, alpha):
    d = x_ref.shape[1]
    merged = mg_ref[:, 0:d] * jnp.dot(ya_ref[...], wb_ref[0], preferred_element_type=F32)
    merged = merged + mg_ref[:, d:2 * d] * jnp.dot(yb_ref[...], wb_ref[1], preferred_element_type=F32)
    merged = merged + mg_ref[:, 2 * d:3 * d] * jnp.dot(yc_ref[...], wb_ref[2], preferred_element_type=F32)
    y = alpha * x_ref[...] + jnp.dot(merged.astype(BF16), wo_ref[...], preferred_element_type=F32)
    x1 = _layer_norm(y, g_ref[...], b_ref[...])
    x1_ref[...] = x1
    x1b = x1.astype(BF16)
    x1b_ref[...] = x1b
    xpk_ref[...] = _pack_bf16_pairs(x1b)
    logits_t = lax.dot_general(wrt_ref[...], x1b, _NT, preferred_element_type=F32)
    sct_ref[...] = jax.nn.sigmoid(logits_t)


def _merge_out(ya, yb, yc, mg, x2d, w_branch, w_out, ln_g, ln_b, w_router, alpha):
    t, d = x2d.shape
    tm = ROW_TILE
    row = lambda w: pl.BlockSpec((tm, w), lambda i: (i, 0))
    wb = w_branch.astype(BF16)
    wo = w_out.astype(BF16)
    wrt = w_router.T.astype(BF16)
    g = ln_g.reshape(1, d)
    b = ln_b.reshape(1, d)
    return pl.pallas_call(
        functools.partial(_merge_body, alpha=alpha),
        grid=(t // tm,),
        in_specs=[row(BRANCH_WIDTH)] * 3 + [row(3 * d), row(d),
                  pl.BlockSpec(wb.shape, lambda i: (0, 0, 0)), pl.BlockSpec(wo.shape, lambda i: (0, 0)),
                  pl.BlockSpec((1, d), lambda i: (0, 0)), pl.BlockSpec((1, d), lambda i: (0, 0)),
                  pl.BlockSpec(wrt.shape, lambda i: (0, 0))],
        out_specs=[row(d), row(d), row(d // 2), pl.BlockSpec((N_EXPERTS, tm), lambda i: (0, i))],
        out_shape=[jax.ShapeDtypeStruct((t, d), F32), jax.ShapeDtypeStruct((t, d), BF16),
                   jax.ShapeDtypeStruct((t, d // 2), jnp.uint32), jax.ShapeDtypeStruct((N_EXPERTS, t), F32)],
        compiler_params=_cparams("parallel"),
        name="merge_out_ln",
    )(ya, yb, yc, mg, x2d, wb, wo, g, b, wrt)


ROUTE_TILE = 512


def _route_body(sct_ref, bias_ref, eidx_ref, gate_ref, rank_ref, cnt_ref):
    sc = sct_ref[...]
    n = sc.shape[1]
    per = N_EXPERTS // N_GROUPS
    biased = sc + bias_ref[...]
    b3 = biased.reshape(N_GROUPS, per, n)
    sub = lax.broadcasted_iota(I32, (N_GROUPS, per, n), 1)
    m1 = jnp.max(b3, axis=1, keepdims=True)
    first = jnp.min(jnp.where(b3 == m1, sub, per), axis=1, keepdims=True)
    m2 = jnp.max(jnp.where(sub == first, -jnp.inf, b3), axis=1, keepdims=True)
    gs = (m1 + m2).reshape(N_GROUPS, n)
    gi = lax.broadcasted_iota(I32, (N_GROUPS, n), 0)
    gmask = jnp.zeros((N_GROUPS, n), F32)
    for _ in range(TOPK_GROUPS):
        top = jnp.max(gs, axis=0, keepdims=True)
        pick = gi == jnp.min(jnp.where(gs == top, gi, N_GROUPS), axis=0, keepdims=True)
        gmask = jnp.where(pick, 1.0, gmask)
        gs = jnp.where(pick, -jnp.inf, gs)
    emask = jnp.broadcast_to(gmask.reshape(N_GROUPS, 1, n), (N_GROUPS, per, n)).reshape(N_EXPERTS, n)
    cand = jnp.where(emask > 0.5, biased, -jnp.inf)
    ei = lax.broadcasted_iota(I32, (N_EXPERTS, n), 0)
    ids, gates, picks = [], [], []
    for _ in range(TOP_K):
        top = jnp.max(cand, axis=0, keepdims=True)
        idx = jnp.min(jnp.where(cand == top, ei, N_EXPERTS), axis=0, keepdims=True)
        pick = ei == idx
        ids.append(idx)
        picks.append(pick)
        gates.append(jnp.sum(jnp.where(pick, sc, 0.0), axis=0, keepdims=True))
        cand = jnp.where(pick, -jnp.inf, cand)
    gate = jnp.concatenate(gates, axis=0)
    gate = gate / jnp.sum(gate, axis=0, keepdims=True) * ROUTED_SCALE
    eidx_ref[...] = jnp.concatenate(ids, axis=0)
    gate_ref[...] = gate

    @pl.when(pl.program_id(0) == 0)
    def _():
        cnt_ref[...] = jnp.zeros(cnt_ref.shape, F32)

    sel = jnp.where(cand == -jnp.inf, 1.0, 0.0) * jnp.where(emask > 0.5, 1.0, 0.0)
    ri = lax.broadcasted_iota(I32, (n, n), 0)
    ci = lax.broadcasted_iota(I32, (n, n), 1)
    before = jnp.dot(sel.astype(BF16), jnp.where(ri < ci, 1.0, 0.0).astype(BF16), preferred_element_type=F32)
    pos = before + cnt_ref[...]
    rank_ref[...] = jnp.concatenate(
        [jnp.sum(jnp.where(pk, pos, 0.0), axis=0, keepdims=True) for pk in picks], axis=0).astype(I32)
    cnt_ref[...] = cnt_ref[...] + jnp.sum(sel, axis=1, keepdims=True)


def _route(sct, router_bias):
    e, t = sct.shape
    tn = ROUTE_TILE
    col = lambda r: pl.BlockSpec((r, tn), lambda i: (0, i))
    return pl.pallas_call(
        _route_body,
        grid=(t // tn,),
        in_specs=[col(e), pl.BlockSpec((e, 1), lambda i: (0, 0))],
        out_specs=[col(TOP_K), col(TOP_K), col(TOP_K), pl.BlockSpec((e, 1), lambda i: (0, 0))],
        out_shape=[jax.ShapeDtypeStruct((TOP_K, t), I32), jax.ShapeDtypeStruct((TOP_K, t), F32),
                   jax.ShapeDtypeStruct((TOP_K, t), I32), jax.ShapeDtypeStruct((e, 1), F32)],
        compiler_params=_cparams("arbitrary"),
        name="route_topk",
    )(sct, router_bias.reshape(e, 1).astype(F32))


U32 = jnp.uint32
DISPATCH_TILE = 512


def _pack_bf16_pairs(xb):
    n = xb.shape[1] // 2
    bits = pltpu.bitcast(xb.astype(F32), U32)
    return (bits[:, :n] >> 16) | (bits[:, n:] & jnp.uint32(0xFFFF0000))


def _unpack_bf16_pairs(packed):
    lo = pltpu.bitcast(packed << 16, F32).astype(BF16)
    hi = pltpu.bitcast(packed & jnp.uint32(0xFFFF0000), F32).astype(BF16)
    return lo, hi


def _dispatch_body(dest_hbm, x_ref, init_hbm, xs_hbm, idx_smem, isem, rsem):
    del init_hbm
    i = pl.program_id(0)
    tn = DISPATCH_TILE
    idx_cp = pltpu.make_async_copy(dest_hbm.at[i], idx_smem, isem)
    idx_cp.start()
    idx_cp.wait()

    def issue(j, _):
        src = x_ref.at[pl.ds(j, 1)]
        for k in range(TOP_K):
            pltpu.make_async_copy(src, xs_hbm.at[pl.ds(idx_smem[k * tn + j], 1)], rsem).start(priority=k % 2)
        return 0

    lax.fori_loop(0, tn, issue, 0)
    for k in range(TOP_K):
        pltpu.make_async_copy(x_ref, xs_hbm.at[pl.ds(0, tn)], rsem).wait()


def _dispatch(xpk, dest_tiles, n_pad):
    t, w = xpk.shape
    tn = DISPATCH_TILE
    anyspec = pl.BlockSpec(memory_space=pl.ANY)
    return pl.pallas_call(
        _dispatch_body,
        grid=(t // tn,),
        in_specs=[anyspec, pl.BlockSpec((tn, w), lambda i: (i, 0)), anyspec],
        out_specs=anyspec,
        out_shape=jax.ShapeDtypeStruct((n_pad, w), xpk.dtype),
        scratch_shapes=[pltpu.SMEM((TOP_K * tn,), I32), pltpu.SemaphoreType.DMA, pltpu.SemaphoreType.DMA],
        input_output_aliases={2: 0},
        compiler_params=_cparams("arbitrary"),
        name="moe_dispatch",
    )(dest_tiles, xpk, jnp.zeros((n_pad, w), xpk.dtype))


def _expert_body(blk_e_ref, nused_ref, xb_ref, wgu_ref, wd_ref, o_ref):
    i = pl.program_id(0)
    half = xb_ref.shape[1]

    @pl.when(i < nused_ref[0])
    def _():
        lo, hi = _unpack_bf16_pairs(xb_ref[...])
        gu = (jnp.dot(lo, wgu_ref[0, :half, :], preferred_element_type=F32)
              + jnp.dot(hi, wgu_ref[0, half:, :], preferred_element_type=F32))
        gg, uu = gu[:, :D_EXPERT], gu[:, D_EXPERT:]
        hmid = (gg * jax.nn.sigmoid(gg) * uu).astype(BF16)
        o_ref[...] = jnp.dot(hmid, wd_ref[0], preferred_element_type=F32)

    @pl.when(i >= nused_ref[0])
    def _():
        o_ref[...] = jnp.zeros(o_ref.shape, o_ref.dtype)


def _expert_ffn(xs, blk_e, n_used, wgu, wd):
    n_pad, half = xs.shape
    d = 2 * half
    nb = n_pad // EXPERT_BLOCK
    grid_spec = pltpu.PrefetchScalarGridSpec(
        num_scalar_prefetch=2,
        grid=(nb,),
        in_specs=[pl.BlockSpec((EXPERT_BLOCK, half), lambda i, be, nu: (i, 0)),
                  pl.BlockSpec((1, d, 2 * D_EXPERT), lambda i, be, nu: (be[i], 0, 0)),
                  pl.BlockSpec((1, D_EXPERT, d), lambda i, be, nu: (be[i], 0, 0))],
        out_specs=pl.BlockSpec((EXPERT_BLOCK, d), lambda i, be, nu: (i, 0)),
    )
    return pl.pallas_call(
        _expert_body,
        grid_spec=grid_spec,
        out_shape=jax.ShapeDtypeStruct((n_pad, d), F32),
        compiler_params=_cparams("arbitrary"),
        name="expert_ffn",
    )(blk_e, n_used, xs, wgu, wd)


def _dest_body(eidx_ref, rank_ref, start_ref, dest_ref):
    n = eidx_ref.shape[1]
    ei = lax.broadcasted_iota(I32, (N_EXPERTS, n), 0)
    start = start_ref[...]
    rows = [jnp.sum(jnp.where(ei == eidx_ref[k:k + 1, :], start, 0.0), axis=0, keepdims=True)
            for k in range(TOP_K)]
    dest_ref[...] = rank_ref[...] + jnp.concatenate(rows, axis=0).astype(I32)


def _moe_plan(eidx_t, rank_t, counts, n_tok):
    n_rows = n_tok * TOP_K
    n_blocks = -(-n_rows // EXPERT_BLOCK) + N_EXPERTS
    cnt = counts.reshape(-1).astype(I32)
    padded = (cnt + EXPERT_BLOCK - 1) // EXPERT_BLOCK * EXPERT_BLOCK
    pad_end = jnp.cumsum(padded)
    pad_start = pad_end - padded
    tn = ROUTE_TILE
    col = pl.BlockSpec((TOP_K, tn), lambda i: (0, i))
    dest = pl.pallas_call(
        _dest_body,
        grid=(n_tok // tn,),
        in_specs=[col, col, pl.BlockSpec((N_EXPERTS, 1), lambda i: (0, 0))],
        out_specs=col,
        out_shape=jax.ShapeDtypeStruct((TOP_K, n_tok), I32),
        compiler_params=_cparams("parallel"),
        name="route_dest",
    )(eidx_t, rank_t, pad_start.astype(F32).reshape(N_EXPERTS, 1))
    blk_first = jnp.arange(n_blocks, dtype=I32) * EXPERT_BLOCK
    blk_e = jnp.minimum(jnp.sum((pad_end[None, :] <= blk_first[:, None]).astype(I32), axis=1), N_EXPERTS - 1)
    n_used = (pad_end[-1] // EXPERT_BLOCK).astype(I32).reshape(1)
    return dest, blk_e, n_used, n_blocks * EXPERT_BLOCK


def _tile_indices(dest, tn):
    k, t = dest.shape
    return dest.reshape(k, t // tn, tn).transpose(1, 0, 2).reshape(t // tn, k * tn)


def _ffn_out_body(dest_hbm, ye_hbm, gate_ref, x1_ref, x1b_ref, p_ref, wsgu_ref, wsd_ref, wpg_ref, wpp_ref,
                  g_ref, b_ref, o_ref, idx_smem, rows_ref, isem, rsem, *, alpha):
    i = pl.program_id(0)
    n_tiles = pl.num_programs(0)
    tm = x1_ref.shape[0]

    def gather_tile(tile, s):
        idx_cp = pltpu.make_async_copy(dest_hbm.at[tile], idx_smem.at[s], isem)
        idx_cp.start()
        idx_cp.wait()

        def issue(j, _):
            for k in range(TOP_K):
                pltpu.make_async_copy(ye_hbm.at[pl.ds(idx_smem[s, k * tm + j], 1)],
                                      rows_ref.at[s, k, pl.ds(j, 1)], rsem.at[s]).start(priority=k % 2)
            return 0

        lax.fori_loop(0, tm, issue, 0)

    def run(slot):
        @pl.when(i + 1 < n_tiles)
        def _():
            gather_tile(i + 1, 1 - slot)

        xb = x1b_ref[...]
        su = jnp.dot(xb, wsgu_ref[...], preferred_element_type=F32)
        sg, uu = su[:, :D_SHARED], su[:, D_SHARED:]
        shared = jnp.dot((sg * jax.nn.sigmoid(sg) * uu).astype(BF16), wsd_ref[...], preferred_element_type=F32)
        pg = jax.nn.sigmoid(jnp.dot(xb, wpg_ref[...], preferred_element_type=F32))
        ple = pg * jnp.dot(p_ref[...].astype(BF16), wpp_ref[...], preferred_element_type=F32)

        g8 = gate_ref[...]
        gate_cols = jnp.concatenate([g8, jnp.zeros((LANES - TOP_K, tm), F32)], axis=0).T
        for k in range(TOP_K):
            pltpu.make_async_copy(ye_hbm.at[pl.ds(0, tm)], rows_ref.at[slot, k], rsem.at[slot]).wait()
        routed = gate_cols[:, 0:1] * rows_ref[slot, 0]
        for k in range(1, TOP_K):
            routed = routed + gate_cols[:, k:k + 1] * rows_ref[slot, k]
        y = alpha * x1_ref[...] + (routed + shared) + ple
        o_ref[...] = _layer_norm(y, g_ref[...], b_ref[...])

    @pl.when(i == 0)
    def _():
        gather_tile(0, 0)

    @pl.when(i % 2 == 0)
    def _():
        run(0)

    @pl.when(i % 2 == 1)
    def _():
        run(1)


def _ffn_out(dest_tiles, ye, gate_t, x1, x1b, p2d, w_sgu, w_sd, w_pg, w_pp, ln_g, ln_b, alpha):
    t, d = x1.shape
    tm = ROW_TILE
    row = lambda w: pl.BlockSpec((tm, w), lambda i: (i, 0))
    const = lambda a: pl.BlockSpec(a.shape, lambda i: (0, 0))
    anyspec = pl.BlockSpec(memory_space=pl.ANY)
    ws = [w.astype(BF16) for w in (w_sgu, w_sd, w_pg, w_pp)]
    g = ln_g.reshape(1, d)
    b = ln_b.reshape(1, d)
    return pl.pallas_call(
        functools.partial(_ffn_out_body, alpha=alpha),
        grid=(t // tm,),
        in_specs=[anyspec, anyspec, pl.BlockSpec((TOP_K, tm), lambda i: (0, i)), row(d), row(d), row(PLE_DIM)]
                 + [const(w) for w in ws] + [const(g), const(b)],
        out_specs=row(d),
        out_shape=jax.ShapeDtypeStruct((t, d), F32),
        scratch_shapes=[pltpu.SMEM((2, TOP_K * tm), I32), pltpu.VMEM((2, TOP_K, tm, d), F32),
                        pltpu.SemaphoreType.DMA, pltpu.SemaphoreType.DMA((2,))],
        compiler_params=_cparams("arbitrary"),
        name="ffn_out_ln",
    )(dest_tiles, ye, gate_t, x1, x1b, p2d, *ws, g, b)


def kernel(x, p, w_in, w_dsa_uv, w_cmp_k, pe_cmp_k, w_cmp_v, pe_cmp_v, conv_w, a_log, dt_bias, gdn_norm, w_branch, w_out, ln1_g, ln1_b, w_router, router_bias, w_gate_up, w_down, w_sh_gate_up, w_sh_down, w_ple_proj, w_ple_gate, ln2_g, ln2_b):
    bsz, seq, d = x.shape
    depth = w_in.shape[0]
    alpha = (2 * depth) ** 0.25
    t = bsz * seq
    tabs64, _ = _rope_lane_tables(seq, HEAD_DIM)
    tabs32, _ = _rope_lane_tables(seq, IDX_DIM)
    x2d = x.reshape(t, d)
    for i in range(depth):
        c = _in_projection(x2d, _split_w_in(w_in[i]), tabs64, tabs32, seq)
        ya = _dsa_attention(c, w_dsa_uv[i], bsz, seq)
        kcmp, vcmpt = _nsa_compress(c, w_cmp_k[i], pe_cmp_k[i], w_cmp_v[i], pe_cmp_v[i], bsz, seq)
        yb = _nsa_attention(c, kcmp, vcmpt, bsz, seq)
        yc = _gdn_mixer(c, conv_w[i], a_log[i], dt_bias[i], gdn_norm[i], bsz, seq)
        x1, x1b, xpk, sct = _merge_out(ya, yb, yc, c["mg"], x2d, w_branch[i], w_out[i], ln1_g[i], ln1_b[i],
                                       w_router[i], alpha)
        eidx_t, gate_t, rank_t, counts = _route(sct, router_bias[i])
        dest, blk_e, n_used, n_pad = _moe_plan(eidx_t, rank_t, counts, t)
        xs = _dispatch(xpk, _tile_indices(dest, DISPATCH_TILE), n_pad)
        ye = _expert_ffn(xs, blk_e, n_used, w_gate_up[i].astype(BF16), w_down[i].astype(BF16))
        x2d = _ffn_out(_tile_indices(dest, ROW_TILE), ye, gate_t, x1, x1b, p[i].reshape(t, PLE_DIM),
                       w_sh_gate_up[i], w_sh_down[i], w_ple_gate[i], w_ple_proj[i], ln2_g[i], ln2_b[i], alpha)
    return x2d.reshape(bsz, seq, d)
```

```python
import functools
import math

import numpy as np
import jax
import jax.numpy as jnp
from jax import lax
from jax.experimental import pallas as pl
from jax.experimental.pallas import tpu as pltpu

F32 = jnp.float32
BF16 = jnp.bfloat16
I32 = jnp.int32

HEAD_DIM = 64
ROPE_THETA = 500000.0
ROPE_FRACTION = 4
DSA_HEADS = 8
DSA_LATENT = 64
IDX_HEADS = 4
IDX_DIM = 32
DSA_TOPK = 256
NSA_HEADS = 8
CMP_BLOCK = 32
CMP_STRIDE = 16
SEL_BLOCK = 32
N_SEL_BLOCKS = 8
WINDOW = 256
FORCED_SCORE = 1.0e4
GDN_HEADS = 4
GDN_DIM = 128
CONV_WIDTH = 4
GDN_CHUNK = 64
BRANCH_WIDTH = 512
N_BRANCHES = 3
GDN_QKV = 3 * GDN_HEADS * GDN_DIM
N_EXPERTS = 64
TOP_K = 8
N_GROUPS = 8
TOPK_GROUPS = 4
D_EXPERT = 256
D_SHARED = 256
ROUTED_SCALE = 2.5
EXPERT_BLOCK = 512
PLE_DIM = 256
LN_EPS = 1e-5
RMS_EPS = 1e-6

IN_SPLITS = (
    ("dsa_q", DSA_HEADS * DSA_LATENT), ("dsa_kv", DSA_LATENT),
    ("idx_q", IDX_HEADS * IDX_DIM), ("idx_k", IDX_DIM), ("idx_w", IDX_HEADS),
    ("nsa_q", NSA_HEADS * HEAD_DIM),
    ("nsa_kc", HEAD_DIM), ("nsa_vc", HEAD_DIM),
    ("nsa_ks", HEAD_DIM), ("nsa_vs", HEAD_DIM),
    ("nsa_kw", HEAD_DIM), ("nsa_vw", HEAD_DIM),
    ("nsa_g", NSA_HEADS * 3),
    ("gdn_qkv", GDN_QKV), ("gdn_a", GDN_HEADS), ("gdn_b", GDN_HEADS), ("gdn_z", GDN_HEADS * GDN_DIM),
    ("merge_g", N_BRANCHES * 1024),
)

LANES = 128
Q_TILE = 128
KEY_CHUNK = 512
TIE_BLOCK = 256
VMEM_LIMIT = 56 * 1024 * 1024
INT_MIN = -2 ** 31
NEG_BIG = -1e30


def _cparams(*sem):
    return pltpu.CompilerParams(dimension_semantics=sem, vmem_limit_bytes=VMEM_LIMIT)


def _rope_lane_tables(seq, head_dim):
    rot = head_dim // ROPE_FRACTION
    half = rot // 2
    inv_freq = ROPE_THETA ** (-(jnp.arange(half, dtype=F32) * 2.0 / rot))
    ang = jnp.arange(seq, dtype=F32)[:, None] * inv_freq[None, :]
    cos, sin = jnp.cos(ang), jnp.sin(ang)
    one = jnp.ones((seq, head_dim - rot), F32)
    c = jnp.concatenate([cos, cos, one], axis=-1)
    sa = jnp.concatenate([-sin, jnp.zeros((seq, head_dim - half), F32)], axis=-1)
    sb = jnp.concatenate([jnp.zeros((seq, half), F32), sin, jnp.zeros((seq, head_dim - rot), F32)], axis=-1)
    rep = LANES // head_dim
    return tuple(jnp.tile(a, (1, rep)) for a in (c, sa, sb)), half


def _rope_apply(y, c, sa, sb, half):
    n = y.shape[-1]
    rep = n // LANES
    if rep > 1:
        c, sa, sb = (jnp.concatenate([a] * rep, axis=-1) for a in (c, sa, sb))
    up = pltpu.roll(y, n - half, 1)
    dn = pltpu.roll(y, half, 1)
    return y * c + up * sa + dn * sb


IN_TILE = 256


def _inproj_body(x_ref, w64_ref, w32_ref, wpl_ref, wvt_ref, wmg_ref,
                 c64_ref, sa64_ref, sb64_ref, c32_ref, sa32_ref, sb32_ref,
                 dq_ref, nq_ref, krot_ref, nkc_ref, nks_ref, nkw_ref, iq_ref, ik_ref,
                 nvc_ref, misc_ref, gz_ref, gqkv_ref, vt_ref, mg_ref):
    xb = x_ref[...].astype(BF16)
    c64, sa64, sb64 = c64_ref[...], sa64_ref[...], sb64_ref[...]
    half64 = HEAD_DIM // ROPE_FRACTION // 2
    half32 = IDX_DIM // ROPE_FRACTION // 2

    def proj(w_ref, lo, hi):
        return jnp.dot(xb, w_ref[:, lo:hi], preferred_element_type=F32)

    dq_ref[...] = _rope_apply(proj(w64_ref, 0, 512), c64, sa64, sb64, half64).astype(BF16)
    nq_ref[...] = _rope_apply(proj(w64_ref, 512, 1024), c64, sa64, sb64, half64).astype(BF16)
    k1 = _rope_apply(proj(w64_ref, 1024, 1152), c64, sa64, sb64, half64).astype(BF16)
    krot_ref[...] = k1[:, :64]
    nkc_ref[...] = k1[:, 64:]
    k2 = _rope_apply(proj(w64_ref, 1152, 1280), c64, sa64, sb64, half64).astype(BF16)
    nks_ref[...] = k2[:, :64]
    nkw_ref[...] = k2[:, 64:]
    c32, sa32, sb32 = c32_ref[...], sa32_ref[...], sb32_ref[...]
    iq_ref[...] = _rope_apply(proj(w32_ref, 0, 128), c32, sa32, sb32, half32).astype(BF16)
    ik = _rope_apply(proj(w32_ref, 128, 256), c32, sa32, sb32, half32).astype(BF16)
    ik_ref[...] = ik[:, :IDX_DIM]
    pl0 = proj(wpl_ref, 0, 128)
    nvc_ref[...] = pl0[:, :64].astype(BF16)
    misc_ref[...] = proj(wpl_ref, 128, 256)
    gz_ref[...] = proj(wpl_ref, 256, 768)
    for j in range(3):
        gqkv_ref[:, j * 512:(j + 1) * 512] = proj(wpl_ref, 768 + j * 512, 768 + (j + 1) * 512)
    vt_ref[...] = lax.dot_general(wvt_ref[...], xb, (((1,), (1,)), ((), ())),
                                  preferred_element_type=F32).astype(BF16)
    for j in range(6):
        g = proj(wmg_ref, j * 512, (j + 1) * 512)
        mg_ref[:, j * 512:(j + 1) * 512] = jax.nn.sigmoid(g)


def _split_w_in(w):
    cols = {}
    off = 0
    for name, size in IN_SPLITS:
        cols[name] = w[:, off:off + size]
        off += size
    d = w.shape[0]
    z = lambda n: jnp.zeros((d, n), w.dtype)
    w64 = jnp.concatenate([cols["dsa_q"], cols["nsa_q"], cols["dsa_kv"], cols["nsa_kc"],
                           cols["nsa_ks"], cols["nsa_kw"]], axis=1)
    w32 = jnp.concatenate([cols["idx_q"], cols["idx_k"], z(LANES - IDX_DIM)], axis=1)
    misc = jnp.concatenate([cols["idx_w"], cols["gdn_a"], cols["gdn_b"], cols["nsa_g"]], axis=1)
    misc = jnp.concatenate([misc, z(LANES - misc.shape[1])], axis=1)
    wpl = jnp.concatenate([cols["nsa_vc"], z(64), misc, cols["gdn_z"], cols["gdn_qkv"]], axis=1)
    wvt = jnp.concatenate([cols["dsa_kv"], cols["nsa_vs"], cols["nsa_vw"], z(64)], axis=1).T
    return [a.astype(BF16) for a in (w64, w32, wpl, wvt, cols["merge_g"])]


MISC_IW = 0
MISC_GA = IDX_HEADS
MISC_GB = MISC_GA + GDN_HEADS
MISC_NG = MISC_GB + GDN_HEADS


def _in_projection(x2d, wparts, tabs64, tabs32, seq):
    t, d = x2d.shape
    tm = IN_TILE
    nt = seq // tm
    w64, w32, wpl, wvt, wmg = wparts
    row = lambda w: pl.BlockSpec((tm, w), lambda i: (i, 0))
    full = lambda a: pl.BlockSpec(a.shape, lambda i: (0, 0))
    tab = pl.BlockSpec((tm, LANES), lambda i: (i % nt, 0))
    outs = [
        ("dq", 512, BF16), ("nq", 512, BF16), ("krot", 64, BF16), ("nkc", 64, BF16), ("nks", 64, BF16),
        ("nkw", 64, BF16), ("iq", 128, BF16), ("ik", IDX_DIM, BF16), ("nvc", 64, BF16),
        ("misc", 128, F32), ("gz", 512, F32), ("gqkv", GDN_QKV, F32),
    ]
    out_shape = [jax.ShapeDtypeStruct((t, w), dt) for _, w, dt in outs]
    out_specs = [row(w) for _, w, _ in outs]
    out_shape += [jax.ShapeDtypeStruct((256, t), BF16), jax.ShapeDtypeStruct((t, 3 * d), F32)]
    out_specs += [pl.BlockSpec((256, tm), lambda i: (0, i)), row(3 * d)]
    res = pl.pallas_call(
        _inproj_body,
        grid=(t // tm,),
        in_specs=[row(d), full(w64), full(w32), full(wpl), full(wvt), full(wmg)] + [tab] * 6,
        out_specs=out_specs,
        out_shape=out_shape,
        compiler_params=_cparams("parallel"),
        name="in_projection",
    )(x2d, w64, w32, wpl, wvt, wmg, *tabs64, *tabs32)
    names = [n for n, _, _ in outs] + ["vt", "mg"]
    return dict(zip(names, res))


_NT = (((1,), (1,)), ((), ()))


def _stack_heads(q, n_heads, width):
    return jnp.concatenate([q[:, h * width:(h + 1) * width] for h in range(n_heads)], axis=0)


def _unstack_heads_t(o_t, n_heads):
    q = o_t.shape[1] // n_heads
    rows = jnp.concatenate([o_t[:, h * q:(h + 1) * q] for h in range(n_heads)], axis=0)
    return rows.T


def _flash_chunks(qs, k_ref, vt_ref, lo, hi, rows, mask_fn, carry0, acc_ref, m_ref, l_ref, n_heads):
    m_ref[...] = jnp.full(m_ref.shape, NEG_BIG, F32)
    l_ref[...] = jnp.zeros(l_ref.shape, F32)
    acc_ref[...] = jnp.zeros(acc_ref.shape, F32)

    def body(c, carry):
        r0 = pl.multiple_of(c * rows, rows)
        bias, carry = mask_fn(r0, carry)
        kc = k_ref[pl.ds(r0, rows), :]
        s = lax.dot_general(kc, qs, _NT, preferred_element_type=F32)
        s = s + jnp.concatenate([bias] * n_heads, axis=1)
        m_old = m_ref[...]
        m_new = jnp.maximum(m_old, jnp.max(s, axis=0, keepdims=True))
        alpha = jnp.exp(m_old - m_new)
        p = jnp.exp(s - m_new)
        l_ref[...] = alpha * l_ref[...] + jnp.sum(p, axis=0, keepdims=True)
        pv = jnp.dot(vt_ref[:, pl.ds(r0, rows)], p.astype(BF16), preferred_element_type=F32)
        acc_ref[...] = acc_ref[...] * alpha + pv
        m_ref[...] = m_new
        return carry

    lax.fori_loop(lo, hi, body, carry0)
    return acc_ref[...] / jnp.maximum(l_ref[...], 1e-30)


def _dsa_body(iq_ref, dq_ref, misc_ref, ik_ref, krot_ref, vt_ref, wuvt_ref, o_ref,
              keys_ref, acc_ref, m_ref, l_ref, *, n_sel):
    rows = KEY_CHUNK
    sub = TIE_BLOCK
    q0 = pl.program_id(1) * Q_TILE
    nch = (q0 + Q_TILE + rows - 1) // rows
    tq = q0 + lax.broadcasted_iota(I32, (1, Q_TILE), 1)
    misc_t = misc_ref[...].T
    coef = (IDX_HEADS ** -0.5) * (IDX_DIM ** -0.5)
    iq = iq_ref[...]
    int_min = jnp.int32(INT_MIN)
    wrow = [misc_t[MISC_IW + h:MISC_IW + h + 1, :] * coef for h in range(IDX_HEADS)]

    def score_block(r0):
        ikc = ik_ref[pl.ds(r0, sub), :]
        acc = jnp.zeros((sub, Q_TILE), F32)
        for h in range(IDX_HEADS):
            s = lax.dot_general(ikc, iq[:, h * IDX_DIM:(h + 1) * IDX_DIM], _NT, preferred_element_type=F32)
            acc = acc + jnp.maximum(s, 0.0) * wrow[h]
        acc = jnp.where(acc == 0.0, 0.0, acc)
        bits = pltpu.bitcast(acc, I32)
        key = bits ^ ((bits >> 31) & jnp.int32(0x7FFFFFFF))
        kpos = r0 + lax.broadcasted_iota(I32, (sub, Q_TILE), 0)
        keys_ref[pl.ds(r0, sub), :] = jnp.where(kpos <= tq, key, int_min)

    def score_chunk(c, _):
        for j in range(rows // sub):
            score_block(pl.multiple_of(c * rows + j * sub, sub))
        return 0

    lax.fori_loop(0, nch, score_chunk, 0)

    def count(pred):
        def body(c, acc):
            r0 = pl.multiple_of(c * rows, rows)
            hit = jnp.where(pred(keys_ref[pl.ds(r0, rows), :]), 1, 0).astype(I32)
            return acc + jnp.sum(hit.reshape(rows // 8, 8, Q_TILE), axis=0)
        acc = lax.fori_loop(0, nch, body, jnp.zeros((8, Q_TILE), I32))
        return jnp.sum(acc, axis=0, keepdims=True)

    kq = jnp.minimum(tq + 1, n_sel)

    def bit_step(i, tu):
        cand_u = tu | lax.shift_left(jnp.int32(1), 31 - i)
        cand = cand_u ^ int_min
        cnt = count(lambda kk: kk >= cand)
        return jnp.where(cnt >= kq, cand_u, tu)

    thr = lax.fori_loop(0, 32, bit_step, jnp.zeros((1, Q_TILE), I32)) ^ int_min
    need = (kq - count(lambda kk: kk > thr)).astype(F32)

    ri = lax.broadcasted_iota(I32, (sub, sub), 0)
    ci = lax.broadcasted_iota(I32, (sub, sub), 1)
    tri = jnp.where(ri > ci, 1.0, 0.0).astype(BF16)

    def mask_fn(r0, seen):
        parts = []
        for j in range(rows // sub):
            kk = keys_ref[pl.ds(pl.multiple_of(r0 + j * sub, sub), sub), :]
            tie = jnp.where(kk == thr, 1.0, 0.0)
            before = jnp.dot(tri, tie.astype(BF16), preferred_element_type=F32) + seen
            take = (kk > thr) | ((kk == thr) & (before < need))
            parts.append(jnp.where(take, 0.0, -jnp.inf))
            seen = seen + jnp.sum(tie, axis=0, keepdims=True)
        return jnp.concatenate(parts, axis=0), seen

    qs = _stack_heads(dq_ref[...], DSA_HEADS, DSA_LATENT) * jnp.asarray(DSA_LATENT ** -0.5, BF16)
    o_lat = _flash_chunks(qs, krot_ref, vt_ref, 0, nch, rows, mask_fn, jnp.zeros((1, Q_TILE), F32),
                          acc_ref, m_ref, l_ref, DSA_HEADS)
    outs = []
    for h in range(DSA_HEADS):
        oh = o_lat[:, h * Q_TILE:(h + 1) * Q_TILE].astype(BF16)
        outs.append(jnp.dot(wuvt_ref[h], oh, preferred_element_type=F32))
    o_ref[...] = jnp.concatenate(outs, axis=0).T.astype(o_ref.dtype)


def _dsa_attention(c, w_uv, bsz, seq):
    t = bsz * seq
    nq = seq // Q_TILE
    n_sel = min(DSA_TOPK, seq // 4)
    wuvt = jnp.swapaxes(w_uv, 1, 2).astype(BF16)
    qrow = lambda w: pl.BlockSpec((Q_TILE, w), lambda b, n: (b * nq + n, 0))
    seqrow = lambda w: pl.BlockSpec((seq, w), lambda b, n: (b, 0))
    nqh = DSA_HEADS * Q_TILE
    return pl.pallas_call(
        functools.partial(_dsa_body, n_sel=n_sel),
        grid=(bsz, nq),
        in_specs=[qrow(128), qrow(512), qrow(128), seqrow(IDX_DIM), seqrow(DSA_LATENT),
                  pl.BlockSpec((DSA_LATENT, seq), lambda b, n: (0, b)),
                  pl.BlockSpec(wuvt.shape, lambda b, n: (0, 0, 0))],
        out_specs=qrow(512),
        out_shape=jax.ShapeDtypeStruct((t, 512), BF16),
        scratch_shapes=[pltpu.VMEM((seq, Q_TILE), I32), pltpu.VMEM((DSA_LATENT, nqh), F32),
                        pltpu.VMEM((1, nqh), F32), pltpu.VMEM((1, nqh), F32)],
        compiler_params=_cparams("parallel", "arbitrary"),
        name="dsa_attention",
    )(c["iq"], c["dq"], c["misc"], c["ik"], c["krot"], c["vt"], wuvt)


CMP_PER_ROW = CMP_STRIDE * HEAD_DIM


def _nsa_cmp_body(kc_ref, vc_ref, wk_ref, wv_ref, pek_ref, pev_ref, kcmp_ref, vcmpt_ref):
    n = kc_ref.shape[0]

    def compress(x_ref, w_ref, pe_ref):
        xf = x_ref[...].astype(F32)
        lo = jnp.dot((xf + pe_ref[0:1, :]).astype(BF16), w_ref[0], preferred_element_type=F32)
        hi = jnp.dot((xf + pe_ref[1:2, :]).astype(BF16), w_ref[1], preferred_element_type=F32)
        return lo + pltpu.roll(hi, n - 1, 0)

    kcmp_ref[...] = compress(kc_ref, wk_ref, pek_ref).astype(BF16)
    vc = compress(vc_ref, wv_ref, pev_ref)
    vpad = jnp.concatenate([vc, jnp.zeros_like(vc)], axis=1)
    vcmpt_ref[...] = vpad.T[:HEAD_DIM, :].astype(BF16)


def _nsa_compress(c, w_ck, pe_k, w_cv, pe_v, bsz, seq):
    ng = seq // CMP_STRIDE
    kc2 = c["nkc"].reshape(bsz * ng, CMP_PER_ROW)
    vc2 = c["nvc"].reshape(bsz * ng, CMP_PER_ROW)
    wk = w_ck.reshape(2, CMP_PER_ROW, HEAD_DIM).astype(BF16)
    wv = w_cv.reshape(2, CMP_PER_ROW, HEAD_DIM).astype(BF16)
    pek = pe_k.reshape(2, CMP_PER_ROW)
    pev = pe_v.reshape(2, CMP_PER_ROW)
    grp = pl.BlockSpec((ng, CMP_PER_ROW), lambda b: (b, 0))
    wsp = pl.BlockSpec((2, CMP_PER_ROW, HEAD_DIM), lambda b: (0, 0, 0))
    psp = pl.BlockSpec((2, CMP_PER_ROW), lambda b: (0, 0))
    return pl.pallas_call(
        _nsa_cmp_body,
        grid=(bsz,),
        in_specs=[grp, grp, wsp, wsp, psp, psp],
        out_specs=[pl.BlockSpec((ng, HEAD_DIM), lambda b: (b, 0)), pl.BlockSpec((HEAD_DIM, ng), lambda b: (0, b))],
        out_shape=[jax.ShapeDtypeStruct((bsz * ng, HEAD_DIM), BF16), jax.ShapeDtypeStruct((HEAD_DIM, bsz * ng), BF16)],
        compiler_params=_cparams("parallel"),
        name="nsa_compress",
    )(kc2, vc2, wk, wv, pek, pev)


def _nsa_body(nq_ref, misc_ref, kcmp_ref, vcmpt_ref, ovl_ref, exp_ref, ks_ref, vst_ref, kw_ref, vwt_ref,
              o_ref, sel_ref, acc_ref, m_ref, l_ref, *, n_sel, n_cmp):
    rows = KEY_CHUNK
    n = pl.program_id(1)
    q0 = n * Q_TILE
    tq = q0 + lax.broadcasted_iota(I32, (1, Q_TILE), 1)
    qs = _stack_heads(nq_ref[...], NSA_HEADS, HEAD_DIM) * jnp.asarray(HEAD_DIM ** -0.5, BF16)
    gate_t = jax.nn.sigmoid(misc_ref[...]).T

    def gate_row(branch):
        return jnp.concatenate([gate_t[MISC_NG + 3 * h + branch:MISC_NG + 3 * h + branch + 1, :]
                                for h in range(NSA_HEADS)], axis=1)

    ng = kcmp_ref.shape[0]
    jrow = lax.broadcasted_iota(I32, (ng, Q_TILE), 0)
    ok_c = (jrow * CMP_STRIDE + (CMP_BLOCK - 1) <= tq) & (jrow < n_cmp)
    bias_c = jnp.where(ok_c, 0.0, -jnp.inf)
    s = lax.dot_general(kcmp_ref[...], qs, _NT, preferred_element_type=F32)
    s = s + jnp.concatenate([bias_c] * NSA_HEADS, axis=1)
    mx = jnp.maximum(jnp.max(s, axis=0, keepdims=True), NEG_BIG)
    e = jnp.exp(s - mx)
    pc = (e / jnp.maximum(jnp.sum(e, axis=0, keepdims=True), 1e-30)).astype(BF16)
    out = gate_row(0) * jnp.dot(vcmpt_ref[...], pc, preferred_element_type=F32)

    imp8 = jnp.dot(ovl_ref[...], pc, preferred_element_type=F32)
    imp = imp8[:, 0:Q_TILE]
    for h in range(1, NSA_HEADS):
        imp = imp + imp8[:, h * Q_TILE:(h + 1) * Q_TILE]
    n_blk = imp.shape[0]
    blk = lax.broadcasted_iota(I32, (n_blk, Q_TILE), 0)
    cur = tq // SEL_BLOCK
    forced = (blk == 0) | (blk == cur) | (blk == cur - 1)
    imp = jnp.where(blk <= cur, jnp.where(forced, FORCED_SCORE, imp), -jnp.inf)
    chosen = jnp.zeros((n_blk, Q_TILE), F32)
    for _ in range(n_sel):
        top = jnp.max(imp, axis=0, keepdims=True)
        first = jnp.min(jnp.where(imp == top, blk, n_blk), axis=0, keepdims=True)
        pick = (blk == first) & (top > -jnp.inf)
        chosen = jnp.where(pick, 1.0, chosen)
        imp = jnp.where(pick, -jnp.inf, imp)
    sel_ref[...] = chosen.astype(BF16)

    def sel_mask(r0, carry):
        hit = jnp.dot(exp_ref[pl.ds(r0, rows), :], sel_ref[...], preferred_element_type=F32)
        kpos = r0 + lax.broadcasted_iota(I32, (rows, Q_TILE), 0)
        return jnp.where((hit > 0.5) & (kpos <= tq), 0.0, -jnp.inf), carry

    nch = (q0 + Q_TILE + rows - 1) // rows
    o_s = _flash_chunks(qs, ks_ref, vst_ref, 0, nch, rows, sel_mask, 0, acc_ref, m_ref, l_ref, NSA_HEADS)
    out = out + gate_row(1) * o_s

    band = WINDOW + Q_TILE
    w0 = pl.multiple_of(jnp.maximum(q0 - WINDOW, 0), Q_TILE)
    kpos = w0 + lax.broadcasted_iota(I32, (band, Q_TILE), 0)
    bias_w = jnp.where((kpos <= tq) & (kpos > tq - WINDOW), 0.0, -jnp.inf)
    sw = lax.dot_general(kw_ref[pl.ds(w0, band), :], qs, _NT, preferred_element_type=F32)
    sw = sw + jnp.concatenate([bias_w] * NSA_HEADS, axis=1)
    ew = jnp.exp(sw - jnp.max(sw, axis=0, keepdims=True))
    o_w = jnp.dot(vwt_ref[:, pl.ds(w0, band)], ew.astype(BF16), preferred_element_type=F32)
    out = out + gate_row(2) * (o_w / jnp.sum(ew, axis=0, keepdims=True))
    o_ref[...] = _unstack_heads_t(out, NSA_HEADS).astype(o_ref.dtype)


def _nsa_attention(c, kcmp, vcmpt, bsz, seq):
    t = bsz * seq
    nq = seq // Q_TILE
    ng = seq // CMP_STRIDE
    n_cmp = (seq - CMP_BLOCK) // CMP_STRIDE + 1
    n_blk = seq // SEL_BLOCK
    n_sel = min(N_SEL_BLOCKS, n_blk)
    j = np.arange(ng)[None, :] * CMP_STRIDE
    b0 = np.arange(n_blk)[:, None] * SEL_BLOCK
    ovl = ((j <= b0 + SEL_BLOCK - 1) & (j + CMP_BLOCK - 1 >= b0) & (np.arange(ng)[None, :] < n_cmp))
    ovl = jnp.asarray(ovl.astype(np.float32), BF16)
    expand = jnp.asarray((np.arange(seq)[:, None] // SEL_BLOCK == np.arange(n_blk)[None, :]).astype(np.float32), BF16)
    qrow = lambda w: pl.BlockSpec((Q_TILE, w), lambda b, n: (b * nq + n, 0))
    seqrow = lambda w: pl.BlockSpec((seq, w), lambda b, n: (b, 0))
    vrow = lambda r: pl.BlockSpec((HEAD_DIM, seq), lambda b, n: (r, b))
    const = lambda a: pl.BlockSpec(a.shape, lambda b, n: (0, 0))
    nqh = NSA_HEADS * Q_TILE
    return pl.pallas_call(
        functools.partial(_nsa_body, n_sel=n_sel, n_cmp=n_cmp),
        grid=(bsz, nq),
        in_specs=[qrow(512), qrow(128),
                  pl.BlockSpec((ng, HEAD_DIM), lambda b, n: (b, 0)), pl.BlockSpec((HEAD_DIM, ng), lambda b, n: (0, b)),
                  const(ovl), const(expand), seqrow(HEAD_DIM), vrow(1), seqrow(HEAD_DIM), vrow(2)],
        out_specs=qrow(512),
        out_shape=jax.ShapeDtypeStruct((t, 512), BF16),
        scratch_shapes=[pltpu.VMEM((n_blk, Q_TILE), BF16), pltpu.VMEM((HEAD_DIM, nqh), F32),
                        pltpu.VMEM((1, nqh), F32), pltpu.VMEM((1, nqh), F32)],
        compiler_params=_cparams("parallel", "arbitrary"),
        name="nsa_attention",
    )(c["nq"], c["misc"], kcmp, vcmpt, ovl, expand, c["nks"], c["vt"], c["nkw"], c["vt"])


GDN_TILE = 512


def _split_bf16(a):
    hi = a.astype(BF16)
    return hi, (a - hi.astype(F32)).astype(BF16)


def _dot3(a, b, dims=None):
    ah, al = _split_bf16(a)
    bh, bl = _split_bf16(b)
    if dims is None:
        f = lambda u, v: jnp.dot(u, v, preferred_element_type=F32)
    else:
        f = lambda u, v: lax.dot_general(u, v, dims, preferred_element_type=F32)
    return f(ah, bh) + (f(ah, bl) + f(al, bh))


def _bdot(a, b, dims=None):
    a, b = a.astype(BF16), b.astype(BF16)
    if dims is None:
        return jnp.dot(a, b, preferred_element_type=F32)
    return lax.dot_general(a, b, dims, preferred_element_type=F32)


def _gdn_body(qkv_ref, misc_ref, z_ref, convw_ref, alog_ref, dtb_ref, normg_ref, o_ref,
              state_ref, tail_ref, q_s, k_s, v_s, gc_s, beta_s):
    tt = qkv_ref.shape[0]
    ch = GDN_CHUNK
    hw = GDN_HEADS * GDN_DIM

    @pl.when(pl.program_id(1) == 0)
    def _():
        state_ref[...] = jnp.zeros(state_ref.shape, F32)
        tail_ref[...] = jnp.zeros(tail_ref.shape, F32)

    x = qkv_ref[...]
    w = convw_ref[...]
    tail = tail_ref[...]
    x8 = x[0:8, :]
    row8 = lax.broadcasted_iota(I32, (8, 1), 0)
    acc = x * w[CONV_WIDTH - 1:CONV_WIDTH, :]
    acc8 = x8 * w[CONV_WIDTH - 1:CONV_WIDTH, :]
    for k in range(1, CONV_WIDTH):
        wk = w[CONV_WIDTH - 1 - k:CONV_WIDTH - k, :]
        acc = acc + pltpu.roll(x, k, 0) * wk
        acc8 = acc8 + jnp.where(row8 < k, pltpu.roll(tail, k, 0), pltpu.roll(x8, k, 0)) * wk
    tail_ref[...] = x[tt - 8:tt, :]
    rowt = lax.broadcasted_iota(I32, (tt, 1), 0)
    acc = jnp.where(rowt < 8, jnp.concatenate([acc8, acc[8:, :]], axis=0), acc)
    xc = acc * jax.nn.sigmoid(acc)

    def l2n(a):
        return a * lax.rsqrt(jnp.sum(a * a, axis=-1, keepdims=True) + RMS_EPS)

    for h in range(GDN_HEADS):
        q_s[h] = l2n(xc[:, h * GDN_DIM:(h + 1) * GDN_DIM]) * (GDN_DIM ** -0.5)
        k_s[h] = l2n(xc[:, hw + h * GDN_DIM:hw + (h + 1) * GDN_DIM])
        v_s[h] = xc[:, 2 * hw + h * GDN_DIM:2 * hw + (h + 1) * GDN_DIM]

    misc = misc_ref[...]
    sp = misc + dtb_ref[...]
    softplus = jnp.maximum(sp, 0.0) + jnp.log(1.0 + jnp.exp(-jnp.abs(sp)))
    g = -jnp.exp(alog_ref[...]) * softplus
    beta_s[...] = jax.nn.sigmoid(misc)
    ri = lax.broadcasted_iota(I32, (tt, tt), 0)
    ci = lax.broadcasted_iota(I32, (tt, tt), 1)
    blk_tril = jnp.where((ri >= ci) & (ri // ch == ci // ch), 1.0, 0.0)
    gc_s[...] = jnp.dot(blk_tril, g, preferred_element_type=F32, precision=lax.Precision.HIGHEST)

    r64 = lax.broadcasted_iota(I32, (ch, ch), 0)
    c64 = lax.broadcasted_iota(I32, (ch, ch), 1)
    tri = r64 >= c64
    strict = r64 > c64
    norm_g = normg_ref[...]

    def chunk_step(c, _):
        r0 = pl.multiple_of(c * ch, ch)
        gcs = gc_s[pl.ds(r0, ch), :]
        gct = jnp.concatenate([gcs, jnp.zeros_like(gcs)], axis=0).T
        bts = beta_s[pl.ds(r0, ch), :]
        heads = range(GDN_HEADS)
        q = [q_s[h, pl.ds(r0, ch), :] for h in heads]
        k = [k_s[h, pl.ds(r0, ch), :] for h in heads]
        gcol = [gcs[:, MISC_GA + h:MISC_GA + h + 1] for h in heads]
        decay = [jnp.exp(jnp.where(tri, gcol[h] - gct[MISC_GA + h:MISC_GA + h + 1, 0:ch], -jnp.inf)) for h in heads]
        beta = [bts[:, MISC_GB + h:MISC_GB + h + 1] for h in heads]
        kb = [k[h] * beta[h] for h in heads]
        eg = [jnp.exp(gcol[h]) for h in heads]
        nmat = [-jnp.where(strict, _bdot(kb[h], k[h], _NT) * decay[h], 0.0) for h in heads]
        y = [jnp.concatenate([v_s[h, pl.ds(r0, ch), :] * beta[h], kb[h] * eg[h]], axis=1) for h in heads]
        for level in range(6):
            y = [y[h] + _dot3(nmat[h], y[h]) for h in heads]
            if level < 5:
                nmat = [_dot3(nmat[h], nmat[h]) for h in heads]
        attn = [jnp.where(tri, _bdot(q[h], k[h], _NT) * decay[h], 0.0) for h in heads]
        glast = [gcol[h][ch - 1:ch, :] for h in heads]
        state = [state_ref[h] for h in heads]
        v_new = [y[h][:, :GDN_DIM] - _bdot(y[h][:, GDN_DIM:], state[h]) for h in heads]
        o = [_bdot(q[h] * eg[h], state[h]) + _bdot(attn[h], v_new[h]) for h in heads]
        for h in heads:
            k_dec = k[h] * jnp.exp(glast[h] - gcol[h])
            state_ref[h] = state[h] * jnp.exp(glast[h]) + _bdot(k_dec, v_new[h], (((0,), (0,)), ((), ())))
        for h in heads:
            on = o[h] * lax.rsqrt(jnp.mean(o[h] * o[h], axis=-1, keepdims=True) + RMS_EPS) * norm_g
            zz = z_ref[pl.ds(r0, ch), h * GDN_DIM:(h + 1) * GDN_DIM]
            o_ref[pl.ds(r0, ch), h * GDN_DIM:(h + 1) * GDN_DIM] = (on * (zz * jax.nn.sigmoid(zz))).astype(o_ref.dtype)
        return 0

    lax.fori_loop(0, tt // ch, chunk_step, 0)


def _gdn_mixer(c, conv_w, a_log, dt_bias, norm_g, bsz, seq):
    t = bsz * seq
    tt = min(GDN_TILE, seq)
    nt = seq // tt
    lane_row = lambda vals, off: jnp.zeros((1, LANES), F32).at[0, off:off + GDN_HEADS].set(vals.astype(F32))
    alog = lane_row(a_log, MISC_GA)
    dtb = lane_row(dt_bias, MISC_GA)
    row = lambda w: pl.BlockSpec((tt, w), lambda b, j: (b * nt + j, 0))
    const = lambda a: pl.BlockSpec(a.shape, lambda b, j: (0, 0))
    ng = norm_g.reshape(1, GDN_DIM).astype(F32)
    hs = pltpu.VMEM((GDN_HEADS, tt, GDN_DIM), F32)
    return pl.pallas_call(
        _gdn_body,
        grid=(bsz, nt),
        in_specs=[row(GDN_QKV), row(LANES), row(GDN_HEADS * GDN_DIM), const(conv_w), const(alog), const(dtb), const(ng)],
        out_specs=row(GDN_HEADS * GDN_DIM),
        out_shape=jax.ShapeDtypeStruct((t, GDN_HEADS * GDN_DIM), BF16),
        scratch_shapes=[pltpu.VMEM((GDN_HEADS, GDN_DIM, GDN_DIM), F32), pltpu.VMEM((8, GDN_QKV), F32),
                        hs, hs, hs, pltpu.VMEM((tt, LANES), F32), pltpu.VMEM((tt, LANES), F32)],
        compiler_params=_cparams("parallel", "arbitrary"),
        name="gdn_mixer",
    )(c["gqkv"], c["misc"], c["gz"], conv_w.astype(F32), alog, dtb, ng)


ROW_TILE = 256


def _layer_norm(v, g, b):
    mu = jnp.mean(v, axis=-1, keepdims=True)
    vc = v - mu
    var = jnp.mean(vc * vc, axis=-1, keepdims=True)
    return vc * lax.rsqrt(var + LN_EPS) * g + b


def _merge_body(ya_ref, yb_ref, yc_ref, mg_ref, x_ref, wb_ref, wo_ref, g_ref, b_ref, wrt_ref,
                x1_ref, x1b_ref, xpk_ref, sct_ref, *, alpha):
    d = x_ref.shape[1]
    merged = mg_ref[:, 0:d] * jnp.dot(ya_ref[...], wb_ref[0], preferred_element_type=F32)
    merged = merged + mg_ref[:, d:2 * d] * jnp.dot(yb_ref[...], wb_ref[1], preferred_element_type=F32)
    merged = merged + mg_ref[:, 2 * d:3 * d] * jnp.dot(yc_ref[...], wb_ref[2], preferred_element_type=F32)
    y = alpha * x_ref[...] + jnp.dot(merged.astype(BF16), wo_ref[...], preferred_element_type=F32)
    x1 = _layer_norm(y, g_ref[...], b_ref[...])
    x1_ref[...] = x1
    x1b = x1.astype(BF16)
    x1b_ref[...] = x1b
    xpk_ref[...] = _pack_bf16_pairs(x1b)
    logits_t = lax.dot_general(wrt_ref[...], x1b, _NT, preferred_element_type=F32)
    sct_ref[...] = jax.nn.sigmoid(logits_t)


def _merge_out(ya, yb, yc, mg, x2d, w_branch, w_out, ln_g, ln_b, w_router, alpha):
    t, d = x2d.shape
    tm = ROW_TILE
    row = lambda w: pl.BlockSpec((tm, w), lambda i: (i, 0))
    wb = w_branch.astype(BF16)
    wo = w_out.astype(BF16)
    wrt = w_router.T.astype(BF16)
    g = ln_g.reshape(1, d)
    b = ln_b.reshape(1, d)
    return pl.pallas_call(
        functools.partial(_merge_body, alpha=alpha),
        grid=(t // tm,),
        in_specs=[row(BRANCH_WIDTH)] * 3 + [row(3 * d), row(d),
                  pl.BlockSpec(wb.shape, lambda i: (0, 0, 0)), pl.BlockSpec(wo.shape, lambda i: (0, 0)),
                  pl.BlockSpec((1, d), lambda i: (0, 0)), pl.BlockSpec((1, d), lambda i: (0, 0)),
                  pl.BlockSpec(wrt.shape, lambda i: (0, 0))],
        out_specs=[row(d), row(d), row(d // 2), pl.BlockSpec((N_EXPERTS, tm), lambda i: (0, i))],
        out_shape=[jax.ShapeDtypeStruct((t, d), F32), jax.ShapeDtypeStruct((t, d), BF16),
                   jax.ShapeDtypeStruct((t, d // 2), jnp.uint32), jax.ShapeDtypeStruct((N_EXPERTS, t), F32)],
        compiler_params=_cparams("parallel"),
        name="merge_out_ln",
    )(ya, yb, yc, mg, x2d, wb, wo, g, b, wrt)


ROUTE_TILE = 512


def _route_body(sct_ref, bias_ref, eidx_ref, gate_ref, rank_ref, cnt_ref):
    sc = sct_ref[...]
    n = sc.shape[1]
    per = N_EXPERTS // N_GROUPS
    biased = sc + bias_ref[...]
    b3 = biased.reshape(N_GROUPS, per, n)
    sub = lax.broadcasted_iota(I32, (N_GROUPS, per, n), 1)
    m1 = jnp.max(b3, axis=1, keepdims=True)
    first = jnp.min(jnp.where(b3 == m1, sub, per), axis=1, keepdims=True)
    m2 = jnp.max(jnp.where(sub == first, -jnp.inf, b3), axis=1, keepdims=True)
    gs = (m1 + m2).reshape(N_GROUPS, n)
    gi = lax.broadcasted_iota(I32, (N_GROUPS, n), 0)
    gmask = jnp.zeros((N_GROUPS, n), F32)
    for _ in range(TOPK_GROUPS):
        top = jnp.max(gs, axis=0, keepdims=True)
        pick = gi == jnp.min(jnp.where(gs == top, gi, N_GROUPS), axis=0, keepdims=True)
        gmask = jnp.where(pick, 1.0, gmask)
        gs = jnp.where(pick, -jnp.inf, gs)
    emask = jnp.broadcast_to(gmask.reshape(N_GROUPS, 1, n), (N_GROUPS, per, n)).reshape(N_EXPERTS, n)
    cand = jnp.where(emask > 0.5, biased, -jnp.inf)
    ei = lax.broadcasted_iota(I32, (N_EXPERTS, n), 0)
    ids, gates, picks = [], [], []
    for _ in range(TOP_K):
        top = jnp.max(cand, axis=0, keepdims=True)
        idx = jnp.min(jnp.where(cand == top, ei, N_EXPERTS), axis=0, keepdims=True)
        pick = ei == idx
        ids.append(idx)
        picks.append(pick)
        gates.append(jnp.sum(jnp.where(pick, sc, 0.0), axis=0, keepdims=True))
        cand = jnp.where(pick, -jnp.inf, cand)
    gate = jnp.concatenate(gates, axis=0)
    gate = gate / jnp.sum(gate, axis=0, keepdims=True) * ROUTED_SCALE
    eidx_ref[...] = jnp.concatenate(ids, axis=0)
    gate_ref[...] = gate

    @pl.when(pl.program_id(0) == 0)
    def _():
        cnt_ref[...] = jnp.zeros(cnt_ref.shape, F32)

    sel = jnp.where(cand == -jnp.inf, 1.0, 0.0) * jnp.where(emask > 0.5, 1.0, 0.0)
    ri = lax.broadcasted_iota(I32, (n, n), 0)
    ci = lax.broadcasted_iota(I32, (n, n), 1)
    before = jnp.dot(sel.astype(BF16), jnp.where(ri < ci, 1.0, 0.0).astype(BF16), preferred_element_type=F32)
    pos = before + cnt_ref[...]
    rank_ref[...] = jnp.concatenate(
        [jnp.sum(jnp.where(pk, pos, 0.0), axis=0, keepdims=True) for pk in picks], axis=0).astype(I32)
    cnt_ref[...] = cnt_ref[...] + jnp.sum(sel, axis=1, keepdims=True)


def _route(sct, router_bias):
    e, t = sct.shape
    tn = ROUTE_TILE
    col = lambda r: pl.BlockSpec((r, tn), lambda i: (0, i))
    return pl.pallas_call(
        _route_body,
        grid=(t // tn,),
        in_specs=[col(e), pl.BlockSpec((e, 1), lambda i: (0, 0))],
        out_specs=[col(TOP_K), col(TOP_K), col(TOP_K), pl.BlockSpec((e, 1), lambda i: (0, 0))],
        out_shape=[jax.ShapeDtypeStruct((TOP_K, t), I32), jax.ShapeDtypeStruct((TOP_K, t), F32),
                   jax.ShapeDtypeStruct((TOP_K, t), I32), jax.ShapeDtypeStruct((e, 1), F32)],
        compiler_params=_cparams("arbitrary"),
        name="route_topk",
    )(sct, router_bias.reshape(e, 1).astype(F32))


U32 = jnp.uint32
DISPATCH_TILE = 512


def _pack_bf16_pairs(xb):
    n = xb.shape[1] // 2
    bits = pltpu.bitcast(xb.astype(F32), U32)
    return (bits[:, :n] >> 16) | (bits[:, n:] & jnp.uint32(0xFFFF0000))


def _unpack_bf16_pairs(packed):
    lo = pltpu.bitcast(packed << 16, F32).astype(BF16)
    hi = pltpu.bitcast(packed & jnp.uint32(0xFFFF0000), F32).astype(BF16)
    return lo, hi


def _dispatch_body(dest_hbm, x_ref, init_hbm, xs_hbm, idx_smem, isem, rsem):
    del init_hbm
    i = pl.program_id(0)
    tn = DISPATCH_TILE
    idx_cp = pltpu.make_async_copy(dest_hbm.at[i], idx_smem, isem)
    idx_cp.start()
    idx_cp.wait()

    def issue(j8, _):
        base = pl.multiple_of(j8 * 8, 8)
        for jj in range(8):
            src = x_ref.at[pl.ds(base + jj, 1)]
            for k in range(TOP_K):
                pltpu.make_async_copy(src, xs_hbm.at[pl.ds(idx_smem[k * tn + base + jj], 1)],
                                      rsem).start(priority=k % 2)
        return 0

    lax.fori_loop(0, tn // 8, issue, 0)
    for k in range(TOP_K):
        pltpu.make_async_copy(x_ref, xs_hbm.at[pl.ds(0, tn)], rsem).wait()


def _dispatch(xpk, dest_tiles, n_pad):
    t, w = xpk.shape
    tn = DISPATCH_TILE
    anyspec = pl.BlockSpec(memory_space=pl.ANY)
    return pl.pallas_call(
        _dispatch_body,
        grid=(t // tn,),
        in_specs=[anyspec, pl.BlockSpec((tn, w), lambda i: (i, 0)), anyspec],
        out_specs=anyspec,
        out_shape=jax.ShapeDtypeStruct((n_pad, w), xpk.dtype),
        scratch_shapes=[pltpu.SMEM((TOP_K * tn,), I32), pltpu.SemaphoreType.DMA, pltpu.SemaphoreType.DMA],
        input_output_aliases={2: 0},
        compiler_params=_cparams("arbitrary"),
        name="moe_dispatch",
    )(dest_tiles, xpk, jnp.zeros((n_pad, w), xpk.dtype))


def _expert_body(blk_e_ref, nused_ref, xb_ref, wgu_ref, wd_ref, o_ref):
    i = pl.program_id(0)
    half = xb_ref.shape[1]

    @pl.when(i < nused_ref[0])
    def _():
        lo, hi = _unpack_bf16_pairs(xb_ref[...])
        gu = (jnp.dot(lo, wgu_ref[0, :half, :], preferred_element_type=F32)
              + jnp.dot(hi, wgu_ref[0, half:, :], preferred_element_type=F32))
        gg, uu = gu[:, :D_EXPERT], gu[:, D_EXPERT:]
        hmid = (gg * jax.nn.sigmoid(gg) * uu).astype(BF16)
        o_ref[...] = jnp.dot(hmid, wd_ref[0], preferred_element_type=F32)

    @pl.when(i >= nused_ref[0])
    def _():
        o_ref[...] = jnp.zeros(o_ref.shape, o_ref.dtype)


def _expert_ffn(xs, blk_e, n_used, wgu, wd):
    n_pad, half = xs.shape
    d = 2 * half
    nb = n_pad // EXPERT_BLOCK
    grid_spec = pltpu.PrefetchScalarGridSpec(
        num_scalar_prefetch=2,
        grid=(nb,),
        in_specs=[pl.BlockSpec((EXPERT_BLOCK, half), lambda i, be, nu: (i, 0)),
                  pl.BlockSpec((1, d, 2 * D_EXPERT), lambda i, be, nu: (be[i], 0, 0)),
                  pl.BlockSpec((1, D_EXPERT, d), lambda i, be, nu: (be[i], 0, 0))],
        out_specs=pl.BlockSpec((EXPERT_BLOCK, d), lambda i, be, nu: (i, 0)),
    )
    return pl.pallas_call(
        _expert_body,
        grid_spec=grid_spec,
        out_shape=jax.ShapeDtypeStruct((n_pad, d), F32),
        compiler_params=_cparams("arbitrary"),
        name="expert_ffn",
    )(blk_e, n_used, xs, wgu, wd)


def _dest_body(eidx_ref, rank_ref, start_ref, dest_ref):
    n = eidx_ref.shape[1]
    ei = lax.broadcasted_iota(I32, (N_EXPERTS, n), 0)
    start = start_ref[...]
    rows = [jnp.sum(jnp.where(ei == eidx_ref[k:k + 1, :], start, 0.0), axis=0, keepdims=True)
            for k in range(TOP_K)]
    dest_ref[...] = rank_ref[...] + jnp.concatenate(rows, axis=0).astype(I32)


def _moe_plan(eidx_t, rank_t, counts, n_tok):
    n_rows = n_tok * TOP_K
    n_blocks = -(-n_rows // EXPERT_BLOCK) + N_EXPERTS
    cnt = counts.reshape(-1).astype(I32)
    padded = (cnt + EXPERT_BLOCK - 1) // EXPERT_BLOCK * EXPERT_BLOCK
    pad_end = jnp.cumsum(padded)
    pad_start = pad_end - padded
    tn = ROUTE_TILE
    col = pl.BlockSpec((TOP_K, tn), lambda i: (0, i))
    dest = pl.pallas_call(
        _dest_body,
        grid=(n_tok // tn,),
        in_specs=[col, col, pl.BlockSpec((N_EXPERTS, 1), lambda i: (0, 0))],
        out_specs=col,
        out_shape=jax.ShapeDtypeStruct((TOP_K, n_tok), I32),
        compiler_params=_cparams("parallel"),
        name="route_dest",
    )(eidx_t, rank_t, pad_start.astype(F32).reshape(N_EXPERTS, 1))
    blk_first = jnp.arange(n_blocks, dtype=I32) * EXPERT_BLOCK
    blk_e = jnp.minimum(jnp.sum((pad_end[None, :] <= blk_first[:, None]).astype(I32), axis=1), N_EXPERTS - 1)
    n_used = (pad_end[-1] // EXPERT_BLOCK).astype(I32).reshape(1)
    return dest, blk_e, n_used, n_blocks * EXPERT_BLOCK


def _tile_indices(dest, tn):
    k, t = dest.shape
    return dest.reshape(k, t // tn, tn).transpose(1, 0, 2).reshape(t // tn, k * tn)


def _ffn_out_body(dest_hbm, ye_hbm, gate_ref, x1_ref, x1b_ref, p_ref, wsgu_ref, wsd_ref, wpg_ref, wpp_ref,
                  g_ref, b_ref, o_ref, idx0_smem, idx1_smem, rows_ref, isem, rsem, *, alpha):
    idx_smem = (idx0_smem, idx1_smem)
    i = pl.program_id(0)
    n_tiles = pl.num_programs(0)
    tm = x1_ref.shape[0]

    def gather_tile(tile, s):
        idx = idx_smem[s]
        idx_cp = pltpu.make_async_copy(dest_hbm.at[tile], idx, isem)
        idx_cp.start()
        idx_cp.wait()

        def issue(j8, _):
            base = pl.multiple_of(j8 * 8, 8)
            for jj in range(8):
                for k in range(TOP_K):
                    pltpu.make_async_copy(ye_hbm.at[pl.ds(idx[k * tm + base + jj], 1)],
                                          rows_ref.at[s, k, pl.ds(base + jj, 1)],
                                          rsem.at[s]).start(priority=k % 2)
            return 0

        lax.fori_loop(0, tm // 8, issue, 0)

    def run(slot):
        @pl.when(i + 1 < n_tiles)
        def _():
            gather_tile(i + 1, 1 - slot)

        xb = x1b_ref[...]
        su = jnp.dot(xb, wsgu_ref[...], preferred_element_type=F32)
        sg, uu = su[:, :D_SHARED], su[:, D_SHARED:]
        shared = jnp.dot((sg * jax.nn.sigmoid(sg) * uu).astype(BF16), wsd_ref[...], preferred_element_type=F32)
        pg = jax.nn.sigmoid(jnp.dot(xb, wpg_ref[...], preferred_element_type=F32))
        ple = pg * jnp.dot(p_ref[...].astype(BF16), wpp_ref[...], preferred_element_type=F32)

        g8 = gate_ref[...]
        gate_cols = jnp.concatenate([g8, jnp.zeros((LANES - TOP_K, tm), F32)], axis=0).T
        for k in range(TOP_K):
            pltpu.make_async_copy(ye_hbm.at[pl.ds(0, tm)], rows_ref.at[slot, k], rsem.at[slot]).wait()
        routed = gate_cols[:, 0:1] * rows_ref[slot, 0]
        for k in range(1, TOP_K):
            routed = routed + gate_cols[:, k:k + 1] * rows_ref[slot, k]
        y = alpha * x1_ref[...] + (routed + shared) + ple
        o_ref[...] = _layer_norm(y, g_ref[...], b_ref[...])

    @pl.when(i == 0)
    def _():
        gather_tile(0, 0)

    @pl.when(i % 2 == 0)
    def _():
        run(0)

    @pl.when(i % 2 == 1)
    def _():
        run(1)


def _ffn_out(dest_tiles, ye, gate_t, x1, x1b, p2d, w_sgu, w_sd, w_pg, w_pp, ln_g, ln_b, alpha):
    t, d = x1.shape
    tm = ROW_TILE
    row = lambda w: pl.BlockSpec((tm, w), lambda i: (i, 0))
    const = lambda a: pl.BlockSpec(a.shape, lambda i: (0, 0))
    anyspec = pl.BlockSpec(memory_space=pl.ANY)
    ws = [w.astype(BF16) for w in (w_sgu, w_sd, w_pg, w_pp)]
    g = ln_g.reshape(1, d)
    b = ln_b.reshape(1, d)
    return pl.pallas_call(
        functools.partial(_ffn_out_body, alpha=alpha),
        grid=(t // tm,),
        in_specs=[anyspec, anyspec, pl.BlockSpec((TOP_K, tm), lambda i: (0, i)), row(d), row(d), row(PLE_DIM)]
                 + [const(w) for w in ws] + [const(g), const(b)],
        out_specs=row(d),
        out_shape=jax.ShapeDtypeStruct((t, d), F32),
        scratch_shapes=[pltpu.SMEM((TOP_K * tm,), I32), pltpu.SMEM((TOP_K * tm,), I32),
                        pltpu.VMEM((2, TOP_K, tm, d), F32),
                        pltpu.SemaphoreType.DMA, pltpu.SemaphoreType.DMA((2,))],
        compiler_params=_cparams("arbitrary"),
        name="ffn_out_ln",
    )(dest_tiles, ye, gate_t, x1, x1b, p2d, *ws, g, b)


def kernel(x, p, w_in, w_dsa_uv, w_cmp_k, pe_cmp_k, w_cmp_v, pe_cmp_v, conv_w, a_log, dt_bias, gdn_norm, w_branch, w_out, ln1_g, ln1_b, w_router, router_bias, w_gate_up, w_down, w_sh_gate_up, w_sh_down, w_ple_proj, w_ple_gate, ln2_g, ln2_b):
    bsz, seq, d = x.shape
    depth = w_in.shape[0]
    alpha = (2 * depth) ** 0.25
    t = bsz * seq
    tabs64, _ = _rope_lane_tables(seq, HEAD_DIM)
    tabs32, _ = _rope_lane_tables(seq, IDX_DIM)
    x2d = x.reshape(t, d)
    for i in range(depth):
        c = _in_projection(x2d, _split_w_in(w_in[i]), tabs64, tabs32, seq)
        ya = _dsa_attention(c, w_dsa_uv[i], bsz, seq)
        kcmp, vcmpt = _nsa_compress(c, w_cmp_k[i], pe_cmp_k[i], w_cmp_v[i], pe_cmp_v[i], bsz, seq)
        yb = _nsa_attention(c, kcmp, vcmpt, bsz, seq)
        yc = _gdn_mixer(c, conv_w[i], a_log[i], dt_bias[i], gdn_norm[i], bsz, seq)
        x1, x1b, xpk, sct = _merge_out(ya, yb, yc, c["mg"], x2d, w_branch[i], w_out[i], ln1_g[i], ln1_b[i],
                                       w_router[i], alpha)
        eidx_t, gate_t, rank_t, counts = _route(sct, router_bias[i])
        dest, blk_e, n_used, n_pad = _moe_plan(eidx_t, rank_t, counts, t)
        xs = _dispatch(xpk, _tile_indices(dest, DISPATCH_TILE), n_pad)
        ye = _expert_ffn(xs, blk_e, n_used, w_gate_up[i].astype(BF16), w_down[i].astype(BF16))
        x2d = _ffn_out(_tile_indices(dest, ROW_TILE), ye, gate_t, x1, x1b, p[i].reshape(t, PLE_DIM),
                       w_sh_gate_up[i], w_sh_down[i], w_ple_gate[i], w_ple_proj[i], ln2_g[i], ln2_b[i], alpha)
    return x2d.reshape(bsz, seq, d)
```

```python
import functools
import math

import numpy as np
import jax
import jax.numpy as jnp
from jax import lax
from jax.experimental import pallas as pl
from jax.experimental.pallas import tpu as pltpu

F32 = jnp.float32
BF16 = jnp.bfloat16
I32 = jnp.int32

HEAD_DIM = 64
ROPE_THETA = 500000.0
ROPE_FRACTION = 4
DSA_HEADS = 8
DSA_LATENT = 64
IDX_HEADS = 4
IDX_DIM = 32
DSA_TOPK = 256
NSA_HEADS = 8
CMP_BLOCK = 32
CMP_STRIDE = 16
SEL_BLOCK = 32
N_SEL_BLOCKS = 8
WINDOW = 256
FORCED_SCORE = 1.0e4
GDN_HEADS = 4
GDN_DIM = 128
CONV_WIDTH = 4
GDN_CHUNK = 64
BRANCH_WIDTH = 512
N_BRANCHES = 3
GDN_QKV = 3 * GDN_HEADS * GDN_DIM
N_EXPERTS = 64
TOP_K = 8
N_GROUPS = 8
TOPK_GROUPS = 4
D_EXPERT = 256
D_SHARED = 256
ROUTED_SCALE = 2.5
EXPERT_BLOCK = 1024
PLE_DIM = 256
LN_EPS = 1e-5
RMS_EPS = 1e-6

IN_SPLITS = (
    ("dsa_q", DSA_HEADS * DSA_LATENT), ("dsa_kv", DSA_LATENT),
    ("idx_q", IDX_HEADS * IDX_DIM), ("idx_k", IDX_DIM), ("idx_w", IDX_HEADS),
    ("nsa_q", NSA_HEADS * HEAD_DIM),
    ("nsa_kc", HEAD_DIM), ("nsa_vc", HEAD_DIM),
    ("nsa_ks", HEAD_DIM), ("nsa_vs", HEAD_DIM),
    ("nsa_kw", HEAD_DIM), ("nsa_vw", HEAD_DIM),
    ("nsa_g", NSA_HEADS * 3),
    ("gdn_qkv", GDN_QKV), ("gdn_a", GDN_HEADS), ("gdn_b", GDN_HEADS), ("gdn_z", GDN_HEADS * GDN_DIM),
    ("merge_g", N_BRANCHES * 1024),
)

LANES = 128
Q_TILE = 128
KEY_CHUNK = 512
TIE_BLOCK = 256
VMEM_LIMIT = 56 * 1024 * 1024
INT_MIN = -2 ** 31
NEG_BIG = -1e30


def _cparams(*sem):
    return pltpu.CompilerParams(dimension_semantics=sem, vmem_limit_bytes=VMEM_LIMIT)


def _rope_lane_tables(seq, head_dim):
    rot = head_dim // ROPE_FRACTION
    half = rot // 2
    inv_freq = ROPE_THETA ** (-(jnp.arange(half, dtype=F32) * 2.0 / rot))
    ang = jnp.arange(seq, dtype=F32)[:, None] * inv_freq[None, :]
    cos, sin = jnp.cos(ang), jnp.sin(ang)
    one = jnp.ones((seq, head_dim - rot), F32)
    c = jnp.concatenate([cos, cos, one], axis=-1)
    sa = jnp.concatenate([-sin, jnp.zeros((seq, head_dim - half), F32)], axis=-1)
    sb = jnp.concatenate([jnp.zeros((seq, half), F32), sin, jnp.zeros((seq, head_dim - rot), F32)], axis=-1)
    rep = LANES // head_dim
    return tuple(jnp.tile(a, (1, rep)) for a in (c, sa, sb)), half


def _rope_apply(y, c, sa, sb, half):
    n = y.shape[-1]
    rep = n // LANES
    if rep > 1:
        c, sa, sb = (jnp.concatenate([a] * rep, axis=-1) for a in (c, sa, sb))
    up = pltpu.roll(y, n - half, 1)
    dn = pltpu.roll(y, half, 1)
    return y * c + up * sa + dn * sb


IN_TILE = 256


def _inproj_body(x_ref, w64_ref, w32_ref, wpl_ref, wvt_ref, wmg_ref,
                 c64_ref, sa64_ref, sb64_ref, c32_ref, sa32_ref, sb32_ref,
                 dq_ref, nq_ref, krot_ref, nkc_ref, nks_ref, nkw_ref, iq_ref, ik_ref,
                 nvc_ref, misc_ref, gz_ref, gqkv_ref, vt_ref, mg_ref):
    xb = x_ref[...].astype(BF16)
    c64, sa64, sb64 = c64_ref[...], sa64_ref[...], sb64_ref[...]
    half64 = HEAD_DIM // ROPE_FRACTION // 2
    half32 = IDX_DIM // ROPE_FRACTION // 2

    def proj(w_ref, lo, hi):
        return jnp.dot(xb, w_ref[:, lo:hi], preferred_element_type=F32)

    dq_ref[...] = _rope_apply(proj(w64_ref, 0, 512), c64, sa64, sb64, half64).astype(BF16)
    nq_ref[...] = _rope_apply(proj(w64_ref, 512, 1024), c64, sa64, sb64, half64).astype(BF16)
    k1 = _rope_apply(proj(w64_ref, 1024, 1152), c64, sa64, sb64, half64).astype(BF16)
    krot_ref[...] = k1[:, :64]
    nkc_ref[...] = k1[:, 64:]
    k2 = _rope_apply(proj(w64_ref, 1152, 1280), c64, sa64, sb64, half64).astype(BF16)
    nks_ref[...] = k2[:, :64]
    nkw_ref[...] = k2[:, 64:]
    c32, sa32, sb32 = c32_ref[...], sa32_ref[...], sb32_ref[...]
    iq_ref[...] = _rope_apply(proj(w32_ref, 0, 128), c32, sa32, sb32, half32).astype(BF16)
    ik = _rope_apply(proj(w32_ref, 128, 256), c32, sa32, sb32, half32).astype(BF16)
    ik_ref[...] = ik[:, :IDX_DIM]
    pl0 = proj(wpl_ref, 0, 128)
    nvc_ref[...] = pl0[:, :64].astype(BF16)
    misc_ref[...] = proj(wpl_ref, 128, 256)
    gz_ref[...] = proj(wpl_ref, 256, 768)
    for j in range(3):
        gqkv_ref[:, j * 512:(j + 1) * 512] = proj(wpl_ref, 768 + j * 512, 768 + (j + 1) * 512)
    vt_ref[...] = lax.dot_general(wvt_ref[...], xb, (((1,), (1,)), ((), ())),
                                  preferred_element_type=F32).astype(BF16)
    for j in range(6):
        g = proj(wmg_ref, j * 512, (j + 1) * 512)
        mg_ref[:, j * 512:(j + 1) * 512] = jax.nn.sigmoid(g)


def _split_w_in(w):
    cols = {}
    off = 0
    for name, size in IN_SPLITS:
        cols[name] = w[:, off:off + size]
        off += size
    d = w.shape[0]
    z = lambda n: jnp.zeros((d, n), w.dtype)
    w64 = jnp.concatenate([cols["dsa_q"], cols["nsa_q"], cols["dsa_kv"], cols["nsa_kc"],
                           cols["nsa_ks"], cols["nsa_kw"]], axis=1)
    w32 = jnp.concatenate([cols["idx_q"], cols["idx_k"], z(LANES - IDX_DIM)], axis=1)
    misc = jnp.concatenate([cols["idx_w"], cols["gdn_a"], cols["gdn_b"], cols["nsa_g"]], axis=1)
    misc = jnp.concatenate([misc, z(LANES - misc.shape[1])], axis=1)
    wpl = jnp.concatenate([cols["nsa_vc"], z(64), misc, cols["gdn_z"], cols["gdn_qkv"]], axis=1)
    wvt = jnp.concatenate([cols["dsa_kv"], cols["nsa_vs"], cols["nsa_vw"], z(64)], axis=1).T
    return [a.astype(BF16) for a in (w64, w32, wpl, wvt, cols["merge_g"])]


MISC_IW = 0
MISC_GA = IDX_HEADS
MISC_GB = MISC_GA + GDN_HEADS
MISC_NG = MISC_GB + GDN_HEADS


def _in_projection(x2d, wparts, tabs64, tabs32, seq):
    t, d = x2d.shape
    tm = IN_TILE
    nt = seq // tm
    w64, w32, wpl, wvt, wmg = wparts
    row = lambda w: pl.BlockSpec((tm, w), lambda i: (i, 0))
    full = lambda a: pl.BlockSpec(a.shape, lambda i: (0, 0))
    tab = pl.BlockSpec((tm, LANES), lambda i: (i % nt, 0))
    outs = [
        ("dq", 512, BF16), ("nq", 512, BF16), ("krot", 64, BF16), ("nkc", 64, BF16), ("nks", 64, BF16),
        ("nkw", 64, BF16), ("iq", 128, BF16), ("ik", IDX_DIM, BF16), ("nvc", 64, BF16),
        ("misc", 128, F32), ("gz", 512, F32), ("gqkv", GDN_QKV, F32),
    ]
    out_shape = [jax.ShapeDtypeStruct((t, w), dt) for _, w, dt in outs]
    out_specs = [row(w) for _, w, _ in outs]
    out_shape += [jax.ShapeDtypeStruct((256, t), BF16), jax.ShapeDtypeStruct((t, 3 * d), F32)]
    out_specs += [pl.BlockSpec((256, tm), lambda i: (0, i)), row(3 * d)]
    res = pl.pallas_call(
        _inproj_body,
        grid=(t // tm,),
        in_specs=[row(d), full(w64), full(w32), full(wpl), full(wvt), full(wmg)] + [tab] * 6,
        out_specs=out_specs,
        out_shape=out_shape,
        compiler_params=_cparams("parallel"),
        name="in_projection",
    )(x2d, w64, w32, wpl, wvt, wmg, *tabs64, *tabs32)
    names = [n for n, _, _ in outs] + ["vt", "mg"]
    return dict(zip(names, res))


_NT = (((1,), (1,)), ((), ()))


def _stack_heads(q, n_heads, width):
    return jnp.concatenate([q[:, h * width:(h + 1) * width] for h in range(n_heads)], axis=0)


def _unstack_heads_t(o_t, n_heads):
    q = o_t.shape[1] // n_heads
    rows = jnp.concatenate([o_t[:, h * q:(h + 1) * q] for h in range(n_heads)], axis=0)
    return rows.T


def _flash_chunks(qs, k_ref, vt_ref, lo, hi, rows, mask_fn, carry0, acc_ref, m_ref, l_ref, n_heads):
    m_ref[...] = jnp.full(m_ref.shape, NEG_BIG, F32)
    l_ref[...] = jnp.zeros(l_ref.shape, F32)
    acc_ref[...] = jnp.zeros(acc_ref.shape, F32)

    def body(c, carry):
        r0 = pl.multiple_of(c * rows, rows)
        bias, carry = mask_fn(r0, carry)
        kc = k_ref[pl.ds(r0, rows), :]
        s = lax.dot_general(kc, qs, _NT, preferred_element_type=F32)
        s = s + jnp.concatenate([bias] * n_heads, axis=1)
        m_old = m_ref[...]
        m_new = jnp.maximum(m_old, jnp.max(s, axis=0, keepdims=True))
        alpha = jnp.exp(m_old - m_new)
        p = jnp.exp(s - m_new)
        l_ref[...] = alpha * l_ref[...] + jnp.sum(p, axis=0, keepdims=True)
        pv = jnp.dot(vt_ref[:, pl.ds(r0, rows)], p.astype(BF16), preferred_element_type=F32)
        acc_ref[...] = acc_ref[...] * alpha + pv
        m_ref[...] = m_new
        return carry

    lax.fori_loop(lo, hi, body, carry0)
    return acc_ref[...] / jnp.maximum(l_ref[...], 1e-30)


def _dsa_body(iq_ref, dq_ref, misc_ref, ik_ref, krot_ref, vt_ref, wuvt_ref, o_ref,
              keys_ref, acc_ref, m_ref, l_ref, *, n_sel):
    rows = KEY_CHUNK
    sub = TIE_BLOCK
    q0 = pl.program_id(1) * Q_TILE
    nch = (q0 + Q_TILE + rows - 1) // rows
    tq = q0 + lax.broadcasted_iota(I32, (1, Q_TILE), 1)
    misc_t = misc_ref[...].T
    coef = (IDX_HEADS ** -0.5) * (IDX_DIM ** -0.5)
    iq = iq_ref[...]
    int_min = jnp.int32(INT_MIN)
    wrow = [misc_t[MISC_IW + h:MISC_IW + h + 1, :] * coef for h in range(IDX_HEADS)]

    def score_block(r0):
        ikc = ik_ref[pl.ds(r0, sub), :]
        acc = jnp.zeros((sub, Q_TILE), F32)
        for h in range(IDX_HEADS):
            s = lax.dot_general(ikc, iq[:, h * IDX_DIM:(h + 1) * IDX_DIM], _NT, preferred_element_type=F32)
            acc = acc + jnp.maximum(s, 0.0) * wrow[h]
        acc = jnp.where(acc == 0.0, 0.0, acc)
        bits = pltpu.bitcast(acc, I32)
        key = bits ^ ((bits >> 31) & jnp.int32(0x7FFFFFFF))
        kpos = r0 + lax.broadcasted_iota(I32, (sub, Q_TILE), 0)
        keys_ref[pl.ds(r0, sub), :] = jnp.where(kpos <= tq, key, int_min)

    def score_chunk(c, _):
        for j in range(rows // sub):
            score_block(pl.multiple_of(c * rows + j * sub, sub))
        return 0

    lax.fori_loop(0, nch, score_chunk, 0)

    def count(pred):
        def body(c, acc):
            r0 = pl.multiple_of(c * rows, rows)
            hit = jnp.where(pred(keys_ref[pl.ds(r0, rows), :]), 1, 0).astype(I32)
            return acc + jnp.sum(hit.reshape(rows // 8, 8, Q_TILE), axis=0)
        acc = lax.fori_loop(0, nch, body, jnp.zeros((8, Q_TILE), I32))
        return jnp.sum(acc, axis=0, keepdims=True)

    kq = jnp.minimum(tq + 1, n_sel)

    def bit_step(i, tu):
        cand_u = tu | lax.shift_left(jnp.int32(1), 31 - i)
        cand = cand_u ^ int_min
        cnt = count(lambda kk: kk >= cand)
        return jnp.where(cnt >= kq, cand_u, tu)

    thr = lax.fori_loop(0, 32, bit_step, jnp.zeros((1, Q_TILE), I32)) ^ int_min
    need = (kq - count(lambda kk: kk > thr)).astype(F32)

    ri = lax.broadcasted_iota(I32, (sub, sub), 0)
    ci = lax.broadcasted_iota(I32, (sub, sub), 1)
    tri = jnp.where(ri > ci, 1.0, 0.0).astype(BF16)

    def mask_fn(r0, seen):
        parts = []
        for j in range(rows // sub):
            kk = keys_ref[pl.ds(pl.multiple_of(r0 + j * sub, sub), sub), :]
            tie = jnp.where(kk == thr, 1.0, 0.0)
            before = jnp.dot(tri, tie.astype(BF16), preferred_element_type=F32) + seen
            take = (kk > thr) | ((kk == thr) & (before < need))
            parts.append(jnp.where(take, 0.0, -jnp.inf))
            seen = seen + jnp.sum(tie, axis=0, keepdims=True)
        return jnp.concatenate(parts, axis=0), seen

    qs = _stack_heads(dq_ref[...], DSA_HEADS, DSA_LATENT) * jnp.asarray(DSA_LATENT ** -0.5, BF16)
    o_lat = _flash_chunks(qs, krot_ref, vt_ref, 0, nch, rows, mask_fn, jnp.zeros((1, Q_TILE), F32),
                          acc_ref, m_ref, l_ref, DSA_HEADS)
    outs = []
    for h in range(DSA_HEADS):
        oh = o_lat[:, h * Q_TILE:(h + 1) * Q_TILE].astype(BF16)
        outs.append(jnp.dot(wuvt_ref[h], oh, preferred_element_type=F32))
    o_ref[...] = jnp.concatenate(outs, axis=0).T.astype(o_ref.dtype)


def _dsa_attention(c, w_uv, bsz, seq):
    t = bsz * seq
    nq = seq // Q_TILE
    n_sel = min(DSA_TOPK, seq // 4)
    wuvt = jnp.swapaxes(w_uv, 1, 2).astype(BF16)
    qrow = lambda w: pl.BlockSpec((Q_TILE, w), lambda b, n: (b * nq + n, 0))
    seqrow = lambda w: pl.BlockSpec((seq, w), lambda b, n: (b, 0))
    nqh = DSA_HEADS * Q_TILE
    return pl.pallas_call(
        functools.partial(_dsa_body, n_sel=n_sel),
        grid=(bsz, nq),
        in_specs=[qrow(128), qrow(512), qrow(128), seqrow(IDX_DIM), seqrow(DSA_LATENT),
                  pl.BlockSpec((DSA_LATENT, seq), lambda b, n: (0, b)),
                  pl.BlockSpec(wuvt.shape, lambda b, n: (0, 0, 0))],
        out_specs=qrow(512),
        out_shape=jax.ShapeDtypeStruct((t, 512), BF16),
        scratch_shapes=[pltpu.VMEM((seq, Q_TILE), I32), pltpu.VMEM((DSA_LATENT, nqh), F32),
                        pltpu.VMEM((1, nqh), F32), pltpu.VMEM((1, nqh), F32)],
        compiler_params=_cparams("parallel", "arbitrary"),
        name="dsa_attention",
    )(c["iq"], c["dq"], c["misc"], c["ik"], c["krot"], c["vt"], wuvt)


CMP_PER_ROW = CMP_STRIDE * HEAD_DIM


def _nsa_cmp_body(kc_ref, vc_ref, wk_ref, wv_ref, pek_ref, pev_ref, kcmp_ref, vcmpt_ref):
    n = kc_ref.shape[0]

    def compress(x_ref, w_ref, pe_ref):
        xf = x_ref[...].astype(F32)
        lo = jnp.dot((xf + pe_ref[0:1, :]).astype(BF16), w_ref[0], preferred_element_type=F32)
        hi = jnp.dot((xf + pe_ref[1:2, :]).astype(BF16), w_ref[1], preferred_element_type=F32)
        return lo + pltpu.roll(hi, n - 1, 0)

    kcmp_ref[...] = compress(kc_ref, wk_ref, pek_ref).astype(BF16)
    vc = compress(vc_ref, wv_ref, pev_ref)
    vpad = jnp.concatenate([vc, jnp.zeros_like(vc)], axis=1)
    vcmpt_ref[...] = vpad.T[:HEAD_DIM, :].astype(BF16)


def _nsa_compress(c, w_ck, pe_k, w_cv, pe_v, bsz, seq):
    ng = seq // CMP_STRIDE
    kc2 = c["nkc"].reshape(bsz * ng, CMP_PER_ROW)
    vc2 = c["nvc"].reshape(bsz * ng, CMP_PER_ROW)
    wk = w_ck.reshape(2, CMP_PER_ROW, HEAD_DIM).astype(BF16)
    wv = w_cv.reshape(2, CMP_PER_ROW, HEAD_DIM).astype(BF16)
    pek = pe_k.reshape(2, CMP_PER_ROW)
    pev = pe_v.reshape(2, CMP_PER_ROW)
    grp = pl.BlockSpec((ng, CMP_PER_ROW), lambda b: (b, 0))
    wsp = pl.BlockSpec((2, CMP_PER_ROW, HEAD_DIM), lambda b: (0, 0, 0))
    psp = pl.BlockSpec((2, CMP_PER_ROW), lambda b: (0, 0))
    return pl.pallas_call(
        _nsa_cmp_body,
        grid=(bsz,),
        in_specs=[grp, grp, wsp, wsp, psp, psp],
        out_specs=[pl.BlockSpec((ng, HEAD_DIM), lambda b: (b, 0)), pl.BlockSpec((HEAD_DIM, ng), lambda b: (0, b))],
        out_shape=[jax.ShapeDtypeStruct((bsz * ng, HEAD_DIM), BF16), jax.ShapeDtypeStruct((HEAD_DIM, bsz * ng), BF16)],
        compiler_params=_cparams("parallel"),
        name="nsa_compress",
    )(kc2, vc2, wk, wv, pek, pev)


def _nsa_body(nq_ref, misc_ref, kcmp_ref, vcmpt_ref, ovl_ref, exp_ref, ks_ref, vst_ref, kw_ref, vwt_ref,
              o_ref, sel_ref, acc_ref, m_ref, l_ref, *, n_sel, n_cmp):
    rows = KEY_CHUNK
    n = pl.program_id(1)
    q0 = n * Q_TILE
    tq = q0 + lax.broadcasted_iota(I32, (1, Q_TILE), 1)
    qs = _stack_heads(nq_ref[...], NSA_HEADS, HEAD_DIM) * jnp.asarray(HEAD_DIM ** -0.5, BF16)
    gate_t = jax.nn.sigmoid(misc_ref[...]).T

    def gate_row(branch):
        return jnp.concatenate([gate_t[MISC_NG + 3 * h + branch:MISC_NG + 3 * h + branch + 1, :]
                                for h in range(NSA_HEADS)], axis=1)

    ng = kcmp_ref.shape[0]
    jrow = lax.broadcasted_iota(I32, (ng, Q_TILE), 0)
    ok_c = (jrow * CMP_STRIDE + (CMP_BLOCK - 1) <= tq) & (jrow < n_cmp)
    bias_c = jnp.where(ok_c, 0.0, -jnp.inf)
    s = lax.dot_general(kcmp_ref[...], qs, _NT, preferred_element_type=F32)
    s = s + jnp.concatenate([bias_c] * NSA_HEADS, axis=1)
    mx = jnp.maximum(jnp.max(s, axis=0, keepdims=True), NEG_BIG)
    e = jnp.exp(s - mx)
    pc = (e / jnp.maximum(jnp.sum(e, axis=0, keepdims=True), 1e-30)).astype(BF16)
    out = gate_row(0) * jnp.dot(vcmpt_ref[...], pc, preferred_element_type=F32)

    imp8 = jnp.dot(ovl_ref[...], pc, preferred_element_type=F32)
    imp = imp8[:, 0:Q_TILE]
    for h in range(1, NSA_HEADS):
        imp = imp + imp8[:, h * Q_TILE:(h + 1) * Q_TILE]
    n_blk = imp.shape[0]
    blk = lax.broadcasted_iota(I32, (n_blk, Q_TILE), 0)
    cur = tq // SEL_BLOCK
    forced = (blk == 0) | (blk == cur) | (blk == cur - 1)
    imp = jnp.where(blk <= cur, jnp.where(forced, FORCED_SCORE, imp), -jnp.inf)
    chosen = jnp.zeros((n_blk, Q_TILE), F32)
    for _ in range(n_sel):
        top = jnp.max(imp, axis=0, keepdims=True)
        first = jnp.min(jnp.where(imp == top, blk, n_blk), axis=0, keepdims=True)
        pick = (blk == first) & (top > -jnp.inf)
        chosen = jnp.where(pick, 1.0, chosen)
        imp = jnp.where(pick, -jnp.inf, imp)
    sel_ref[...] = chosen.astype(BF16)

    def sel_mask(r0, carry):
        hit = jnp.dot(exp_ref[pl.ds(r0, rows), :], sel_ref[...], preferred_element_type=F32)
        kpos = r0 + lax.broadcasted_iota(I32, (rows, Q_TILE), 0)
        return jnp.where((hit > 0.5) & (kpos <= tq), 0.0, -jnp.inf), carry

    nch = (q0 + Q_TILE + rows - 1) // rows
    o_s = _flash_chunks(qs, ks_ref, vst_ref, 0, nch, rows, sel_mask, 0, acc_ref, m_ref, l_ref, NSA_HEADS)
    out = out + gate_row(1) * o_s

    band = WINDOW + Q_TILE
    w0 = pl.multiple_of(jnp.maximum(q0 - WINDOW, 0), Q_TILE)
    kpos = w0 + lax.broadcasted_iota(I32, (band, Q_TILE), 0)
    bias_w = jnp.where((kpos <= tq) & (kpos > tq - WINDOW), 0.0, -jnp.inf)
    sw = lax.dot_general(kw_ref[pl.ds(w0, band), :], qs, _NT, preferred_element_type=F32)
    sw = sw + jnp.concatenate([bias_w] * NSA_HEADS, axis=1)
    ew = jnp.exp(sw - jnp.max(sw, axis=0, keepdims=True))
    o_w = jnp.dot(vwt_ref[:, pl.ds(w0, band)], ew.astype(BF16), preferred_element_type=F32)
    out = out + gate_row(2) * (o_w / jnp.sum(ew, axis=0, keepdims=True))
    o_ref[...] = _unstack_heads_t(out, NSA_HEADS).astype(o_ref.dtype)


def _nsa_attention(c, kcmp, vcmpt, bsz, seq):
    t = bsz * seq
    nq = seq // Q_TILE
    ng = seq // CMP_STRIDE
    n_cmp = (seq - CMP_BLOCK) // CMP_STRIDE + 1
    n_blk = seq // SEL_BLOCK
    n_sel = min(N_SEL_BLOCKS, n_blk)
    j = np.arange(ng)[None, :] * CMP_STRIDE
    b0 = np.arange(n_blk)[:, None] * SEL_BLOCK
    ovl = ((j <= b0 + SEL_BLOCK - 1) & (j + CMP_BLOCK - 1 >= b0) & (np.arange(ng)[None, :] < n_cmp))
    ovl = jnp.asarray(ovl.astype(np.float32), BF16)
    expand = jnp.asarray((np.arange(seq)[:, None] // SEL_BLOCK == np.arange(n_blk)[None, :]).astype(np.float32), BF16)
    qrow = lambda w: pl.BlockSpec((Q_TILE, w), lambda b, n: (b * nq + n, 0))
    seqrow = lambda w: pl.BlockSpec((seq, w), lambda b, n: (b, 0))
    vrow = lambda r: pl.BlockSpec((HEAD_DIM, seq), lambda b, n: (r, b))
    const = lambda a: pl.BlockSpec(a.shape, lambda b, n: (0, 0))
    nqh = NSA_HEADS * Q_TILE
    return pl.pallas_call(
        functools.partial(_nsa_body, n_sel=n_sel, n_cmp=n_cmp),
        grid=(bsz, nq),
        in_specs=[qrow(512), qrow(128),
                  pl.BlockSpec((ng, HEAD_DIM), lambda b, n: (b, 0)), pl.BlockSpec((HEAD_DIM, ng), lambda b, n: (0, b)),
                  const(ovl), const(expand), seqrow(HEAD_DIM), vrow(1), seqrow(HEAD_DIM), vrow(2)],
        out_specs=qrow(512),
        out_shape=jax.ShapeDtypeStruct((t, 512), BF16),
        scratch_shapes=[pltpu.VMEM((n_blk, Q_TILE), BF16), pltpu.VMEM((HEAD_DIM, nqh), F32),
                        pltpu.VMEM((1, nqh), F32), pltpu.VMEM((1, nqh), F32)],
        compiler_params=_cparams("parallel", "arbitrary"),
        name="nsa_attention",
    )(c["nq"], c["misc"], kcmp, vcmpt, ovl, expand, c["nks"], c["vt"], c["nkw"], c["vt"])


GDN_TILE = 512


def _split_bf16(a):
    hi = a.astype(BF16)
    return hi, (a - hi.astype(F32)).astype(BF16)


def _dot3(a, b, dims=None):
    ah, al = _split_bf16(a)
    bh, bl = _split_bf16(b)
    if dims is None:
        f = lambda u, v: jnp.dot(u, v, preferred_element_type=F32)
    else:
        f = lambda u, v: lax.dot_general(u, v, dims, preferred_element_type=F32)
    return f(ah, bh) + (f(ah, bl) + f(al, bh))


def _bdot(a, b, dims=None):
    a, b = a.astype(BF16), b.astype(BF16)
    if dims is None:
        return jnp.dot(a, b, preferred_element_type=F32)
    return lax.dot_general(a, b, dims, preferred_element_type=F32)


def _gdn_body(qkv_ref, misc_ref, z_ref, convw_ref, alog_ref, dtb_ref, normg_ref, o_ref,
              state_ref, tail_ref, q_s, k_s, v_s, gc_s, beta_s):
    tt = qkv_ref.shape[0]
    ch = GDN_CHUNK
    hw = GDN_HEADS * GDN_DIM

    @pl.when(pl.program_id(1) == 0)
    def _():
        state_ref[...] = jnp.zeros(state_ref.shape, F32)
        tail_ref[...] = jnp.zeros(tail_ref.shape, F32)

    x = qkv_ref[...]
    w = convw_ref[...]
    tail = tail_ref[...]
    x8 = x[0:8, :]
    row8 = lax.broadcasted_iota(I32, (8, 1), 0)
    acc = x * w[CONV_WIDTH - 1:CONV_WIDTH, :]
    acc8 = x8 * w[CONV_WIDTH - 1:CONV_WIDTH, :]
    for k in range(1, CONV_WIDTH):
        wk = w[CONV_WIDTH - 1 - k:CONV_WIDTH - k, :]
        acc = acc + pltpu.roll(x, k, 0) * wk
        acc8 = acc8 + jnp.where(row8 < k, pltpu.roll(tail, k, 0), pltpu.roll(x8, k, 0)) * wk
    tail_ref[...] = x[tt - 8:tt, :]
    rowt = lax.broadcasted_iota(I32, (tt, 1), 0)
    acc = jnp.where(rowt < 8, jnp.concatenate([acc8, acc[8:, :]], axis=0), acc)
    xc = acc * jax.nn.sigmoid(acc)

    def l2n(a):
        return a * lax.rsqrt(jnp.sum(a * a, axis=-1, keepdims=True) + RMS_EPS)

    for h in range(GDN_HEADS):
        q_s[h] = l2n(xc[:, h * GDN_DIM:(h + 1) * GDN_DIM]) * (GDN_DIM ** -0.5)
        k_s[h] = l2n(xc[:, hw + h * GDN_DIM:hw + (h + 1) * GDN_DIM])
        v_s[h] = xc[:, 2 * hw + h * GDN_DIM:2 * hw + (h + 1) * GDN_DIM]

    misc = misc_ref[...]
    sp = misc + dtb_ref[...]
    softplus = jnp.maximum(sp, 0.0) + jnp.log(1.0 + jnp.exp(-jnp.abs(sp)))
    g = -jnp.exp(alog_ref[...]) * softplus
    beta_s[...] = jax.nn.sigmoid(misc)
    ri = lax.broadcasted_iota(I32, (tt, tt), 0)
    ci = lax.broadcasted_iota(I32, (tt, tt), 1)
    blk_tril = jnp.where((ri >= ci) & (ri // ch == ci // ch), 1.0, 0.0)
    gc_s[...] = jnp.dot(blk_tril, g, preferred_element_type=F32, precision=lax.Precision.HIGHEST)

    r64 = lax.broadcasted_iota(I32, (ch, ch), 0)
    c64 = lax.broadcasted_iota(I32, (ch, ch), 1)
    tri = r64 >= c64
    strict = r64 > c64
    norm_g = normg_ref[...]

    def chunk_step(c, _):
        r0 = pl.multiple_of(c * ch, ch)
        gcs = gc_s[pl.ds(r0, ch), :]
        gct = jnp.concatenate([gcs, jnp.zeros_like(gcs)], axis=0).T
        bts = beta_s[pl.ds(r0, ch), :]
        heads = range(GDN_HEADS)
        q = [q_s[h, pl.ds(r0, ch), :] for h in heads]
        k = [k_s[h, pl.ds(r0, ch), :] for h in heads]
        gcol = [gcs[:, MISC_GA + h:MISC_GA + h + 1] for h in heads]
        decay = [jnp.exp(jnp.where(tri, gcol[h] - gct[MISC_GA + h:MISC_GA + h + 1, 0:ch], -jnp.inf)) for h in heads]
        beta = [bts[:, MISC_GB + h:MISC_GB + h + 1] for h in heads]
        kb = [k[h] * beta[h] for h in heads]
        eg = [jnp.exp(gcol[h]) for h in heads]
        nmat = [-jnp.where(strict, _bdot(kb[h], k[h], _NT) * decay[h], 0.0) for h in heads]
        y = [jnp.concatenate([v_s[h, pl.ds(r0, ch), :] * beta[h], kb[h] * eg[h]], axis=1) for h in heads]
        for level in range(6):
            y = [y[h] + _dot3(nmat[h], y[h]) for h in heads]
            if level < 5:
                nmat = [_dot3(nmat[h], nmat[h]) for h in heads]
        attn = [jnp.where(tri, _bdot(q[h], k[h], _NT) * decay[h], 0.0) for h in heads]
        glast = [gcol[h][ch - 1:ch, :] for h in heads]
        state = [state_ref[h] for h in heads]
        v_new = [y[h][:, :GDN_DIM] - _bdot(y[h][:, GDN_DIM:], state[h]) for h in heads]
        o = [_bdot(q[h] * eg[h], state[h]) + _bdot(attn[h], v_new[h]) for h in heads]
        for h in heads:
            k_dec = k[h] * jnp.exp(glast[h] - gcol[h])
            state_ref[h] = state[h] * jnp.exp(glast[h]) + _bdot(k_dec, v_new[h], (((0,), (0,)), ((), ())))
        for h in heads:
            on = o[h] * lax.rsqrt(jnp.mean(o[h] * o[h], axis=-1, keepdims=True) + RMS_EPS) * norm_g
            zz = z_ref[pl.ds(r0, ch), h * GDN_DIM:(h + 1) * GDN_DIM]
            o_ref[pl.ds(r0, ch), h * GDN_DIM:(h + 1) * GDN_DIM] = (on * (zz * jax.nn.sigmoid(zz))).astype(o_ref.dtype)
        return 0

    lax.fori_loop(0, tt // ch, chunk_step, 0)


def _gdn_mixer(c, conv_w, a_log, dt_bias, norm_g, bsz, seq):
    t = bsz * seq
    tt = min(GDN_TILE, seq)
    nt = seq // tt
    lane_row = lambda vals, off: jnp.zeros((1, LANES), F32).at[0, off:off + GDN_HEADS].set(vals.astype(F32))
    alog = lane_row(a_log, MISC_GA)
    dtb = lane_row(dt_bias, MISC_GA)
    row = lambda w: pl.BlockSpec((tt, w), lambda b, j: (b * nt + j, 0))
    const = lambda a: pl.BlockSpec(a.shape, lambda b, j: (0, 0))
    ng = norm_g.reshape(1, GDN_DIM).astype(F32)
    hs = pltpu.VMEM((GDN_HEADS, tt, GDN_DIM), F32)
    return pl.pallas_call(
        _gdn_body,
        grid=(bsz, nt),
        in_specs=[row(GDN_QKV), row(LANES), row(GDN_HEADS * GDN_DIM), const(conv_w), const(alog), const(dtb), const(ng)],
        out_specs=row(GDN_HEADS * GDN_DIM),
        out_shape=jax.ShapeDtypeStruct((t, GDN_HEADS * GDN_DIM), BF16),
        scratch_shapes=[pltpu.VMEM((GDN_HEADS, GDN_DIM, GDN_DIM), F32), pltpu.VMEM((8, GDN_QKV), F32),
                        hs, hs, hs, pltpu.VMEM((tt, LANES), F32), pltpu.VMEM((tt, LANES), F32)],
        compiler_params=_cparams("parallel", "arbitrary"),
        name="gdn_mixer",
    )(c["gqkv"], c["misc"], c["gz"], conv_w.astype(F32), alog, dtb, ng)


ROW_TILE = 256


def _layer_norm(v, g, b):
    mu = jnp.mean(v, axis=-1, keepdims=True)
    vc = v - mu
    var = jnp.mean(vc * vc, axis=-1, keepdims=True)
    return vc * lax.rsqrt(var + LN_EPS) * g + b


def _merge_body(ya_ref, yb_ref, yc_ref, mg_ref, x_ref, wb_ref, wo_ref, g_ref, b_ref, wrt_ref,
                x1_ref, x1b_ref, xpk_ref, sct_ref, *, alpha):
    d = x_ref.shape[1]
    merged = mg_ref[:, 0:d] * jnp.dot(ya_ref[...], wb_ref[0], preferred_element_type=F32)
    merged = merged + mg_ref[:, d:2 * d] * jnp.dot(yb_ref[...], wb_ref[1], preferred_element_type=F32)
    merged = merged + mg_ref[:, 2 * d:3 * d] * jnp.dot(yc_ref[...], wb_ref[2], preferred_element_type=F32)
    y = alpha * x_ref[...] + jnp.dot(merged.astype(BF16), wo_ref[...], preferred_element_type=F32)
    x1 = _layer_norm(y, g_ref[...], b_ref[...])
    x1_ref[...] = x1
    x1b = x1.astype(BF16)
    x1b_ref[...] = x1b
    xpk_ref[...] = _pack_bf16_pairs(x1b)
    logits_t = lax.dot_general(wrt_ref[...], x1b, _NT, preferred_element_type=F32)
    sct_ref[...] = jax.nn.sigmoid(logits_t)


def _merge_out(ya, yb, yc, mg, x2d, w_branch, w_out, ln_g, ln_b, w_router, alpha):
    t, d = x2d.shape
    tm = ROW_TILE
    row = lambda w: pl.BlockSpec((tm, w), lambda i: (i, 0))
    wb = w_branch.astype(BF16)
    wo = w_out.astype(BF16)
    wrt = w_router.T.astype(BF16)
    g = ln_g.reshape(1, d)
    b = ln_b.reshape(1, d)
    return pl.pallas_call(
        functools.partial(_merge_body, alpha=alpha),
        grid=(t // tm,),
        in_specs=[row(BRANCH_WIDTH)] * 3 + [row(3 * d), row(d),
                  pl.BlockSpec(wb.shape, lambda i: (0, 0, 0)), pl.BlockSpec(wo.shape, lambda i: (0, 0)),
                  pl.BlockSpec((1, d), lambda i: (0, 0)), pl.BlockSpec((1, d), lambda i: (0, 0)),
                  pl.BlockSpec(wrt.shape, lambda i: (0, 0))],
        out_specs=[row(d), row(d), row(d // 2), pl.BlockSpec((N_EXPERTS, tm), lambda i: (0, i))],
        out_shape=[jax.ShapeDtypeStruct((t, d), F32), jax.ShapeDtypeStruct((t, d), BF16),
                   jax.ShapeDtypeStruct((t, d // 2), jnp.uint32), jax.ShapeDtypeStruct((N_EXPERTS, t), F32)],
        compiler_params=_cparams("parallel"),
        name="merge_out_ln",
    )(ya, yb, yc, mg, x2d, wb, wo, g, b, wrt)


ROUTE_TILE = 512


def _route_body(sct_ref, bias_ref, eidx_ref, gate_ref, rank_ref, cnt_ref):
    sc = sct_ref[...]
    n = sc.shape[1]
    per = N_EXPERTS // N_GROUPS
    biased = sc + bias_ref[...]
    b3 = biased.reshape(N_GROUPS, per, n)
    sub = lax.broadcasted_iota(I32, (N_GROUPS, per, n), 1)
    m1 = jnp.max(b3, axis=1, keepdims=True)
    first = jnp.min(jnp.where(b3 == m1, sub, per), axis=1, keepdims=True)
    m2 = jnp.max(jnp.where(sub == first, -jnp.inf, b3), axis=1, keepdims=True)
    gs = (m1 + m2).reshape(N_GROUPS, n)
    gi = lax.broadcasted_iota(I32, (N_GROUPS, n), 0)
    gmask = jnp.zeros((N_GROUPS, n), F32)
    for _ in range(TOPK_GROUPS):
        top = jnp.max(gs, axis=0, keepdims=True)
        pick = gi == jnp.min(jnp.where(gs == top, gi, N_GROUPS), axis=0, keepdims=True)
        gmask = jnp.where(pick, 1.0, gmask)
        gs = jnp.where(pick, -jnp.inf, gs)
    emask = jnp.broadcast_to(gmask.reshape(N_GROUPS, 1, n), (N_GROUPS, per, n)).reshape(N_EXPERTS, n)
    cand = jnp.where(emask > 0.5, biased, -jnp.inf)
    ei = lax.broadcasted_iota(I32, (N_EXPERTS, n), 0)
    ids, gates, picks = [], [], []
    for _ in range(TOP_K):
        top = jnp.max(cand, axis=0, keepdims=True)
        idx = jnp.min(jnp.where(cand == top, ei, N_EXPERTS), axis=0, keepdims=True)
        pick = ei == idx
        ids.append(idx)
        picks.append(pick)
        gates.append(jnp.sum(jnp.where(pick, sc, 0.0), axis=0, keepdims=True))
        cand = jnp.where(pick, -jnp.inf, cand)
    gate = jnp.concatenate(gates, axis=0)
    gate = gate / jnp.sum(gate, axis=0, keepdims=True) * ROUTED_SCALE
    eidx_ref[...] = jnp.concatenate(ids, axis=0)
    gate_ref[...] = gate

    @pl.when(pl.program_id(0) == 0)
    def _():
        cnt_ref[...] = jnp.zeros(cnt_ref.shape, F32)

    sel = jnp.where(cand == -jnp.inf, 1.0, 0.0) * jnp.where(emask > 0.5, 1.0, 0.0)
    ri = lax.broadcasted_iota(I32, (n, n), 0)
    ci = lax.broadcasted_iota(I32, (n, n), 1)
    before = jnp.dot(sel.astype(BF16), jnp.where(ri < ci, 1.0, 0.0).astype(BF16), preferred_element_type=F32)
    pos = before + cnt_ref[...]
    rank_ref[...] = jnp.concatenate(
        [jnp.sum(jnp.where(pk, pos, 0.0), axis=0, keepdims=True) for pk in picks], axis=0).astype(I32)
    cnt_ref[...] = cnt_ref[...] + jnp.sum(sel, axis=1, keepdims=True)


def _route(sct, router_bias):
    e, t = sct.shape
    tn = ROUTE_TILE
    col = lambda r: pl.BlockSpec((r, tn), lambda i: (0, i))
    return pl.pallas_call(
        _route_body,
        grid=(t // tn,),
        in_specs=[col(e), pl.BlockSpec((e, 1), lambda i: (0, 0))],
        out_specs=[col(TOP_K), col(TOP_K), col(TOP_K), pl.BlockSpec((e, 1), lambda i: (0, 0))],
        out_shape=[jax.ShapeDtypeStruct((TOP_K, t), I32), jax.ShapeDtypeStruct((TOP_K, t), F32),
                   jax.ShapeDtypeStruct((TOP_K, t), I32), jax.ShapeDtypeStruct((e, 1), F32)],
        compiler_params=_cparams("arbitrary"),
        name="route_topk",
    )(sct, router_bias.reshape(e, 1).astype(F32))


U32 = jnp.uint32
DISPATCH_TILE = 512


def _pack_bf16_pairs(xb):
    n = xb.shape[1] // 2
    bits = pltpu.bitcast(xb.astype(F32), U32)
    return (bits[:, :n] >> 16) | (bits[:, n:] & jnp.uint32(0xFFFF0000))


def _unpack_bf16_pairs(packed):
    lo = pltpu.bitcast(packed << 16, F32).astype(BF16)
    hi = pltpu.bitcast(packed & jnp.uint32(0xFFFF0000), F32).astype(BF16)
    return lo, hi


def _dispatch_body(last_ref, nused_ref, dest_hbm, x_ref, xs_hbm, idx_smem, zero_ref, isem, rsem, zsem):
    i = pl.program_id(0)
    tn = DISPATCH_TILE
    n_blocks = xs_hbm.shape[0] // EXPERT_BLOCK

    @pl.when(i == 0)
    def _():
        zero_ref[...] = jnp.zeros(zero_ref.shape, zero_ref.dtype)

        def block_copy(start):
            return pltpu.make_async_copy(zero_ref, xs_hbm.at[pl.ds(pl.multiple_of(start, EXPERT_BLOCK), EXPERT_BLOCK)],
                                         zsem)

        def fill(e, _):
            @pl.when(last_ref[e] >= 0)
            def _():
                block_copy(jnp.maximum(last_ref[e], 0)).start()
            return 0

        def drain(e, _):
            @pl.when(last_ref[e] >= 0)
            def _():
                block_copy(jnp.maximum(last_ref[e], 0)).wait()
            return 0

        def fill_tail(b, _):
            block_copy(b * EXPERT_BLOCK).start()
            return 0

        def drain_tail(b, _):
            block_copy(b * EXPERT_BLOCK).wait()
            return 0

        lax.fori_loop(0, N_EXPERTS, fill, 0)
        lax.fori_loop(nused_ref[0], n_blocks, fill_tail, 0)
        lax.fori_loop(0, N_EXPERTS, drain, 0)
        lax.fori_loop(nused_ref[0], n_blocks, drain_tail, 0)

    idx_cp = pltpu.make_async_copy(dest_hbm.at[i], idx_smem, isem)
    idx_cp.start()
    idx_cp.wait()

    def issue(j8, _):
        base = pl.multiple_of(j8 * 8, 8)
        for jj in range(8):
            src = x_ref.at[pl.ds(base + jj, 1)]
            for k in range(TOP_K):
                pltpu.make_async_copy(src, xs_hbm.at[pl.ds(idx_smem[k * tn + base + jj], 1)],
                                      rsem).start(priority=k % 2)
        return 0

    lax.fori_loop(0, tn // 8, issue, 0)
    for k in range(TOP_K):
        pltpu.make_async_copy(x_ref, xs_hbm.at[pl.ds(0, tn)], rsem).wait()


def _dispatch(xpk, dest_tiles, last_blk, n_used, n_pad):
    t, w = xpk.shape
    tn = DISPATCH_TILE
    anyspec = pl.BlockSpec(memory_space=pl.ANY)
    grid_spec = pltpu.PrefetchScalarGridSpec(
        num_scalar_prefetch=2,
        grid=(t // tn,),
        in_specs=[anyspec, pl.BlockSpec((tn, w), lambda i, last, nu: (i, 0))],
        out_specs=anyspec,
        scratch_shapes=[pltpu.SMEM((TOP_K * tn,), I32), pltpu.VMEM((EXPERT_BLOCK, w), xpk.dtype),
                        pltpu.SemaphoreType.DMA, pltpu.SemaphoreType.DMA, pltpu.SemaphoreType.DMA],
    )
    return pl.pallas_call(
        _dispatch_body,
        grid_spec=grid_spec,
        out_shape=jax.ShapeDtypeStruct((n_pad, w), xpk.dtype),
        compiler_params=_cparams("arbitrary"),
        name="moe_dispatch",
    )(last_blk, n_used, dest_tiles, xpk)


def _expert_body(blk_e_ref, nused_ref, xb_ref, wgu_ref, wd_ref, o_ref):
    i = pl.program_id(0)
    half = xb_ref.shape[1]

    @pl.when(i < nused_ref[0])
    def _():
        lo, hi = _unpack_bf16_pairs(xb_ref[...])
        gu = (jnp.dot(lo, wgu_ref[0, :half, :], preferred_element_type=F32)
              + jnp.dot(hi, wgu_ref[0, half:, :], preferred_element_type=F32))
        gg, uu = gu[:, :D_EXPERT], gu[:, D_EXPERT:]
        hmid = (gg * jax.nn.sigmoid(gg) * uu).astype(BF16)
        o_ref[...] = jnp.dot(hmid, wd_ref[0], preferred_element_type=F32)

    @pl.when(i >= nused_ref[0])
    def _():
        o_ref[...] = jnp.zeros(o_ref.shape, o_ref.dtype)


def _expert_ffn(xs, blk_e, n_used, wgu, wd):
    n_pad, half = xs.shape
    d = 2 * half
    nb = n_pad // EXPERT_BLOCK
    grid_spec = pltpu.PrefetchScalarGridSpec(
        num_scalar_prefetch=2,
        grid=(nb,),
        in_specs=[pl.BlockSpec((EXPERT_BLOCK, half), lambda i, be, nu: (jnp.minimum(i, nu[0] - 1), 0)),
                  pl.BlockSpec((1, d, 2 * D_EXPERT), lambda i, be, nu: (be[i], 0, 0)),
                  pl.BlockSpec((1, D_EXPERT, d), lambda i, be, nu: (be[i], 0, 0))],
        out_specs=pl.BlockSpec((EXPERT_BLOCK, d), lambda i, be, nu: (i, 0)),
    )
    return pl.pallas_call(
        _expert_body,
        grid_spec=grid_spec,
        out_shape=jax.ShapeDtypeStruct((n_pad, d), F32),
        compiler_params=_cparams("arbitrary"),
        name="expert_ffn",
    )(blk_e, n_used, xs, wgu, wd)


def _dest_body(eidx_ref, rank_ref, start_ref, dest_ref):
    n = eidx_ref.shape[1]
    ei = lax.broadcasted_iota(I32, (N_EXPERTS, n), 0)
    start = start_ref[...]
    rows = [jnp.sum(jnp.where(ei == eidx_ref[k:k + 1, :], start, 0.0), axis=0, keepdims=True)
            for k in range(TOP_K)]
    dest_ref[...] = rank_ref[...] + jnp.concatenate(rows, axis=0).astype(I32)


def _moe_plan(eidx_t, rank_t, counts, n_tok):
    n_rows = n_tok * TOP_K
    n_blocks = -(-n_rows // EXPERT_BLOCK) + N_EXPERTS
    cnt = counts.reshape(-1).astype(I32)
    padded = (cnt + EXPERT_BLOCK - 1) // EXPERT_BLOCK * EXPERT_BLOCK
    pad_end = jnp.cumsum(padded)
    pad_start = pad_end - padded
    tn = ROUTE_TILE
    col = pl.BlockSpec((TOP_K, tn), lambda i: (0, i))
    dest = pl.pallas_call(
        _dest_body,
        grid=(n_tok // tn,),
        in_specs=[col, col, pl.BlockSpec((N_EXPERTS, 1), lambda i: (0, 0))],
        out_specs=col,
        out_shape=jax.ShapeDtypeStruct((TOP_K, n_tok), I32),
        compiler_params=_cparams("parallel"),
        name="route_dest",
    )(eidx_t, rank_t, pad_start.astype(F32).reshape(N_EXPERTS, 1))
    blk_first = jnp.arange(n_blocks, dtype=I32) * EXPERT_BLOCK
    blk_e = jnp.minimum(jnp.sum((pad_end[None, :] <= blk_first[:, None]).astype(I32), axis=1), N_EXPERTS - 1)
    n_used = (pad_end[-1] // EXPERT_BLOCK).astype(I32).reshape(1)
    last_blk = jnp.where(padded > 0, pad_end - EXPERT_BLOCK, -1).astype(I32)
    return dest, blk_e, n_used, last_blk, n_blocks * EXPERT_BLOCK


def _tile_indices(dest, tn):
    k, t = dest.shape
    return dest.reshape(k, t // tn, tn).transpose(1, 0, 2).reshape(t // tn, k * tn)


def _ffn_out_body(dest_hbm, ye_hbm, gate_ref, x1_ref, x1b_ref, p_ref, wsgu_ref, wsd_ref, wpg_ref, wpp_ref,
                  g_ref, b_ref, o_ref, idx0_smem, idx1_smem, rows_ref, isem, rsem, *, alpha):
    idx_smem = (idx0_smem, idx1_smem)
    i = pl.program_id(0)
    n_tiles = pl.num_programs(0)
    tm = x1_ref.shape[0]

    def gather_tile(tile, s):
        idx = idx_smem[s]
        idx_cp = pltpu.make_async_copy(dest_hbm.at[tile], idx, isem)
        idx_cp.start()
        idx_cp.wait()

        def issue(j8, _):
            base = pl.multiple_of(j8 * 8, 8)
            for jj in range(8):
                for k in range(TOP_K):
                    pltpu.make_async_copy(ye_hbm.at[pl.ds(idx[k * tm + base + jj], 1)],
                                          rows_ref.at[s, k, pl.ds(base + jj, 1)],
                                          rsem.at[s]).start(priority=k % 2)
            return 0

        lax.fori_loop(0, tm // 8, issue, 0)

    def run(slot):
        @pl.when(i + 1 < n_tiles)
        def _():
            gather_tile(i + 1, 1 - slot)

        xb = x1b_ref[...]
        su = jnp.dot(xb, wsgu_ref[...], preferred_element_type=F32)
        sg, uu = su[:, :D_SHARED], su[:, D_SHARED:]
        shared = jnp.dot((sg * jax.nn.sigmoid(sg) * uu).astype(BF16), wsd_ref[...], preferred_element_type=F32)
        pg = jax.nn.sigmoid(jnp.dot(xb, wpg_ref[...], preferred_element_type=F32))
        ple = pg * jnp.dot(p_ref[...].astype(BF16), wpp_ref[...], preferred_element_type=F32)

        g8 = gate_ref[...]
        gate_cols = jnp.concatenate([g8, jnp.zeros((LANES - TOP_K, tm), F32)], axis=0).T
        for k in range(TOP_K):
            pltpu.make_async_copy(ye_hbm.at[pl.ds(0, tm)], rows_ref.at[slot, k], rsem.at[slot]).wait()
        routed = gate_cols[:, 0:1] * rows_ref[slot, 0]
        for k in range(1, TOP_K):
            routed = routed + gate_cols[:, k:k + 1] * rows_ref[slot, k]
        y = alpha * x1_ref[...] + (routed + shared) + ple
        o_ref[...] = _layer_norm(y, g_ref[...], b_ref[...])

    @pl.when(i == 0)
    def _():
        gather_tile(0, 0)

    @pl.when(i % 2 == 0)
    def _():
        run(0)

    @pl.when(i % 2 == 1)
    def _():
        run(1)


def _ffn_out(dest_tiles, ye, gate_t, x1, x1b, p2d, w_sgu, w_sd, w_pg, w_pp, ln_g, ln_b, alpha):
    t, d = x1.shape
    tm = ROW_TILE
    row = lambda w: pl.BlockSpec((tm, w), lambda i: (i, 0))
    const = lambda a: pl.BlockSpec(a.shape, lambda i: (0, 0))
    anyspec = pl.BlockSpec(memory_space=pl.ANY)
    ws = [w.astype(BF16) for w in (w_sgu, w_sd, w_pg, w_pp)]
    g = ln_g.reshape(1, d)
    b = ln_b.reshape(1, d)
    return pl.pallas_call(
        functools.partial(_ffn_out_body, alpha=alpha),
        grid=(t // tm,),
        in_specs=[anyspec, anyspec, pl.BlockSpec((TOP_K, tm), lambda i: (0, i)), row(d), row(d), row(PLE_DIM)]
                 + [const(w) for w in ws] + [const(g), const(b)],
        out_specs=row(d),
        out_shape=jax.ShapeDtypeStruct((t, d), F32),
        scratch_shapes=[pltpu.SMEM((TOP_K * tm,), I32), pltpu.SMEM((TOP_K * tm,), I32),
                        pltpu.VMEM((2, TOP_K, tm, d), F32),
                        pltpu.SemaphoreType.DMA, pltpu.SemaphoreType.DMA((2,))],
        compiler_params=_cparams("arbitrary"),
        name="ffn_out_ln",
    )(dest_tiles, ye, gate_t, x1, x1b, p2d, *ws, g, b)


def kernel(x, p, w_in, w_dsa_uv, w_cmp_k, pe_cmp_k, w_cmp_v, pe_cmp_v, conv_w, a_log, dt_bias, gdn_norm, w_branch, w_out, ln1_g, ln1_b, w_router, router_bias, w_gate_up, w_down, w_sh_gate_up, w_sh_down, w_ple_proj, w_ple_gate, ln2_g, ln2_b):
    bsz, seq, d = x.shape
    depth = w_in.shape[0]
    alpha = (2 * depth) ** 0.25
    t = bsz * seq
    tabs64, _ = _rope_lane_tables(seq, HEAD_DIM)
    tabs32, _ = _rope_lane_tables(seq, IDX_DIM)
    x2d = x.reshape(t, d)
    for i in range(depth):
        c = _in_projection(x2d, _split_w_in(w_in[i]), tabs64, tabs32, seq)
        ya = _dsa_attention(c, w_dsa_uv[i], bsz, seq)
        kcmp, vcmpt = _nsa_compress(c, w_cmp_k[i], pe_cmp_k[i], w_cmp_v[i], pe_cmp_v[i], bsz, seq)
        yb = _nsa_attention(c, kcmp, vcmpt, bsz, seq)
        yc = _gdn_mixer(c, conv_w[i], a_log[i], dt_bias[i], gdn_norm[i], bsz, seq)
        x1, x1b, xpk, sct = _merge_out(ya, yb, yc, c["mg"], x2d, w_branch[i], w_out[i], ln1_g[i], ln1_b[i],
                                       w_router[i], alpha)
        eidx_t, gate_t, rank_t, counts = _route(sct, router_bias[i])
        dest, blk_e, n_used, last_blk, n_pad = _moe_plan(eidx_t, rank_t, counts, t)
        xs = _dispatch(xpk, _tile_indices(dest, DISPATCH_TILE), last_blk, n_used, n_pad)
        ye = _expert_ffn(xs, blk_e, n_used, w_gate_up[i].astype(BF16), w_down[i].astype(BF16))
        x2d = _ffn_out(_tile_indices(dest, ROW_TILE), ye, gate_t, x1, x1b, p[i].reshape(t, PLE_DIM),
                       w_sh_gate_up[i], w_sh_down[i], w_ple_gate[i], w_ple_proj[i], ln2_g[i], ln2_b[i], alpha)
    return x2d.reshape(bsz, seq, d)
```

```python
import functools
import math

import numpy as np
import jax
import jax.numpy as jnp
from jax import lax
from jax.experimental import pallas as pl
from jax.experimental.pallas import tpu as pltpu

F32 = jnp.float32
BF16 = jnp.bfloat16
I32 = jnp.int32

HEAD_DIM = 64
ROPE_THETA = 500000.0
ROPE_FRACTION = 4
DSA_HEADS = 8
DSA_LATENT = 64
IDX_HEADS = 4
IDX_DIM = 32
DSA_TOPK = 256
NSA_HEADS = 8
CMP_BLOCK = 32
CMP_STRIDE = 16
SEL_BLOCK = 32
N_SEL_BLOCKS = 8
WINDOW = 256
FORCED_SCORE = 1.0e4
GDN_HEADS = 4
GDN_DIM = 128
CONV_WIDTH = 4
GDN_CHUNK = 64
BRANCH_WIDTH = 512
N_BRANCHES = 3
GDN_QKV = 3 * GDN_HEADS * GDN_DIM
N_EXPERTS = 64
TOP_K = 8
N_GROUPS = 8
TOPK_GROUPS = 4
D_EXPERT = 256
D_SHARED = 256
ROUTED_SCALE = 2.5
EXPERT_BLOCK = 1024
PLE_DIM = 256
LN_EPS = 1e-5
RMS_EPS = 1e-6

IN_SPLITS = (
    ("dsa_q", DSA_HEADS * DSA_LATENT), ("dsa_kv", DSA_LATENT),
    ("idx_q", IDX_HEADS * IDX_DIM), ("idx_k", IDX_DIM), ("idx_w", IDX_HEADS),
    ("nsa_q", NSA_HEADS * HEAD_DIM),
    ("nsa_kc", HEAD_DIM), ("nsa_vc", HEAD_DIM),
    ("nsa_ks", HEAD_DIM), ("nsa_vs", HEAD_DIM),
    ("nsa_kw", HEAD_DIM), ("nsa_vw", HEAD_DIM),
    ("nsa_g", NSA_HEADS * 3),
    ("gdn_qkv", GDN_QKV), ("gdn_a", GDN_HEADS), ("gdn_b", GDN_HEADS), ("gdn_z", GDN_HEADS * GDN_DIM),
    ("merge_g", N_BRANCHES * 1024),
)

LANES = 128
Q_TILE = 256
KEY_CHUNK = 512
TIE_BLOCK = 256
VMEM_LIMIT = 56 * 1024 * 1024
INT_MIN = -2 ** 31
NEG_BIG = -1e30


def _cparams(*sem):
    return pltpu.CompilerParams(dimension_semantics=sem, vmem_limit_bytes=VMEM_LIMIT)


def _rope_lane_tables(seq, head_dim):
    rot = head_dim // ROPE_FRACTION
    half = rot // 2
    inv_freq = ROPE_THETA ** (-(jnp.arange(half, dtype=F32) * 2.0 / rot))
    ang = jnp.arange(seq, dtype=F32)[:, None] * inv_freq[None, :]
    cos, sin = jnp.cos(ang), jnp.sin(ang)
    one = jnp.ones((seq, head_dim - rot), F32)
    c = jnp.concatenate([cos, cos, one], axis=-1)
    sa = jnp.concatenate([-sin, jnp.zeros((seq, head_dim - half), F32)], axis=-1)
    sb = jnp.concatenate([jnp.zeros((seq, half), F32), sin, jnp.zeros((seq, head_dim - rot), F32)], axis=-1)
    rep = LANES // head_dim
    return tuple(jnp.tile(a, (1, rep)) for a in (c, sa, sb)), half


def _rope_apply(y, c, sa, sb, half):
    n = y.shape[-1]
    rep = n // LANES
    if rep > 1:
        c, sa, sb = (jnp.concatenate([a] * rep, axis=-1) for a in (c, sa, sb))
    up = pltpu.roll(y, n - half, 1)
    dn = pltpu.roll(y, half, 1)
    return y * c + up * sa + dn * sb


IN_TILE = 512


def _inproj_body(x_ref, w64_ref, w32_ref, wpl_ref, wvt_ref, wmg_ref,
                 c64_ref, sa64_ref, sb64_ref, c32_ref, sa32_ref, sb32_ref,
                 dq_ref, nq_ref, krot_ref, nkc_ref, nks_ref, nkw_ref, iq_ref, ik_ref,
                 nvc_ref, misc_ref, gz_ref, gqkv_ref, vt_ref, mg_ref):
    xb = x_ref[...].astype(BF16)
    c64, sa64, sb64 = c64_ref[...], sa64_ref[...], sb64_ref[...]
    half64 = HEAD_DIM // ROPE_FRACTION // 2
    half32 = IDX_DIM // ROPE_FRACTION // 2

    def proj(w_ref, lo, hi):
        return jnp.dot(xb, w_ref[:, lo:hi], preferred_element_type=F32)

    dq_ref[...] = _rope_apply(proj(w64_ref, 0, 512), c64, sa64, sb64, half64).astype(BF16)
    nq_ref[...] = _rope_apply(proj(w64_ref, 512, 1024), c64, sa64, sb64, half64).astype(BF16)
    k1 = _rope_apply(proj(w64_ref, 1024, 1152), c64, sa64, sb64, half64).astype(BF16)
    krot_ref[...] = k1[:, :64]
    nkc_ref[...] = k1[:, 64:]
    k2 = _rope_apply(proj(w64_ref, 1152, 1280), c64, sa64, sb64, half64).astype(BF16)
    nks_ref[...] = k2[:, :64]
    nkw_ref[...] = k2[:, 64:]
    c32, sa32, sb32 = c32_ref[...], sa32_ref[...], sb32_ref[...]
    iq_ref[...] = _rope_apply(proj(w32_ref, 0, 128), c32, sa32, sb32, half32).astype(BF16)
    ik = _rope_apply(proj(w32_ref, 128, 256), c32, sa32, sb32, half32).astype(BF16)
    ik_ref[...] = ik[:, :IDX_DIM]
    pl0 = proj(wpl_ref, 0, 128)
    nvc_ref[...] = pl0[:, :64].astype(BF16)
    misc_ref[...] = proj(wpl_ref, 128, 256)
    gz_ref[...] = proj(wpl_ref, 256, 768)
    for j in range(3):
        gqkv_ref[:, j * 512:(j + 1) * 512] = proj(wpl_ref, 768 + j * 512, 768 + (j + 1) * 512)
    vt_ref[...] = lax.dot_general(wvt_ref[...], xb, (((1,), (1,)), ((), ())),
                                  preferred_element_type=F32).astype(BF16)
    for j in range(6):
        g = proj(wmg_ref, j * 512, (j + 1) * 512)
        mg_ref[:, j * 512:(j + 1) * 512] = jax.nn.sigmoid(g)


def _split_w_in(w):
    cols = {}
    off = 0
    for name, size in IN_SPLITS:
        cols[name] = w[:, off:off + size]
        off += size
    d = w.shape[0]
    z = lambda n: jnp.zeros((d, n), w.dtype)
    w64 = jnp.concatenate([cols["dsa_q"], cols["nsa_q"], cols["dsa_kv"], cols["nsa_kc"],
                           cols["nsa_ks"], cols["nsa_kw"]], axis=1)
    w32 = jnp.concatenate([cols["idx_q"], cols["idx_k"], z(LANES - IDX_DIM)], axis=1)
    misc = jnp.concatenate([cols["idx_w"], cols["gdn_a"], cols["gdn_b"], cols["nsa_g"]], axis=1)
    misc = jnp.concatenate([misc, z(LANES - misc.shape[1])], axis=1)
    wpl = jnp.concatenate([cols["nsa_vc"], z(64), misc, cols["gdn_z"], cols["gdn_qkv"]], axis=1)
    wvt = jnp.concatenate([cols["dsa_kv"], cols["nsa_vs"], cols["nsa_vw"], z(64)], axis=1).T
    return [a.astype(BF16) for a in (w64, w32, wpl, wvt, cols["merge_g"])]


MISC_IW = 0
MISC_GA = IDX_HEADS
MISC_GB = MISC_GA + GDN_HEADS
MISC_NG = MISC_GB + GDN_HEADS


def _in_projection(x2d, wparts, tabs64, tabs32, seq):
    t, d = x2d.shape
    tm = IN_TILE
    nt = seq // tm
    w64, w32, wpl, wvt, wmg = wparts
    row = lambda w: pl.BlockSpec((tm, w), lambda i: (i, 0))
    full = lambda a: pl.BlockSpec(a.shape, lambda i: (0, 0), pipeline_mode=pl.Buffered(1))
    tab = pl.BlockSpec((tm, LANES), lambda i: (i % nt, 0))
    outs = [
        ("dq", 512, BF16), ("nq", 512, BF16), ("krot", 64, BF16), ("nkc", 64, BF16), ("nks", 64, BF16),
        ("nkw", 64, BF16), ("iq", 128, BF16), ("ik", IDX_DIM, BF16), ("nvc", 64, BF16),
        ("misc", 128, F32), ("gz", 512, F32), ("gqkv", GDN_QKV, F32),
    ]
    out_shape = [jax.ShapeDtypeStruct((t, w), dt) for _, w, dt in outs]
    out_specs = [row(w) for _, w, _ in outs]
    out_shape += [jax.ShapeDtypeStruct((256, t), BF16), jax.ShapeDtypeStruct((t, 3 * d), F32)]
    out_specs += [pl.BlockSpec((256, tm), lambda i: (0, i)), row(3 * d)]
    res = pl.pallas_call(
        _inproj_body,
        grid=(t // tm,),
        in_specs=[row(d), full(w64), full(w32), full(wpl), full(wvt), full(wmg)] + [tab] * 6,
        out_specs=out_specs,
        out_shape=out_shape,
        compiler_params=_cparams("parallel"),
        name="in_projection",
    )(x2d, w64, w32, wpl, wvt, wmg, *tabs64, *tabs32)
    names = [n for n, _, _ in outs] + ["vt", "mg"]
    return dict(zip(names, res))


_NT = (((1,), (1,)), ((), ()))


def _stack_heads(q, n_heads, width):
    return jnp.concatenate([q[:, h * width:(h + 1) * width] for h in range(n_heads)], axis=0)


def _unstack_heads_t(o_t, n_heads):
    q = o_t.shape[1] // n_heads
    rows = jnp.concatenate([o_t[:, h * q:(h + 1) * q] for h in range(n_heads)], axis=0)
    return rows.T


def _flash_chunks(qs, k_ref, vt_ref, lo, hi, rows, mask_fn, carry0, acc_ref, m_ref, l_ref, n_heads):
    m_ref[...] = jnp.full(m_ref.shape, NEG_BIG, F32)
    l_ref[...] = jnp.zeros(l_ref.shape, F32)
    acc_ref[...] = jnp.zeros(acc_ref.shape, F32)

    def body(c, carry):
        r0 = pl.multiple_of(c * rows, rows)
        bias, carry = mask_fn(r0, carry)
        kc = k_ref[pl.ds(r0, rows), :]
        s = lax.dot_general(kc, qs, _NT, preferred_element_type=F32)
        s = s + jnp.concatenate([bias] * n_heads, axis=1)
        m_old = m_ref[...]
        m_new = jnp.maximum(m_old, jnp.max(s, axis=0, keepdims=True))
        alpha = jnp.exp(m_old - m_new)
        p = jnp.exp(s - m_new)
        l_ref[...] = alpha * l_ref[...] + jnp.sum(p, axis=0, keepdims=True)
        pv = jnp.dot(vt_ref[:, pl.ds(r0, rows)], p.astype(BF16), preferred_element_type=F32)
        acc_ref[...] = acc_ref[...] * alpha + pv
        m_ref[...] = m_new
        return carry

    lax.fori_loop(lo, hi, body, carry0)
    return acc_ref[...] / jnp.maximum(l_ref[...], 1e-30)


def _dsa_body(iq_ref, dq_ref, misc_ref, ik_ref, krot_ref, vt_ref, wuvt_ref, o_ref,
              keys_ref, acc_ref, m_ref, l_ref, *, n_sel):
    rows = KEY_CHUNK
    sub = TIE_BLOCK
    q0 = pl.program_id(1) * Q_TILE
    nch = (q0 + Q_TILE + rows - 1) // rows
    tq = q0 + lax.broadcasted_iota(I32, (1, Q_TILE), 1)
    misc_t = misc_ref[...].T
    coef = (IDX_HEADS ** -0.5) * (IDX_DIM ** -0.5)
    iq = iq_ref[...]
    int_min = jnp.int32(INT_MIN)
    wrow = [misc_t[MISC_IW + h:MISC_IW + h + 1, :] * coef for h in range(IDX_HEADS)]

    def score_block(r0):
        ikc = ik_ref[pl.ds(r0, sub), :]
        acc = jnp.zeros((sub, Q_TILE), F32)
        for h in range(IDX_HEADS):
            s = lax.dot_general(ikc, iq[:, h * IDX_DIM:(h + 1) * IDX_DIM], _NT, preferred_element_type=F32)
            acc = acc + jnp.maximum(s, 0.0) * wrow[h]
        acc = jnp.where(acc == 0.0, 0.0, acc)
        bits = pltpu.bitcast(acc, I32)
        key = bits ^ ((bits >> 31) & jnp.int32(0x7FFFFFFF))
        kpos = r0 + lax.broadcasted_iota(I32, (sub, Q_TILE), 0)
        keys_ref[pl.ds(r0, sub), :] = jnp.where(kpos <= tq, key, int_min)

    def score_chunk(c, _):
        for j in range(rows // sub):
            score_block(pl.multiple_of(c * rows + j * sub, sub))
        return 0

    lax.fori_loop(0, nch, score_chunk, 0)

    def count(pred):
        def body(c, acc):
            r0 = pl.multiple_of(c * rows, rows)
            hit = jnp.where(pred(keys_ref[pl.ds(r0, rows), :]), 1, 0).astype(I32)
            return acc + jnp.sum(hit.reshape(rows // 8, 8, Q_TILE), axis=0)
        acc = lax.fori_loop(0, nch, body, jnp.zeros((8, Q_TILE), I32))
        return jnp.sum(acc, axis=0, keepdims=True)

    kq = jnp.minimum(tq + 1, n_sel)

    def bit_step(i, tu):
        cand_u = tu | lax.shift_left(jnp.int32(1), 31 - i)
        cand = cand_u ^ int_min
        cnt = count(lambda kk: kk >= cand)
        return jnp.where(cnt >= kq, cand_u, tu)

    thr = lax.fori_loop(0, 32, bit_step, jnp.zeros((1, Q_TILE), I32)) ^ int_min
    need = (kq - count(lambda kk: kk > thr)).astype(F32)

    ri = lax.broadcasted_iota(I32, (sub, sub), 0)
    ci = lax.broadcasted_iota(I32, (sub, sub), 1)
    tri = jnp.where(ri > ci, 1.0, 0.0).astype(BF16)

    def mask_fn(r0, seen):
        parts = []
        for j in range(rows // sub):
            kk = keys_ref[pl.ds(pl.multiple_of(r0 + j * sub, sub), sub), :]
            tie = jnp.where(kk == thr, 1.0, 0.0)
            before = jnp.dot(tri, tie.astype(BF16), preferred_element_type=F32) + seen
            take = (kk > thr) | ((kk == thr) & (before < need))
            parts.append(jnp.where(take, 0.0, -jnp.inf))
            seen = seen + jnp.sum(tie, axis=0, keepdims=True)
        return jnp.concatenate(parts, axis=0), seen

    qs = _stack_heads(dq_ref[...], DSA_HEADS, DSA_LATENT) * jnp.asarray(DSA_LATENT ** -0.5, BF16)
    o_lat = _flash_chunks(qs, krot_ref, vt_ref, 0, nch, rows, mask_fn, jnp.zeros((1, Q_TILE), F32),
                          acc_ref, m_ref, l_ref, DSA_HEADS)
    outs = []
    for h in range(DSA_HEADS):
        oh = o_lat[:, h * Q_TILE:(h + 1) * Q_TILE].astype(BF16)
        outs.append(jnp.dot(wuvt_ref[h], oh, preferred_element_type=F32))
    o_ref[...] = jnp.concatenate(outs, axis=0).T.astype(o_ref.dtype)


def _dsa_attention(c, w_uv, bsz, seq):
    t = bsz * seq
    nq = seq // Q_TILE
    n_sel = min(DSA_TOPK, seq // 4)
    wuvt = jnp.swapaxes(w_uv, 1, 2).astype(BF16)
    qrow = lambda w: pl.BlockSpec((Q_TILE, w), lambda b, n: (b * nq + n, 0))
    seqrow = lambda w: pl.BlockSpec((seq, w), lambda b, n: (b, 0))
    nqh = DSA_HEADS * Q_TILE
    return pl.pallas_call(
        functools.partial(_dsa_body, n_sel=n_sel),
        grid=(bsz, nq),
        in_specs=[qrow(128), qrow(512), qrow(128), seqrow(IDX_DIM), seqrow(DSA_LATENT),
                  pl.BlockSpec((DSA_LATENT, seq), lambda b, n: (0, b)),
                  pl.BlockSpec(wuvt.shape, lambda b, n: (0, 0, 0))],
        out_specs=qrow(512),
        out_shape=jax.ShapeDtypeStruct((t, 512), BF16),
        scratch_shapes=[pltpu.VMEM((seq, Q_TILE), I32), pltpu.VMEM((DSA_LATENT, nqh), F32),
                        pltpu.VMEM((1, nqh), F32), pltpu.VMEM((1, nqh), F32)],
        compiler_params=_cparams("parallel", "arbitrary"),
        name="dsa_attention",
    )(c["iq"], c["dq"], c["misc"], c["ik"], c["krot"], c["vt"], wuvt)


CMP_PER_ROW = CMP_STRIDE * HEAD_DIM


def _nsa_cmp_body(kc_ref, vc_ref, wk_ref, wv_ref, pek_ref, pev_ref, kcmp_ref, vcmpt_ref):
    n = kc_ref.shape[0]

    def compress(x_ref, w_ref, pe_ref):
        xf = x_ref[...].astype(F32)
        lo = jnp.dot((xf + pe_ref[0:1, :]).astype(BF16), w_ref[0], preferred_element_type=F32)
        hi = jnp.dot((xf + pe_ref[1:2, :]).astype(BF16), w_ref[1], preferred_element_type=F32)
        return lo + pltpu.roll(hi, n - 1, 0)

    kcmp_ref[...] = compress(kc_ref, wk_ref, pek_ref).astype(BF16)
    vc = compress(vc_ref, wv_ref, pev_ref)
    vpad = jnp.concatenate([vc, jnp.zeros_like(vc)], axis=1)
    vcmpt_ref[...] = vpad.T[:HEAD_DIM, :].astype(BF16)


def _nsa_compress(c, w_ck, pe_k, w_cv, pe_v, bsz, seq):
    ng = seq // CMP_STRIDE
    kc2 = c["nkc"].reshape(bsz * ng, CMP_PER_ROW)
    vc2 = c["nvc"].reshape(bsz * ng, CMP_PER_ROW)
    wk = w_ck.reshape(2, CMP_PER_ROW, HEAD_DIM).astype(BF16)
    wv = w_cv.reshape(2, CMP_PER_ROW, HEAD_DIM).astype(BF16)
    pek = pe_k.reshape(2, CMP_PER_ROW)
    pev = pe_v.reshape(2, CMP_PER_ROW)
    grp = pl.BlockSpec((ng, CMP_PER_ROW), lambda b: (b, 0))
    wsp = pl.BlockSpec((2, CMP_PER_ROW, HEAD_DIM), lambda b: (0, 0, 0))
    psp = pl.BlockSpec((2, CMP_PER_ROW), lambda b: (0, 0))
    return pl.pallas_call(
        _nsa_cmp_body,
        grid=(bsz,),
        in_specs=[grp, grp, wsp, wsp, psp, psp],
        out_specs=[pl.BlockSpec((ng, HEAD_DIM), lambda b: (b, 0)), pl.BlockSpec((HEAD_DIM, ng), lambda b: (0, b))],
        out_shape=[jax.ShapeDtypeStruct((bsz * ng, HEAD_DIM), BF16), jax.ShapeDtypeStruct((HEAD_DIM, bsz * ng), BF16)],
        compiler_params=_cparams("parallel"),
        name="nsa_compress",
    )(kc2, vc2, wk, wv, pek, pev)


def _nsa_body(nq_ref, misc_ref, kcmp_ref, vcmpt_ref, ovl_ref, exp_ref, ks_ref, vst_ref, kw_ref, vwt_ref,
              o_ref, sel_ref, acc_ref, m_ref, l_ref, *, n_sel, n_cmp):
    rows = KEY_CHUNK
    n = pl.program_id(1)
    q0 = n * Q_TILE
    tq = q0 + lax.broadcasted_iota(I32, (1, Q_TILE), 1)
    qs = _stack_heads(nq_ref[...], NSA_HEADS, HEAD_DIM) * jnp.asarray(HEAD_DIM ** -0.5, BF16)
    gate_t = jax.nn.sigmoid(misc_ref[...]).T

    def gate_row(branch):
        return jnp.concatenate([gate_t[MISC_NG + 3 * h + branch:MISC_NG + 3 * h + branch + 1, :]
                                for h in range(NSA_HEADS)], axis=1)

    ng = kcmp_ref.shape[0]
    jrow = lax.broadcasted_iota(I32, (ng, Q_TILE), 0)
    ok_c = (jrow * CMP_STRIDE + (CMP_BLOCK - 1) <= tq) & (jrow < n_cmp)
    bias_c = jnp.where(ok_c, 0.0, -jnp.inf)
    s = lax.dot_general(kcmp_ref[...], qs, _NT, preferred_element_type=F32)
    s = s + jnp.concatenate([bias_c] * NSA_HEADS, axis=1)
    mx = jnp.maximum(jnp.max(s, axis=0, keepdims=True), NEG_BIG)
    e = jnp.exp(s - mx)
    pc = (e / jnp.maximum(jnp.sum(e, axis=0, keepdims=True), 1e-30)).astype(BF16)
    out = gate_row(0) * jnp.dot(vcmpt_ref[...], pc, preferred_element_type=F32)

    imp8 = jnp.dot(ovl_ref[...], pc, preferred_element_type=F32)
    imp = imp8[:, 0:Q_TILE]
    for h in range(1, NSA_HEADS):
        imp = imp + imp8[:, h * Q_TILE:(h + 1) * Q_TILE]
    n_blk = imp.shape[0]
    blk = lax.broadcasted_iota(I32, (n_blk, Q_TILE), 0)
    cur = tq // SEL_BLOCK
    forced = (blk == 0) | (blk == cur) | (blk == cur - 1)
    imp = jnp.where(blk <= cur, jnp.where(forced, FORCED_SCORE, imp), -jnp.inf)
    chosen = jnp.zeros((n_blk, Q_TILE), F32)
    for _ in range(n_sel):
        top = jnp.max(imp, axis=0, keepdims=True)
        first = jnp.min(jnp.where(imp == top, blk, n_blk), axis=0, keepdims=True)
        pick = (blk == first) & (top > -jnp.inf)
        chosen = jnp.where(pick, 1.0, chosen)
        imp = jnp.where(pick, -jnp.inf, imp)
    sel_ref[...] = chosen.astype(BF16)

    def sel_mask(r0, carry):
        hit = jnp.dot(exp_ref[pl.ds(r0, rows), :], sel_ref[...], preferred_element_type=F32)
        kpos = r0 + lax.broadcasted_iota(I32, (rows, Q_TILE), 0)
        return jnp.where((hit > 0.5) & (kpos <= tq), 0.0, -jnp.inf), carry

    nch = (q0 + Q_TILE + rows - 1) // rows
    o_s = _flash_chunks(qs, ks_ref, vst_ref, 0, nch, rows, sel_mask, 0, acc_ref, m_ref, l_ref, NSA_HEADS)
    out = out + gate_row(1) * o_s

    band = WINDOW + Q_TILE
    w0 = pl.multiple_of(jnp.maximum(q0 - WINDOW, 0), Q_TILE)
    kpos = w0 + lax.broadcasted_iota(I32, (band, Q_TILE), 0)
    bias_w = jnp.where((kpos <= tq) & (kpos > tq - WINDOW), 0.0, -jnp.inf)
    sw = lax.dot_general(kw_ref[pl.ds(w0, band), :], qs, _NT, preferred_element_type=F32)
    sw = sw + jnp.concatenate([bias_w] * NSA_HEADS, axis=1)
    ew = jnp.exp(sw - jnp.max(sw, axis=0, keepdims=True))
    o_w = jnp.dot(vwt_ref[:, pl.ds(w0, band)], ew.astype(BF16), preferred_element_type=F32)
    out = out + gate_row(2) * (o_w / jnp.sum(ew, axis=0, keepdims=True))
    o_ref[...] = _unstack_heads_t(out, NSA_HEADS).astype(o_ref.dtype)


def _nsa_attention(c, kcmp, vcmpt, bsz, seq):
    t = bsz * seq
    nq = seq // Q_TILE
    ng = seq // CMP_STRIDE
    n_cmp = (seq - CMP_BLOCK) // CMP_STRIDE + 1
    n_blk = seq // SEL_BLOCK
    n_sel = min(N_SEL_BLOCKS, n_blk)
    j = np.arange(ng)[None, :] * CMP_STRIDE
    b0 = np.arange(n_blk)[:, None] * SEL_BLOCK
    ovl = ((j <= b0 + SEL_BLOCK - 1) & (j + CMP_BLOCK - 1 >= b0) & (np.arange(ng)[None, :] < n_cmp))
    ovl = jnp.asarray(ovl.astype(np.float32), BF16)
    expand = jnp.asarray((np.arange(seq)[:, None] // SEL_BLOCK == np.arange(n_blk)[None, :]).astype(np.float32), BF16)
    qrow = lambda w: pl.BlockSpec((Q_TILE, w), lambda b, n: (b * nq + n, 0))
    seqrow = lambda w: pl.BlockSpec((seq, w), lambda b, n: (b, 0))
    vrow = lambda r: pl.BlockSpec((HEAD_DIM, seq), lambda b, n: (r, b))
    const = lambda a: pl.BlockSpec(a.shape, lambda b, n: (0, 0))
    nqh = NSA_HEADS * Q_TILE
    return pl.pallas_call(
        functools.partial(_nsa_body, n_sel=n_sel, n_cmp=n_cmp),
        grid=(bsz, nq),
        in_specs=[qrow(512), qrow(128),
                  pl.BlockSpec((ng, HEAD_DIM), lambda b, n: (b, 0)), pl.BlockSpec((HEAD_DIM, ng), lambda b, n: (0, b)),
                  const(ovl), const(expand), seqrow(HEAD_DIM), vrow(1), seqrow(HEAD_DIM), vrow(2)],
        out_specs=qrow(512),
        out_shape=jax.ShapeDtypeStruct((t, 512), BF16),
        scratch_shapes=[pltpu.VMEM((n_blk, Q_TILE), BF16), pltpu.VMEM((HEAD_DIM, nqh), F32),
                        pltpu.VMEM((1, nqh), F32), pltpu.VMEM((1, nqh), F32)],
        compiler_params=_cparams("parallel", "arbitrary"),
        name="nsa_attention",
    )(c["nq"], c["misc"], kcmp, vcmpt, ovl, expand, c["nks"], c["vt"], c["nkw"], c["vt"])


GDN_TILE = 512


def _split_bf16(a):
    hi = a.astype(BF16)
    return hi, (a - hi.astype(F32)).astype(BF16)


def _dot3(a, b, dims=None):
    ah, al = _split_bf16(a)
    bh, bl = _split_bf16(b)
    if dims is None:
        f = lambda u, v: jnp.dot(u, v, preferred_element_type=F32)
    else:
        f = lambda u, v: lax.dot_general(u, v, dims, preferred_element_type=F32)
    return f(ah, bh) + (f(ah, bl) + f(al, bh))


def _bdot(a, b, dims=None):
    a, b = a.astype(BF16), b.astype(BF16)
    if dims is None:
        return jnp.dot(a, b, preferred_element_type=F32)
    return lax.dot_general(a, b, dims, preferred_element_type=F32)


def _gdn_body(qkv_ref, misc_ref, z_ref, convw_ref, alog_ref, dtb_ref, normg_ref, o_ref,
              state_ref, tail_ref, q_s, k_s, v_s, gc_s, beta_s):
    tt = qkv_ref.shape[0]
    ch = GDN_CHUNK
    hw = GDN_HEADS * GDN_DIM

    @pl.when(pl.program_id(1) == 0)
    def _():
        state_ref[...] = jnp.zeros(state_ref.shape, F32)
        tail_ref[...] = jnp.zeros(tail_ref.shape, F32)

    x = qkv_ref[...]
    w = convw_ref[...]
    tail = tail_ref[...]
    x8 = x[0:8, :]
    row8 = lax.broadcasted_iota(I32, (8, 1), 0)
    acc = x * w[CONV_WIDTH - 1:CONV_WIDTH, :]
    acc8 = x8 * w[CONV_WIDTH - 1:CONV_WIDTH, :]
    for k in range(1, CONV_WIDTH):
        wk = w[CONV_WIDTH - 1 - k:CONV_WIDTH - k, :]
        acc = acc + pltpu.roll(x, k, 0) * wk
        acc8 = acc8 + jnp.where(row8 < k, pltpu.roll(tail, k, 0), pltpu.roll(x8, k, 0)) * wk
    tail_ref[...] = x[tt - 8:tt, :]
    rowt = lax.broadcasted_iota(I32, (tt, 1), 0)
    acc = jnp.where(rowt < 8, jnp.concatenate([acc8, acc[8:, :]], axis=0), acc)
    xc = acc * jax.nn.sigmoid(acc)

    def l2n(a):
        return a * lax.rsqrt(jnp.sum(a * a, axis=-1, keepdims=True) + RMS_EPS)

    for h in range(GDN_HEADS):
        q_s[h] = l2n(xc[:, h * GDN_DIM:(h + 1) * GDN_DIM]) * (GDN_DIM ** -0.5)
        k_s[h] = l2n(xc[:, hw + h * GDN_DIM:hw + (h + 1) * GDN_DIM])
        v_s[h] = xc[:, 2 * hw + h * GDN_DIM:2 * hw + (h + 1) * GDN_DIM]

    misc = misc_ref[...]
    sp = misc + dtb_ref[...]
    softplus = jnp.maximum(sp, 0.0) + jnp.log(1.0 + jnp.exp(-jnp.abs(sp)))
    g = -jnp.exp(alog_ref[...]) * softplus
    beta_s[...] = jax.nn.sigmoid(misc)
    ri = lax.broadcasted_iota(I32, (tt, tt), 0)
    ci = lax.broadcasted_iota(I32, (tt, tt), 1)
    blk_tril = jnp.where((ri >= ci) & (ri // ch == ci // ch), 1.0, 0.0)
    gc_s[...] = jnp.dot(blk_tril, g, preferred_element_type=F32, precision=lax.Precision.HIGHEST)

    r64 = lax.broadcasted_iota(I32, (ch, ch), 0)
    c64 = lax.broadcasted_iota(I32, (ch, ch), 1)
    tri = r64 >= c64
    strict = r64 > c64
    norm_g = normg_ref[...]

    def chunk_step(c, _):
        r0 = pl.multiple_of(c * ch, ch)
        gcs = gc_s[pl.ds(r0, ch), :]
        gct = jnp.concatenate([gcs, jnp.zeros_like(gcs)], axis=0).T
        bts = beta_s[pl.ds(r0, ch), :]
        heads = range(GDN_HEADS)
        q = [q_s[h, pl.ds(r0, ch), :] for h in heads]
        k = [k_s[h, pl.ds(r0, ch), :] for h in heads]
        gcol = [gcs[:, MISC_GA + h:MISC_GA + h + 1] for h in heads]
        decay = [jnp.exp(jnp.where(tri, gcol[h] - gct[MISC_GA + h:MISC_GA + h + 1, 0:ch], -jnp.inf)) for h in heads]
        beta = [bts[:, MISC_GB + h:MISC_GB + h + 1] for h in heads]
        kb = [k[h] * beta[h] for h in heads]
        eg = [jnp.exp(gcol[h]) for h in heads]
        nmat = [-jnp.where(strict, _bdot(kb[h], k[h], _NT) * decay[h], 0.0) for h in heads]
        y = [jnp.concatenate([v_s[h, pl.ds(r0, ch), :] * beta[h], kb[h] * eg[h]], axis=1) for h in heads]
        for level in range(6):
            y = [y[h] + _dot3(nmat[h], y[h]) for h in heads]
            if level < 5:
                nmat = [_dot3(nmat[h], nmat[h]) for h in heads]
        attn = [jnp.where(tri, _bdot(q[h], k[h], _NT) * decay[h], 0.0) for h in heads]
        glast = [gcol[h][ch - 1:ch, :] for h in heads]
        state = [state_ref[h] for h in heads]
        v_new = [y[h][:, :GDN_DIM] - _bdot(y[h][:, GDN_DIM:], state[h]) for h in heads]
        o = [_bdot(q[h] * eg[h], state[h]) + _bdot(attn[h], v_new[h]) for h in heads]
        for h in heads:
            k_dec = k[h] * jnp.exp(glast[h] - gcol[h])
            state_ref[h] = state[h] * jnp.exp(glast[h]) + _bdot(k_dec, v_new[h], (((0,), (0,)), ((), ())))
        for h in heads:
            on = o[h] * lax.rsqrt(jnp.mean(o[h] * o[h], axis=-1, keepdims=True) + RMS_EPS) * norm_g
            zz = z_ref[pl.ds(r0, ch), h * GDN_DIM:(h + 1) * GDN_DIM]
            o_ref[pl.ds(r0, ch), h * GDN_DIM:(h + 1) * GDN_DIM] = (on * (zz * jax.nn.sigmoid(zz))).astype(o_ref.dtype)
        return 0

    lax.fori_loop(0, tt // ch, chunk_step, 0)


def _gdn_mixer(c, conv_w, a_log, dt_bias, norm_g, bsz, seq):
    t = bsz * seq
    tt = min(GDN_TILE, seq)
    nt = seq // tt
    lane_row = lambda vals, off: jnp.zeros((1, LANES), F32).at[0, off:off + GDN_HEADS].set(vals.astype(F32))
    alog = lane_row(a_log, MISC_GA)
    dtb = lane_row(dt_bias, MISC_GA)
    row = lambda w: pl.BlockSpec((tt, w), lambda b, j: (b * nt + j, 0))
    const = lambda a: pl.BlockSpec(a.shape, lambda b, j: (0, 0))
    ng = norm_g.reshape(1, GDN_DIM).astype(F32)
    hs = pltpu.VMEM((GDN_HEADS, tt, GDN_DIM), F32)
    return pl.pallas_call(
        _gdn_body,
        grid=(bsz, nt),
        in_specs=[row(GDN_QKV), row(LANES), row(GDN_HEADS * GDN_DIM), const(conv_w), const(alog), const(dtb), const(ng)],
        out_specs=row(GDN_HEADS * GDN_DIM),
        out_shape=jax.ShapeDtypeStruct((t, GDN_HEADS * GDN_DIM), BF16),
        scratch_shapes=[pltpu.VMEM((GDN_HEADS, GDN_DIM, GDN_DIM), F32), pltpu.VMEM((8, GDN_QKV), F32),
                        hs, hs, hs, pltpu.VMEM((tt, LANES), F32), pltpu.VMEM((tt, LANES), F32)],
        compiler_params=_cparams("parallel", "arbitrary"),
        name="gdn_mixer",
    )(c["gqkv"], c["misc"], c["gz"], conv_w.astype(F32), alog, dtb, ng)


ROW_TILE = 256
MERGE_TILE = 512


def _layer_norm(v, g, b):
    mu = jnp.mean(v, axis=-1, keepdims=True)
    vc = v - mu
    var = jnp.mean(vc * vc, axis=-1, keepdims=True)
    return vc * lax.rsqrt(var + LN_EPS) * g + b


def _merge_body(ya_ref, yb_ref, yc_ref, mg_ref, x_ref, wb_ref, wo_ref, g_ref, b_ref, wrt_ref,
                x1_ref, x1b_ref, xpk_ref, sct_ref, *, alpha):
    d = x_ref.shape[1]
    merged = mg_ref[:, 0:d] * jnp.dot(ya_ref[...], wb_ref[0], preferred_element_type=F32)
    merged = merged + mg_ref[:, d:2 * d] * jnp.dot(yb_ref[...], wb_ref[1], preferred_element_type=F32)
    merged = merged + mg_ref[:, 2 * d:3 * d] * jnp.dot(yc_ref[...], wb_ref[2], preferred_element_type=F32)
    y = alpha * x_ref[...] + jnp.dot(merged.astype(BF16), wo_ref[...], preferred_element_type=F32)
    x1 = _layer_norm(y, g_ref[...], b_ref[...])
    x1_ref[...] = x1
    x1b = x1.astype(BF16)
    x1b_ref[...] = x1b
    xpk_ref[...] = _pack_bf16_pairs(x1b)
    logits_t = lax.dot_general(wrt_ref[...], x1b, _NT, preferred_element_type=F32)
    sct_ref[...] = jax.nn.sigmoid(logits_t)


def _merge_out(ya, yb, yc, mg, x2d, w_branch, w_out, ln_g, ln_b, w_router, alpha):
    t, d = x2d.shape
    tm = MERGE_TILE
    row = lambda w: pl.BlockSpec((tm, w), lambda i: (i, 0))
    wb = w_branch.astype(BF16)
    wo = w_out.astype(BF16)
    wrt = w_router.T.astype(BF16)
    g = ln_g.reshape(1, d)
    b = ln_b.reshape(1, d)
    return pl.pallas_call(
        functools.partial(_merge_body, alpha=alpha),
        grid=(t // tm,),
        in_specs=[row(BRANCH_WIDTH)] * 3 + [row(3 * d), row(d),
                  pl.BlockSpec(wb.shape, lambda i: (0, 0, 0)), pl.BlockSpec(wo.shape, lambda i: (0, 0)),
                  pl.BlockSpec((1, d), lambda i: (0, 0)), pl.BlockSpec((1, d), lambda i: (0, 0)),
                  pl.BlockSpec(wrt.shape, lambda i: (0, 0))],
        out_specs=[row(d), row(d), row(d // 2), pl.BlockSpec((N_EXPERTS, tm), lambda i: (0, i))],
        out_shape=[jax.ShapeDtypeStruct((t, d), F32), jax.ShapeDtypeStruct((t, d), BF16),
                   jax.ShapeDtypeStruct((t, d // 2), jnp.uint32), jax.ShapeDtypeStruct((N_EXPERTS, t), F32)],
        compiler_params=_cparams("parallel"),
        name="merge_out_ln",
    )(ya, yb, yc, mg, x2d, wb, wo, g, b, wrt)


ROUTE_TILE = 512


def _route_body(sct_ref, bias_ref, eidx_ref, gate_ref, rank_ref, cnt_ref):
    sc = sct_ref[...]
    n = sc.shape[1]
    per = N_EXPERTS // N_GROUPS
    biased = sc + bias_ref[...]
    b3 = biased.reshape(N_GROUPS, per, n)
    sub = lax.broadcasted_iota(I32, (N_GROUPS, per, n), 1)
    m1 = jnp.max(b3, axis=1, keepdims=True)
    first = jnp.min(jnp.where(b3 == m1, sub, per), axis=1, keepdims=True)
    m2 = jnp.max(jnp.where(sub == first, -jnp.inf, b3), axis=1, keepdims=True)
    gs = (m1 + m2).reshape(N_GROUPS, n)
    gi = lax.broadcasted_iota(I32, (N_GROUPS, n), 0)
    gmask = jnp.zeros((N_GROUPS, n), F32)
    for _ in range(TOPK_GROUPS):
        top = jnp.max(gs, axis=0, keepdims=True)
        pick = gi == jnp.min(jnp.where(gs == top, gi, N_GROUPS), axis=0, keepdims=True)
        gmask = jnp.where(pick, 1.0, gmask)
        gs = jnp.where(pick, -jnp.inf, gs)
    emask = jnp.broadcast_to(gmask.reshape(N_GROUPS, 1, n), (N_GROUPS, per, n)).reshape(N_EXPERTS, n)
    cand = jnp.where(emask > 0.5, biased, -jnp.inf)
    ei = lax.broadcasted_iota(I32, (N_EXPERTS, n), 0)
    ids, gates, picks = [], [], []
    for _ in range(TOP_K):
        top = jnp.max(cand, axis=0, keepdims=True)
        idx = jnp.min(jnp.where(cand == top, ei, N_EXPERTS), axis=0, keepdims=True)
        pick = ei == idx
        ids.append(idx)
        picks.append(pick)
        gates.append(jnp.sum(jnp.where(pick, sc, 0.0), axis=0, keepdims=True))
        cand = jnp.where(pick, -jnp.inf, cand)
    gate = jnp.concatenate(gates, axis=0)
    gate = gate / jnp.sum(gate, axis=0, keepdims=True) * ROUTED_SCALE
    eidx_ref[...] = jnp.concatenate(ids, axis=0)
    gate_ref[...] = gate

    @pl.when(pl.program_id(0) == 0)
    def _():
        cnt_ref[...] = jnp.zeros(cnt_ref.shape, F32)

    sel = jnp.where(cand == -jnp.inf, 1.0, 0.0) * jnp.where(emask > 0.5, 1.0, 0.0)
    ri = lax.broadcasted_iota(I32, (n, n), 0)
    ci = lax.broadcasted_iota(I32, (n, n), 1)
    before = jnp.dot(sel.astype(BF16), jnp.where(ri < ci, 1.0, 0.0).astype(BF16), preferred_element_type=F32)
    pos = before + cnt_ref[...]
    rank_ref[...] = jnp.concatenate(
        [jnp.sum(jnp.where(pk, pos, 0.0), axis=0, keepdims=True) for pk in picks], axis=0).astype(I32)
    cnt_ref[...] = cnt_ref[...] + jnp.sum(sel, axis=1, keepdims=True)


def _route(sct, router_bias):
    e, t = sct.shape
    tn = ROUTE_TILE
    col = lambda r: pl.BlockSpec((r, tn), lambda i: (0, i))
    return pl.pallas_call(
        _route_body,
        grid=(t // tn,),
        in_specs=[col(e), pl.BlockSpec((e, 1), lambda i: (0, 0))],
        out_specs=[col(TOP_K), col(TOP_K), col(TOP_K), pl.BlockSpec((e, 1), lambda i: (0, 0))],
        out_shape=[jax.ShapeDtypeStruct((TOP_K, t), I32), jax.ShapeDtypeStruct((TOP_K, t), F32),
                   jax.ShapeDtypeStruct((TOP_K, t), I32), jax.ShapeDtypeStruct((e, 1), F32)],
        compiler_params=_cparams("arbitrary"),
        name="route_topk",
    )(sct, router_bias.reshape(e, 1).astype(F32))


U32 = jnp.uint32
DISPATCH_TILE = 512


def _pack_bf16_pairs(xb):
    n = xb.shape[1] // 2
    bits = pltpu.bitcast(xb.astype(F32), U32)
    return (bits[:, :n] >> 16) | (bits[:, n:] & jnp.uint32(0xFFFF0000))


def _unpack_bf16_pairs(packed):
    lo = pltpu.bitcast(packed << 16, F32).astype(BF16)
    hi = pltpu.bitcast(packed & jnp.uint32(0xFFFF0000), F32).astype(BF16)
    return lo, hi


def _dispatch_body(last_ref, nused_ref, dest_hbm, x_ref, xs_hbm, idx_smem, zero_ref, isem, rsem, zsem):
    i = pl.program_id(0)
    tn = DISPATCH_TILE
    n_blocks = xs_hbm.shape[0] // EXPERT_BLOCK

    @pl.when(i == 0)
    def _():
        zero_ref[...] = jnp.zeros(zero_ref.shape, zero_ref.dtype)

        def block_copy(start):
            return pltpu.make_async_copy(zero_ref, xs_hbm.at[pl.ds(pl.multiple_of(start, EXPERT_BLOCK), EXPERT_BLOCK)],
                                         zsem)

        def fill(e, _):
            @pl.when(last_ref[e] >= 0)
            def _():
                block_copy(jnp.maximum(last_ref[e], 0)).start()
            return 0

        def drain(e, _):
            @pl.when(last_ref[e] >= 0)
            def _():
                block_copy(jnp.maximum(last_ref[e], 0)).wait()
            return 0

        def fill_tail(b, _):
            block_copy(b * EXPERT_BLOCK).start()
            return 0

        def drain_tail(b, _):
            block_copy(b * EXPERT_BLOCK).wait()
            return 0

        lax.fori_loop(0, N_EXPERTS, fill, 0)
        lax.fori_loop(nused_ref[0], n_blocks, fill_tail, 0)
        lax.fori_loop(0, N_EXPERTS, drain, 0)
        lax.fori_loop(nused_ref[0], n_blocks, drain_tail, 0)

    idx_cp = pltpu.make_async_copy(dest_hbm.at[i], idx_smem, isem)
    idx_cp.start()
    idx_cp.wait()

    def issue(j8, _):
        base = pl.multiple_of(j8 * 8, 8)
        for jj in range(8):
            src = x_ref.at[pl.ds(base + jj, 1)]
            for k in range(TOP_K):
                pltpu.make_async_copy(src, xs_hbm.at[pl.ds(idx_smem[k * tn + base + jj], 1)],
                                      rsem).start(priority=k % 2)
        return 0

    lax.fori_loop(0, tn // 8, issue, 0)
    for k in range(TOP_K):
        pltpu.make_async_copy(x_ref, xs_hbm.at[pl.ds(0, tn)], rsem).wait()


def _dispatch(xpk, dest_tiles, last_blk, n_used, n_pad):
    t, w = xpk.shape
    tn = DISPATCH_TILE
    anyspec = pl.BlockSpec(memory_space=pl.ANY)
    grid_spec = pltpu.PrefetchScalarGridSpec(
        num_scalar_prefetch=2,
        grid=(t // tn,),
        in_specs=[anyspec, pl.BlockSpec((tn, w), lambda i, last, nu: (i, 0))],
        out_specs=anyspec,
        scratch_shapes=[pltpu.SMEM((TOP_K * tn,), I32), pltpu.VMEM((EXPERT_BLOCK, w), xpk.dtype),
                        pltpu.SemaphoreType.DMA, pltpu.SemaphoreType.DMA, pltpu.SemaphoreType.DMA],
    )
    return pl.pallas_call(
        _dispatch_body,
        grid_spec=grid_spec,
        out_shape=jax.ShapeDtypeStruct((n_pad, w), xpk.dtype),
        compiler_params=_cparams("arbitrary"),
        name="moe_dispatch",
    )(last_blk, n_used, dest_tiles, xpk)


def _expert_body(blk_e_ref, nused_ref, xb_ref, wgu_ref, wd_ref, o_ref):
    i = pl.program_id(0)
    half = xb_ref.shape[1]

    @pl.when(i < nused_ref[0])
    def _():
        lo, hi = _unpack_bf16_pairs(xb_ref[...])
        gu = (jnp.dot(lo, wgu_ref[0, :half, :], preferred_element_type=F32)
              + jnp.dot(hi, wgu_ref[0, half:, :], preferred_element_type=F32))
        gg, uu = gu[:, :D_EXPERT], gu[:, D_EXPERT:]
        hmid = (gg * jax.nn.sigmoid(gg) * uu).astype(BF16)
        o_ref[...] = jnp.dot(hmid, wd_ref[0], preferred_element_type=F32)

    @pl.when(i >= nused_ref[0])
    def _():
        o_ref[...] = jnp.zeros(o_ref.shape, o_ref.dtype)


def _expert_ffn(xs, blk_e, n_used, wgu, wd):
    n_pad, half = xs.shape
    d = 2 * half
    nb = n_pad // EXPERT_BLOCK
    grid_spec = pltpu.PrefetchScalarGridSpec(
        num_scalar_prefetch=2,
        grid=(nb,),
        in_specs=[pl.BlockSpec((EXPERT_BLOCK, half), lambda i, be, nu: (jnp.minimum(i, nu[0] - 1), 0)),
                  pl.BlockSpec((1, d, 2 * D_EXPERT), lambda i, be, nu: (be[i], 0, 0)),
                  pl.BlockSpec((1, D_EXPERT, d), lambda i, be, nu: (be[i], 0, 0))],
        out_specs=pl.BlockSpec((EXPERT_BLOCK, d), lambda i, be, nu: (i, 0)),
    )
    return pl.pallas_call(
        _expert_body,
        grid_spec=grid_spec,
        out_shape=jax.ShapeDtypeStruct((n_pad, d), F32),
        compiler_params=_cparams("arbitrary"),
        name="expert_ffn",
    )(blk_e, n_used, xs, wgu, wd)


def _dest_body(eidx_ref, rank_ref, start_ref, dest_ref):
    n = eidx_ref.shape[1]
    ei = lax.broadcasted_iota(I32, (N_EXPERTS, n), 0)
    start = start_ref[...]
    rows = [jnp.sum(jnp.where(ei == eidx_ref[k:k + 1, :], start, 0.0), axis=0, keepdims=True)
            for k in range(TOP_K)]
    dest_ref[...] = rank_ref[...] + jnp.concatenate(rows, axis=0).astype(I32)


def _moe_plan(eidx_t, rank_t, counts, n_tok):
    n_rows = n_tok * TOP_K
    n_blocks = -(-n_rows // EXPERT_BLOCK) + N_EXPERTS
    cnt = counts.reshape(-1).astype(I32)
    padded = (cnt + EXPERT_BLOCK - 1) // EXPERT_BLOCK * EXPERT_BLOCK
    pad_end = jnp.cumsum(padded)
    pad_start = pad_end - padded
    tn = ROUTE_TILE
    col = pl.BlockSpec((TOP_K, tn), lambda i: (0, i))
    dest = pl.pallas_call(
        _dest_body,
        grid=(n_tok // tn,),
        in_specs=[col, col, pl.BlockSpec((N_EXPERTS, 1), lambda i: (0, 0))],
        out_specs=col,
        out_shape=jax.ShapeDtypeStruct((TOP_K, n_tok), I32),
        compiler_params=_cparams("parallel"),
        name="route_dest",
    )(eidx_t, rank_t, pad_start.astype(F32).reshape(N_EXPERTS, 1))
    blk_first = jnp.arange(n_blocks, dtype=I32) * EXPERT_BLOCK
    blk_e = jnp.minimum(jnp.sum((pad_end[None, :] <= blk_first[:, None]).astype(I32), axis=1), N_EXPERTS - 1)
    n_used = (pad_end[-1] // EXPERT_BLOCK).astype(I32).reshape(1)
    last_blk = jnp.where(padded > 0, pad_end - EXPERT_BLOCK, -1).astype(I32)
    return dest, blk_e, n_used, last_blk, n_blocks * EXPERT_BLOCK


def _tile_indices(dest, tn):
    k, t = dest.shape
    return dest.reshape(k, t // tn, tn).transpose(1, 0, 2).reshape(t // tn, k * tn)


def _ffn_out_body(dest_hbm, ye_hbm, gate_ref, x1_ref, x1b_ref, p_ref, wsgu_ref, wsd_ref, wpg_ref, wpp_ref,
                  g_ref, b_ref, o_ref, idx0_smem, idx1_smem, rows_ref, isem, rsem, *, alpha):
    idx_smem = (idx0_smem, idx1_smem)
    i = pl.program_id(0)
    n_tiles = pl.num_programs(0)
    tm = x1_ref.shape[0]

    def gather_tile(tile, s):
        idx = idx_smem[s]
        idx_cp = pltpu.make_async_copy(dest_hbm.at[tile], idx, isem)
        idx_cp.start()
        idx_cp.wait()

        def issue(j8, _):
            base = pl.multiple_of(j8 * 8, 8)
            for jj in range(8):
                for k in range(TOP_K):
                    pltpu.make_async_copy(ye_hbm.at[pl.ds(idx[k * tm + base + jj], 1)],
                                          rows_ref.at[s, k, pl.ds(base + jj, 1)],
                                          rsem.at[s]).start(priority=k % 2)
            return 0

        lax.fori_loop(0, tm // 8, issue, 0)

    def run(slot):
        @pl.when(i + 1 < n_tiles)
        def _():
            gather_tile(i + 1, 1 - slot)

        xb = x1b_ref[...]
        su = jnp.dot(xb, wsgu_ref[...], preferred_element_type=F32)
        sg, uu = su[:, :D_SHARED], su[:, D_SHARED:]
        shared = jnp.dot((sg * jax.nn.sigmoid(sg) * uu).astype(BF16), wsd_ref[...], preferred_element_type=F32)
        pg = jax.nn.sigmoid(jnp.dot(xb, wpg_ref[...], preferred_element_type=F32))
        ple = pg * jnp.dot(p_ref[...].astype(BF16), wpp_ref[...], preferred_element_type=F32)

        g8 = gate_ref[...]
        gate_cols = jnp.concatenate([g8, jnp.zeros((LANES - TOP_K, tm), F32)], axis=0).T
        for k in range(TOP_K):
            pltpu.make_async_copy(ye_hbm.at[pl.ds(0, tm)], rows_ref.at[slot, k], rsem.at[slot]).wait()
        routed = gate_cols[:, 0:1] * rows_ref[slot, 0]
        for k in range(1, TOP_K):
            routed = routed + gate_cols[:, k:k + 1] * rows_ref[slot, k]
        y = alpha * x1_ref[...] + (routed + shared) + ple
        o_ref[...] = _layer_norm(y, g_ref[...], b_ref[...])

    @pl.when(i == 0)
    def _():
        gather_tile(0, 0)

    @pl.when(i % 2 == 0)
    def _():
        run(0)

    @pl.when(i % 2 == 1)
    def _():
        run(1)


def _ffn_out(dest_tiles, ye, gate_t, x1, x1b, p2d, w_sgu, w_sd, w_pg, w_pp, ln_g, ln_b, alpha):
    t, d = x1.shape
    tm = ROW_TILE
    row = lambda w: pl.BlockSpec((tm, w), lambda i: (i, 0))
    const = lambda a: pl.BlockSpec(a.shape, lambda i: (0, 0))
    anyspec = pl.BlockSpec(memory_space=pl.ANY)
    ws = [w.astype(BF16) for w in (w_sgu, w_sd, w_pg, w_pp)]
    g = ln_g.reshape(1, d)
    b = ln_b.reshape(1, d)
    return pl.pallas_call(
        functools.partial(_ffn_out_body, alpha=alpha),
        grid=(t // tm,),
        in_specs=[anyspec, anyspec, pl.BlockSpec((TOP_K, tm), lambda i: (0, i)), row(d), row(d), row(PLE_DIM)]
                 + [const(w) for w in ws] + [const(g), const(b)],
        out_specs=row(d),
        out_shape=jax.ShapeDtypeStruct((t, d), F32),
        scratch_shapes=[pltpu.SMEM((TOP_K * tm,), I32), pltpu.SMEM((TOP_K * tm,), I32),
                        pltpu.VMEM((2, TOP_K, tm, d), F32),
                        pltpu.SemaphoreType.DMA, pltpu.SemaphoreType.DMA((2,))],
        compiler_params=_cparams("arbitrary"),
        name="ffn_out_ln",
    )(dest_tiles, ye, gate_t, x1, x1b, p2d, *ws, g, b)


def kernel(x, p, w_in, w_dsa_uv, w_cmp_k, pe_cmp_k, w_cmp_v, pe_cmp_v, conv_w, a_log, dt_bias, gdn_norm, w_branch, w_out, ln1_g, ln1_b, w_router, router_bias, w_gate_up, w_down, w_sh_gate_up, w_sh_down, w_ple_proj, w_ple_gate, ln2_g, ln2_b):
    bsz, seq, d = x.shape
    depth = w_in.shape[0]
    alpha = (2 * depth) ** 0.25
    t = bsz * seq
    tabs64, _ = _rope_lane_tables(seq, HEAD_DIM)
    tabs32, _ = _rope_lane_tables(seq, IDX_DIM)
    x2d = x.reshape(t, d)
    for i in range(depth):
        c = _in_projection(x2d, _split_w_in(w_in[i]), tabs64, tabs32, seq)
        ya = _dsa_attention(c, w_dsa_uv[i], bsz, seq)
        kcmp, vcmpt = _nsa_compress(c, w_cmp_k[i], pe_cmp_k[i], w_cmp_v[i], pe_cmp_v[i], bsz, seq)
        yb = _nsa_attention(c, kcmp, vcmpt, bsz, seq)
        yc = _gdn_mixer(c, conv_w[i], a_log[i], dt_bias[i], gdn_norm[i], bsz, seq)
        x1, x1b, xpk, sct = _merge_out(ya, yb, yc, c["mg"], x2d, w_branch[i], w_out[i], ln1_g[i], ln1_b[i],
                                       w_router[i], alpha)
        eidx_t, gate_t, rank_t, counts = _route(sct, router_bias[i])
        dest, blk_e, n_used, last_blk, n_pad = _moe_plan(eidx_t, rank_t, counts, t)
        xs = _dispatch(xpk, _tile_indices(dest, DISPATCH_TILE), last_blk, n_used, n_pad)
        ye = _expert_ffn(xs, blk_e, n_used, w_gate_up[i].astype(BF16), w_down[i].astype(BF16))
        x2d = _ffn_out(_tile_indices(dest, ROW_TILE), ye, gate_t, x1, x1b, p[i].reshape(t, PLE_DIM),
                       w_sh_gate_up[i], w_sh_down[i], w_ple_gate[i], w_ple_proj[i], ln2_g[i], ln2_b[i], alpha)
    return x2d.reshape(bsz, seq, d)
```

```python
import functools
import math

import numpy as np
import jax
import jax.numpy as jnp
from jax import lax
from jax.experimental import pallas as pl
from jax.experimental.pallas import tpu as pltpu

F32 = jnp.float32
BF16 = jnp.bfloat16
I32 = jnp.int32

HEAD_DIM = 64
ROPE_THETA = 500000.0
ROPE_FRACTION = 4
DSA_HEADS = 8
DSA_LATENT = 64
IDX_HEADS = 4
IDX_DIM = 32
DSA_TOPK = 256
NSA_HEADS = 8
CMP_BLOCK = 32
CMP_STRIDE = 16
SEL_BLOCK = 32
N_SEL_BLOCKS = 8
WINDOW = 256
FORCED_SCORE = 1.0e4
GDN_HEADS = 4
GDN_DIM = 128
CONV_WIDTH = 4
GDN_CHUNK = 64
BRANCH_WIDTH = 512
N_BRANCHES = 3
GDN_QKV = 3 * GDN_HEADS * GDN_DIM
N_EXPERTS = 64
TOP_K = 8
N_GROUPS = 8
TOPK_GROUPS = 4
D_EXPERT = 256
D_SHARED = 256
ROUTED_SCALE = 2.5
EXPERT_BLOCK = 1024
PLE_DIM = 256
LN_EPS = 1e-5
RMS_EPS = 1e-6

IN_SPLITS = (
    ("dsa_q", DSA_HEADS * DSA_LATENT), ("dsa_kv", DSA_LATENT),
    ("idx_q", IDX_HEADS * IDX_DIM), ("idx_k", IDX_DIM), ("idx_w", IDX_HEADS),
    ("nsa_q", NSA_HEADS * HEAD_DIM),
    ("nsa_kc", HEAD_DIM), ("nsa_vc", HEAD_DIM),
    ("nsa_ks", HEAD_DIM), ("nsa_vs", HEAD_DIM),
    ("nsa_kw", HEAD_DIM), ("nsa_vw", HEAD_DIM),
    ("nsa_g", NSA_HEADS * 3),
    ("gdn_qkv", GDN_QKV), ("gdn_a", GDN_HEADS), ("gdn_b", GDN_HEADS), ("gdn_z", GDN_HEADS * GDN_DIM),
    ("merge_g", N_BRANCHES * 1024),
)

LANES = 128
Q_TILE = 256
DSA_Q_TILE = 512
KEY_CHUNK = 512
TIE_BLOCK = 256
VMEM_LIMIT = 56 * 1024 * 1024
INT_MIN = -2 ** 31
NEG_BIG = -1e30


def _cparams(*sem):
    return pltpu.CompilerParams(dimension_semantics=sem, vmem_limit_bytes=VMEM_LIMIT)


def _rope_lane_tables(seq, head_dim):
    rot = head_dim // ROPE_FRACTION
    half = rot // 2
    inv_freq = ROPE_THETA ** (-(jnp.arange(half, dtype=F32) * 2.0 / rot))
    ang = jnp.arange(seq, dtype=F32)[:, None] * inv_freq[None, :]
    cos, sin = jnp.cos(ang), jnp.sin(ang)
    one = jnp.ones((seq, head_dim - rot), F32)
    c = jnp.concatenate([cos, cos, one], axis=-1)
    sa = jnp.concatenate([-sin, jnp.zeros((seq, head_dim - half), F32)], axis=-1)
    sb = jnp.concatenate([jnp.zeros((seq, half), F32), sin, jnp.zeros((seq, head_dim - rot), F32)], axis=-1)
    rep = LANES // head_dim
    return tuple(jnp.tile(a, (1, rep)) for a in (c, sa, sb)), half


def _rope_apply(y, c, sa, sb, half):
    n = y.shape[-1]
    rep = n // LANES
    if rep > 1:
        c, sa, sb = (jnp.concatenate([a] * rep, axis=-1) for a in (c, sa, sb))
    up = pltpu.roll(y, n - half, 1)
    dn = pltpu.roll(y, half, 1)
    return y * c + up * sa + dn * sb


IN_TILE = 512


def _inproj_body(x_ref, w64_ref, w32_ref, wpl_ref, wvt_ref, wmg_ref,
                 c64_ref, sa64_ref, sb64_ref, c32_ref, sa32_ref, sb32_ref,
                 dq_ref, nq_ref, krot_ref, nkc_ref, nks_ref, nkw_ref, iq_ref, ik_ref,
                 nvc_ref, misc_ref, gz_ref, gqkv_ref, vt_ref, mg_ref):
    xb = x_ref[...].astype(BF16)
    c64, sa64, sb64 = c64_ref[...], sa64_ref[...], sb64_ref[...]
    half64 = HEAD_DIM // ROPE_FRACTION // 2
    half32 = IDX_DIM // ROPE_FRACTION // 2

    def proj(w_ref, lo, hi):
        return jnp.dot(xb, w_ref[:, lo:hi], preferred_element_type=F32)

    dq_ref[...] = _rope_apply(proj(w64_ref, 0, 512), c64, sa64, sb64, half64).astype(BF16)
    nq_ref[...] = _rope_apply(proj(w64_ref, 512, 1024), c64, sa64, sb64, half64).astype(BF16)
    k1 = _rope_apply(proj(w64_ref, 1024, 1152), c64, sa64, sb64, half64).astype(BF16)
    krot_ref[...] = k1[:, :64]
    nkc_ref[...] = k1[:, 64:]
    k2 = _rope_apply(proj(w64_ref, 1152, 1280), c64, sa64, sb64, half64).astype(BF16)
    nks_ref[...] = k2[:, :64]
    nkw_ref[...] = k2[:, 64:]
    c32, sa32, sb32 = c32_ref[...], sa32_ref[...], sb32_ref[...]
    iq_ref[...] = _rope_apply(proj(w32_ref, 0, 128), c32, sa32, sb32, half32).astype(BF16)
    ik = _rope_apply(proj(w32_ref, 128, 256), c32, sa32, sb32, half32).astype(BF16)
    ik_ref[...] = ik[:, :IDX_DIM]
    pl0 = proj(wpl_ref, 0, 128)
    nvc_ref[...] = pl0[:, :64].astype(BF16)
    misc_ref[...] = proj(wpl_ref, 128, 256)
    gz_ref[...] = proj(wpl_ref, 256, 768)
    for j in range(3):
        gqkv_ref[:, j * 512:(j + 1) * 512] = proj(wpl_ref, 768 + j * 512, 768 + (j + 1) * 512)
    vt_ref[...] = lax.dot_general(wvt_ref[...], xb, (((1,), (1,)), ((), ())),
                                  preferred_element_type=F32).astype(BF16)
    for j in range(6):
        g = proj(wmg_ref, j * 512, (j + 1) * 512)
        mg_ref[:, j * 512:(j + 1) * 512] = jax.nn.sigmoid(g)


def _split_w_in(w):
    cols = {}
    off = 0
    for name, size in IN_SPLITS:
        cols[name] = w[:, off:off + size]
        off += size
    d = w.shape[0]
    z = lambda n: jnp.zeros((d, n), w.dtype)
    w64 = jnp.concatenate([cols["dsa_q"], cols["nsa_q"], cols["dsa_kv"], cols["nsa_kc"],
                           cols["nsa_ks"], cols["nsa_kw"]], axis=1)
    w32 = jnp.concatenate([cols["idx_q"], cols["idx_k"], z(LANES - IDX_DIM)], axis=1)
    misc = jnp.concatenate([cols["idx_w"], cols["gdn_a"], cols["gdn_b"], cols["nsa_g"]], axis=1)
    misc = jnp.concatenate([misc, z(LANES - misc.shape[1])], axis=1)
    wpl = jnp.concatenate([cols["nsa_vc"], z(64), misc, cols["gdn_z"], cols["gdn_qkv"]], axis=1)
    wvt = jnp.concatenate([cols["dsa_kv"], cols["nsa_vs"], cols["nsa_vw"], z(64)], axis=1).T
    return [a.astype(BF16) for a in (w64, w32, wpl, wvt, cols["merge_g"])]


MISC_IW = 0
MISC_GA = IDX_HEADS
MISC_GB = MISC_GA + GDN_HEADS
MISC_NG = MISC_GB + GDN_HEADS


def _in_projection(x2d, wparts, tabs64, tabs32, seq):
    t, d = x2d.shape
    tm = IN_TILE
    nt = seq // tm
    w64, w32, wpl, wvt, wmg = wparts
    row = lambda w: pl.BlockSpec((tm, w), lambda i: (i, 0))
    full = lambda a: pl.BlockSpec(a.shape, lambda i: (0, 0), pipeline_mode=pl.Buffered(1))
    tab = pl.BlockSpec((tm, LANES), lambda i: (i % nt, 0))
    outs = [
        ("dq", 512, BF16), ("nq", 512, BF16), ("krot", 64, BF16), ("nkc", 64, BF16), ("nks", 64, BF16),
        ("nkw", 64, BF16), ("iq", 128, BF16), ("ik", IDX_DIM, BF16), ("nvc", 64, BF16),
        ("misc", 128, F32), ("gz", 512, F32), ("gqkv", GDN_QKV, F32),
    ]
    out_shape = [jax.ShapeDtypeStruct((t, w), dt) for _, w, dt in outs]
    out_specs = [row(w) for _, w, _ in outs]
    out_shape += [jax.ShapeDtypeStruct((256, t), BF16), jax.ShapeDtypeStruct((t, 3 * d), F32)]
    out_specs += [pl.BlockSpec((256, tm), lambda i: (0, i)), row(3 * d)]
    res = pl.pallas_call(
        _inproj_body,
        grid=(t // tm,),
        in_specs=[row(d), full(w64), full(w32), full(wpl), full(wvt), full(wmg)] + [tab] * 6,
        out_specs=out_specs,
        out_shape=out_shape,
        compiler_params=_cparams("parallel"),
        name="in_projection",
    )(x2d, w64, w32, wpl, wvt, wmg, *tabs64, *tabs32)
    names = [n for n, _, _ in outs] + ["vt", "mg"]
    return dict(zip(names, res))


_NT = (((1,), (1,)), ((), ()))


def _stack_heads(q, n_heads, width):
    return jnp.concatenate([q[:, h * width:(h + 1) * width] for h in range(n_heads)], axis=0)


def _unstack_heads_t(o_t, n_heads):
    q = o_t.shape[1] // n_heads
    rows = jnp.concatenate([o_t[:, h * q:(h + 1) * q] for h in range(n_heads)], axis=0)
    return rows.T


def _flash_chunks(qs, k_ref, vt_ref, lo, hi, rows, mask_fn, carry0, acc_ref, m_ref, l_ref, n_heads):
    m_ref[...] = jnp.full(m_ref.shape, NEG_BIG, F32)
    l_ref[...] = jnp.zeros(l_ref.shape, F32)
    acc_ref[...] = jnp.zeros(acc_ref.shape, F32)

    def body(c, carry):
        r0 = pl.multiple_of(c * rows, rows)
        bias, carry = mask_fn(r0, carry)
        kc = k_ref[pl.ds(r0, rows), :]
        s = lax.dot_general(kc, qs, _NT, preferred_element_type=F32)
        s = s + jnp.concatenate([bias] * n_heads, axis=1)
        m_old = m_ref[...]
        m_new = jnp.maximum(m_old, jnp.max(s, axis=0, keepdims=True))
        alpha = jnp.exp(m_old - m_new)
        p = jnp.exp(s - m_new)
        l_ref[...] = alpha * l_ref[...] + jnp.sum(p, axis=0, keepdims=True)
        pv = jnp.dot(vt_ref[:, pl.ds(r0, rows)], p.astype(BF16), preferred_element_type=F32)
        acc_ref[...] = acc_ref[...] * alpha + pv
        m_ref[...] = m_new
        return carry

    lax.fori_loop(lo, hi, body, carry0)
    return acc_ref[...] / jnp.maximum(l_ref[...], 1e-30)


def _dsa_body(iq_ref, dq_ref, misc_ref, ik_ref, krot_ref, vt_ref, wuvt_ref, o_ref,
              keys_ref, acc_ref, m_ref, l_ref, *, n_sel):
    Q_TILE = iq_ref.shape[0]
    rows = KEY_CHUNK
    sub = TIE_BLOCK
    q0 = pl.program_id(1) * Q_TILE
    nch = (q0 + Q_TILE + rows - 1) // rows
    tq = q0 + lax.broadcasted_iota(I32, (1, Q_TILE), 1)
    misc_t = misc_ref[...].T
    coef = (IDX_HEADS ** -0.5) * (IDX_DIM ** -0.5)
    iq = iq_ref[...]
    int_min = jnp.int32(INT_MIN)
    wrow = [misc_t[MISC_IW + h:MISC_IW + h + 1, :] * coef for h in range(IDX_HEADS)]

    def score_block(r0):
        ikc = ik_ref[pl.ds(r0, sub), :]
        acc = jnp.zeros((sub, Q_TILE), F32)
        for h in range(IDX_HEADS):
            s = lax.dot_general(ikc, iq[:, h * IDX_DIM:(h + 1) * IDX_DIM], _NT, preferred_element_type=F32)
            acc = acc + jnp.maximum(s, 0.0) * wrow[h]
        acc = jnp.where(acc == 0.0, 0.0, acc)
        bits = pltpu.bitcast(acc, I32)
        key = bits ^ ((bits >> 31) & jnp.int32(0x7FFFFFFF))
        kpos = r0 + lax.broadcasted_iota(I32, (sub, Q_TILE), 0)
        keys_ref[pl.ds(r0, sub), :] = jnp.where(kpos <= tq, key, int_min)

    def score_chunk(c, _):
        for j in range(rows // sub):
            score_block(pl.multiple_of(c * rows + j * sub, sub))
        return 0

    lax.fori_loop(0, nch, score_chunk, 0)

    def count(pred):
        def body(c, acc):
            r0 = pl.multiple_of(c * rows, rows)
            hit = jnp.where(pred(keys_ref[pl.ds(r0, rows), :]), 1, 0).astype(I32)
            return acc + jnp.sum(hit.reshape(rows // 8, 8, Q_TILE), axis=0)
        acc = lax.fori_loop(0, nch, body, jnp.zeros((8, Q_TILE), I32))
        return jnp.sum(acc, axis=0, keepdims=True)

    kq = jnp.minimum(tq + 1, n_sel)

    def bit_step(i, tu):
        cand_u = tu | lax.shift_left(jnp.int32(1), 31 - i)
        cand = cand_u ^ int_min
        cnt = count(lambda kk: kk >= cand)
        return jnp.where(cnt >= kq, cand_u, tu)

    thr = lax.fori_loop(0, 32, bit_step, jnp.zeros((1, Q_TILE), I32)) ^ int_min
    need = (kq - count(lambda kk: kk > thr)).astype(F32)

    ri = lax.broadcasted_iota(I32, (sub, sub), 0)
    ci = lax.broadcasted_iota(I32, (sub, sub), 1)
    tri = jnp.where(ri > ci, 1.0, 0.0).astype(BF16)

    def mask_fn(r0, seen):
        parts = []
        for j in range(rows // sub):
            kk = keys_ref[pl.ds(pl.multiple_of(r0 + j * sub, sub), sub), :]
            tie = jnp.where(kk == thr, 1.0, 0.0)
            before = jnp.dot(tri, tie.astype(BF16), preferred_element_type=F32) + seen
            take = (kk > thr) | ((kk == thr) & (before < need))
            parts.append(jnp.where(take, 0.0, -jnp.inf))
            seen = seen + jnp.sum(tie, axis=0, keepdims=True)
        return jnp.concatenate(parts, axis=0), seen

    qs = _stack_heads(dq_ref[...], DSA_HEADS, DSA_LATENT) * jnp.asarray(DSA_LATENT ** -0.5, BF16)
    o_lat = _flash_chunks(qs, krot_ref, vt_ref, 0, nch, rows, mask_fn, jnp.zeros((1, Q_TILE), F32),
                          acc_ref, m_ref, l_ref, DSA_HEADS)
    outs = []
    for h in range(DSA_HEADS):
        oh = o_lat[:, h * Q_TILE:(h + 1) * Q_TILE].astype(BF16)
        outs.append(jnp.dot(wuvt_ref[h], oh, preferred_element_type=F32))
    o_ref[...] = jnp.concatenate(outs, axis=0).T.astype(o_ref.dtype)


def _dsa_attention(c, w_uv, bsz, seq):
    t = bsz * seq
    Q_TILE = min(DSA_Q_TILE, seq)
    nq = seq // Q_TILE
    n_sel = min(DSA_TOPK, seq // 4)
    wuvt = jnp.swapaxes(w_uv, 1, 2).astype(BF16)
    qrow = lambda w: pl.BlockSpec((Q_TILE, w), lambda b, n: (b * nq + n, 0))
    seqrow = lambda w: pl.BlockSpec((seq, w), lambda b, n: (b, 0))
    nqh = DSA_HEADS * Q_TILE
    return pl.pallas_call(
        functools.partial(_dsa_body, n_sel=n_sel),
        grid=(bsz, nq),
        in_specs=[qrow(128), qrow(512), qrow(128), seqrow(IDX_DIM), seqrow(DSA_LATENT),
                  pl.BlockSpec((DSA_LATENT, seq), lambda b, n: (0, b)),
                  pl.BlockSpec(wuvt.shape, lambda b, n: (0, 0, 0))],
        out_specs=qrow(512),
        out_shape=jax.ShapeDtypeStruct((t, 512), BF16),
        scratch_shapes=[pltpu.VMEM((seq, Q_TILE), I32), pltpu.VMEM((DSA_LATENT, nqh), F32),
                        pltpu.VMEM((1, nqh), F32), pltpu.VMEM((1, nqh), F32)],
        compiler_params=_cparams("parallel", "arbitrary"),
        name="dsa_attention",
    )(c["iq"], c["dq"], c["misc"], c["ik"], c["krot"], c["vt"], wuvt)


CMP_PER_ROW = CMP_STRIDE * HEAD_DIM


def _nsa_cmp_body(kc_ref, vc_ref, wk_ref, wv_ref, pek_ref, pev_ref, kcmp_ref, vcmpt_ref):
    n = kc_ref.shape[0]

    def compress(x_ref, w_ref, pe_ref):
        xf = x_ref[...].astype(F32)
        lo = jnp.dot((xf + pe_ref[0:1, :]).astype(BF16), w_ref[0], preferred_element_type=F32)
        hi = jnp.dot((xf + pe_ref[1:2, :]).astype(BF16), w_ref[1], preferred_element_type=F32)
        return lo + pltpu.roll(hi, n - 1, 0)

    kcmp_ref[...] = compress(kc_ref, wk_ref, pek_ref).astype(BF16)
    vc = compress(vc_ref, wv_ref, pev_ref)
    vpad = jnp.concatenate([vc, jnp.zeros_like(vc)], axis=1)
    vcmpt_ref[...] = vpad.T[:HEAD_DIM, :].astype(BF16)


def _nsa_compress(c, w_ck, pe_k, w_cv, pe_v, bsz, seq):
    ng = seq // CMP_STRIDE
    kc2 = c["nkc"].reshape(bsz * ng, CMP_PER_ROW)
    vc2 = c["nvc"].reshape(bsz * ng, CMP_PER_ROW)
    wk = w_ck.reshape(2, CMP_PER_ROW, HEAD_DIM).astype(BF16)
    wv = w_cv.reshape(2, CMP_PER_ROW, HEAD_DIM).astype(BF16)
    pek = pe_k.reshape(2, CMP_PER_ROW)
    pev = pe_v.reshape(2, CMP_PER_ROW)
    grp = pl.BlockSpec((ng, CMP_PER_ROW), lambda b: (b, 0))
    wsp = pl.BlockSpec((2, CMP_PER_ROW, HEAD_DIM), lambda b: (0, 0, 0))
    psp = pl.BlockSpec((2, CMP_PER_ROW), lambda b: (0, 0))
    return pl.pallas_call(
        _nsa_cmp_body,
        grid=(bsz,),
        in_specs=[grp, grp, wsp, wsp, psp, psp],
        out_specs=[pl.BlockSpec((ng, HEAD_DIM), lambda b: (b, 0)), pl.BlockSpec((HEAD_DIM, ng), lambda b: (0, b))],
        out_shape=[jax.ShapeDtypeStruct((bsz * ng, HEAD_DIM), BF16), jax.ShapeDtypeStruct((HEAD_DIM, bsz * ng), BF16)],
        compiler_params=_cparams("parallel"),
        name="nsa_compress",
    )(kc2, vc2, wk, wv, pek, pev)


def _nsa_body(nq_ref, misc_ref, kcmp_ref, vcmpt_ref, ovl_ref, exp_ref, ks_ref, vst_ref, kw_ref, vwt_ref,
              o_ref, sel_ref, acc_ref, m_ref, l_ref, *, n_sel, n_cmp):
    rows = KEY_CHUNK
    n = pl.program_id(1)
    q0 = n * Q_TILE
    tq = q0 + lax.broadcasted_iota(I32, (1, Q_TILE), 1)
    qs = _stack_heads(nq_ref[...], NSA_HEADS, HEAD_DIM) * jnp.asarray(HEAD_DIM ** -0.5, BF16)
    gate_t = jax.nn.sigmoid(misc_ref[...]).T

    def gate_row(branch):
        return jnp.concatenate([gate_t[MISC_NG + 3 * h + branch:MISC_NG + 3 * h + branch + 1, :]
                                for h in range(NSA_HEADS)], axis=1)

    ng = kcmp_ref.shape[0]
    jrow = lax.broadcasted_iota(I32, (ng, Q_TILE), 0)
    ok_c = (jrow * CMP_STRIDE + (CMP_BLOCK - 1) <= tq) & (jrow < n_cmp)
    bias_c = jnp.where(ok_c, 0.0, -jnp.inf)
    s = lax.dot_general(kcmp_ref[...], qs, _NT, preferred_element_type=F32)
    s = s + jnp.concatenate([bias_c] * NSA_HEADS, axis=1)
    mx = jnp.maximum(jnp.max(s, axis=0, keepdims=True), NEG_BIG)
    e = jnp.exp(s - mx)
    pc = (e / jnp.maximum(jnp.sum(e, axis=0, keepdims=True), 1e-30)).astype(BF16)
    out = gate_row(0) * jnp.dot(vcmpt_ref[...], pc, preferred_element_type=F32)

    imp8 = jnp.dot(ovl_ref[...], pc, preferred_element_type=F32)
    imp = imp8[:, 0:Q_TILE]
    for h in range(1, NSA_HEADS):
        imp = imp + imp8[:, h * Q_TILE:(h + 1) * Q_TILE]
    n_blk = imp.shape[0]
    blk = lax.broadcasted_iota(I32, (n_blk, Q_TILE), 0)
    cur = tq // SEL_BLOCK
    forced = (blk == 0) | (blk == cur) | (blk == cur - 1)
    imp = jnp.where(blk <= cur, jnp.where(forced, FORCED_SCORE, imp), -jnp.inf)
    chosen = jnp.zeros((n_blk, Q_TILE), F32)
    for _ in range(n_sel):
        top = jnp.max(imp, axis=0, keepdims=True)
        first = jnp.min(jnp.where(imp == top, blk, n_blk), axis=0, keepdims=True)
        pick = (blk == first) & (top > -jnp.inf)
        chosen = jnp.where(pick, 1.0, chosen)
        imp = jnp.where(pick, -jnp.inf, imp)
    sel_ref[...] = chosen.astype(BF16)

    def sel_mask(r0, carry):
        hit = jnp.dot(exp_ref[pl.ds(r0, rows), :], sel_ref[...], preferred_element_type=F32)
        kpos = r0 + lax.broadcasted_iota(I32, (rows, Q_TILE), 0)
        return jnp.where((hit > 0.5) & (kpos <= tq), 0.0, -jnp.inf), carry

    nch = (q0 + Q_TILE + rows - 1) // rows
    o_s = _flash_chunks(qs, ks_ref, vst_ref, 0, nch, rows, sel_mask, 0, acc_ref, m_ref, l_ref, NSA_HEADS)
    out = out + gate_row(1) * o_s

    band = WINDOW + Q_TILE
    w0 = pl.multiple_of(jnp.maximum(q0 - WINDOW, 0), Q_TILE)
    kpos = w0 + lax.broadcasted_iota(I32, (band, Q_TILE), 0)
    bias_w = jnp.where((kpos <= tq) & (kpos > tq - WINDOW), 0.0, -jnp.inf)
    sw = lax.dot_general(kw_ref[pl.ds(w0, band), :], qs, _NT, preferred_element_type=F32)
    sw = sw + jnp.concatenate([bias_w] * NSA_HEADS, axis=1)
    ew = jnp.exp(sw - jnp.max(sw, axis=0, keepdims=True))
    o_w = jnp.dot(vwt_ref[:, pl.ds(w0, band)], ew.astype(BF16), preferred_element_type=F32)
    out = out + gate_row(2) * (o_w / jnp.sum(ew, axis=0, keepdims=True))
    o_ref[...] = _unstack_heads_t(out, NSA_HEADS).astype(o_ref.dtype)


def _nsa_attention(c, kcmp, vcmpt, bsz, seq):
    t = bsz * seq
    nq = seq // Q_TILE
    ng = seq // CMP_STRIDE
    n_cmp = (seq - CMP_BLOCK) // CMP_STRIDE + 1
    n_blk = seq // SEL_BLOCK
    n_sel = min(N_SEL_BLOCKS, n_blk)
    j = np.arange(ng)[None, :] * CMP_STRIDE
    b0 = np.arange(n_blk)[:, None] * SEL_BLOCK
    ovl = ((j <= b0 + SEL_BLOCK - 1) & (j + CMP_BLOCK - 1 >= b0) & (np.arange(ng)[None, :] < n_cmp))
    ovl = jnp.asarray(ovl.astype(np.float32), BF16)
    expand = jnp.asarray((np.arange(seq)[:, None] // SEL_BLOCK == np.arange(n_blk)[None, :]).astype(np.float32), BF16)
    qrow = lambda w: pl.BlockSpec((Q_TILE, w), lambda b, n: (b * nq + n, 0))
    seqrow = lambda w: pl.BlockSpec((seq, w), lambda b, n: (b, 0))
    vrow = lambda r: pl.BlockSpec((HEAD_DIM, seq), lambda b, n: (r, b))
    const = lambda a: pl.BlockSpec(a.shape, lambda b, n: (0, 0))
    nqh = NSA_HEADS * Q_TILE
    return pl.pallas_call(
        functools.partial(_nsa_body, n_sel=n_sel, n_cmp=n_cmp),
        grid=(bsz, nq),
        in_specs=[qrow(512), qrow(128),
                  pl.BlockSpec((ng, HEAD_DIM), lambda b, n: (b, 0)), pl.BlockSpec((HEAD_DIM, ng), lambda b, n: (0, b)),
                  const(ovl), const(expand), seqrow(HEAD_DIM), vrow(1), seqrow(HEAD_DIM), vrow(2)],
        out_specs=qrow(512),
        out_shape=jax.ShapeDtypeStruct((t, 512), BF16),
        scratch_shapes=[pltpu.VMEM((n_blk, Q_TILE), BF16), pltpu.VMEM((HEAD_DIM, nqh), F32),
                        pltpu.VMEM((1, nqh), F32), pltpu.VMEM((1, nqh), F32)],
        compiler_params=_cparams("parallel", "arbitrary"),
        name="nsa_attention",
    )(c["nq"], c["misc"], kcmp, vcmpt, ovl, expand, c["nks"], c["vt"], c["nkw"], c["vt"])


GDN_TILE = 512


def _split_bf16(a):
    hi = a.astype(BF16)
    return hi, (a - hi.astype(F32)).astype(BF16)


def _dot3(a, b, dims=None):
    ah, al = _split_bf16(a)
    bh, bl = _split_bf16(b)
    if dims is None:
        f = lambda u, v: jnp.dot(u, v, preferred_element_type=F32)
    else:
        f = lambda u, v: lax.dot_general(u, v, dims, preferred_element_type=F32)
    return f(ah, bh) + (f(ah, bl) + f(al, bh))


def _bdot(a, b, dims=None):
    a, b = a.astype(BF16), b.astype(BF16)
    if dims is None:
        return jnp.dot(a, b, preferred_element_type=F32)
    return lax.dot_general(a, b, dims, preferred_element_type=F32)


def _gdn_body(qkv_ref, misc_ref, z_ref, convw_ref, alog_ref, dtb_ref, normg_ref, o_ref,
              state_ref, tail_ref, q_s, k_s, v_s, gc_s, beta_s):
    tt = qkv_ref.shape[0]
    ch = GDN_CHUNK
    hw = GDN_HEADS * GDN_DIM

    @pl.when(pl.program_id(1) == 0)
    def _():
        state_ref[...] = jnp.zeros(state_ref.shape, F32)
        tail_ref[...] = jnp.zeros(tail_ref.shape, F32)

    x = qkv_ref[...]
    w = convw_ref[...]
    tail = tail_ref[...]
    x8 = x[0:8, :]
    row8 = lax.broadcasted_iota(I32, (8, 1), 0)
    acc = x * w[CONV_WIDTH - 1:CONV_WIDTH, :]
    acc8 = x8 * w[CONV_WIDTH - 1:CONV_WIDTH, :]
    for k in range(1, CONV_WIDTH):
        wk = w[CONV_WIDTH - 1 - k:CONV_WIDTH - k, :]
        acc = acc + pltpu.roll(x, k, 0) * wk
        acc8 = acc8 + jnp.where(row8 < k, pltpu.roll(tail, k, 0), pltpu.roll(x8, k, 0)) * wk
    tail_ref[...] = x[tt - 8:tt, :]
    rowt = lax.broadcasted_iota(I32, (tt, 1), 0)
    acc = jnp.where(rowt < 8, jnp.concatenate([acc8, acc[8:, :]], axis=0), acc)
    xc = acc * jax.nn.sigmoid(acc)

    def l2n(a):
        return a * lax.rsqrt(jnp.sum(a * a, axis=-1, keepdims=True) + RMS_EPS)

    for h in range(GDN_HEADS):
        q_s[h] = l2n(xc[:, h * GDN_DIM:(h + 1) * GDN_DIM]) * (GDN_DIM ** -0.5)
        k_s[h] = l2n(xc[:, hw + h * GDN_DIM:hw + (h + 1) * GDN_DIM])
        v_s[h] = xc[:, 2 * hw + h * GDN_DIM:2 * hw + (h + 1) * GDN_DIM]

    misc = misc_ref[...]
    sp = misc + dtb_ref[...]
    softplus = jnp.maximum(sp, 0.0) + jnp.log(1.0 + jnp.exp(-jnp.abs(sp)))
    g = -jnp.exp(alog_ref[...]) * softplus
    beta_s[...] = jax.nn.sigmoid(misc)
    ri = lax.broadcasted_iota(I32, (tt, tt), 0)
    ci = lax.broadcasted_iota(I32, (tt, tt), 1)
    blk_tril = jnp.where((ri >= ci) & (ri // ch == ci // ch), 1.0, 0.0)
    gc_s[...] = jnp.dot(blk_tril, g, preferred_element_type=F32, precision=lax.Precision.HIGHEST)

    r64 = lax.broadcasted_iota(I32, (ch, ch), 0)
    c64 = lax.broadcasted_iota(I32, (ch, ch), 1)
    tri = r64 >= c64
    strict = r64 > c64
    norm_g = normg_ref[...]

    def chunk_step(c, _):
        r0 = pl.multiple_of(c * ch, ch)
        gcs = gc_s[pl.ds(r0, ch), :]
        gct = jnp.concatenate([gcs, jnp.zeros_like(gcs)], axis=0).T
        bts = beta_s[pl.ds(r0, ch), :]
        heads = range(GDN_HEADS)
        q = [q_s[h, pl.ds(r0, ch), :] for h in heads]
        k = [k_s[h, pl.ds(r0, ch), :] for h in heads]
        gcol = [gcs[:, MISC_GA + h:MISC_GA + h + 1] for h in heads]
        decay = [jnp.exp(jnp.where(tri, gcol[h] - gct[MISC_GA + h:MISC_GA + h + 1, 0:ch], -jnp.inf)) for h in heads]
        beta = [bts[:, MISC_GB + h:MISC_GB + h + 1] for h in heads]
        kb = [k[h] * beta[h] for h in heads]
        eg = [jnp.exp(gcol[h]) for h in heads]
        nmat = [-jnp.where(strict, _bdot(kb[h], k[h], _NT) * decay[h], 0.0) for h in heads]
        y = [jnp.concatenate([v_s[h, pl.ds(r0, ch), :] * beta[h], kb[h] * eg[h]], axis=1) for h in heads]
        for level in range(6):
            y = [y[h] + _dot3(nmat[h], y[h]) for h in heads]
            if level < 5:
                nmat = [_dot3(nmat[h], nmat[h]) for h in heads]
        attn = [jnp.where(tri, _bdot(q[h], k[h], _NT) * decay[h], 0.0) for h in heads]
        glast = [gcol[h][ch - 1:ch, :] for h in heads]
        state = [state_ref[h] for h in heads]
        v_new = [y[h][:, :GDN_DIM] - _bdot(y[h][:, GDN_DIM:], state[h]) for h in heads]
        o = [_bdot(q[h] * eg[h], state[h]) + _bdot(attn[h], v_new[h]) for h in heads]
        for h in heads:
            k_dec = k[h] * jnp.exp(glast[h] - gcol[h])
            state_ref[h] = state[h] * jnp.exp(glast[h]) + _bdot(k_dec, v_new[h], (((0,), (0,)), ((), ())))
        for h in heads:
            on = o[h] * lax.rsqrt(jnp.mean(o[h] * o[h], axis=-1, keepdims=True) + RMS_EPS) * norm_g
            zz = z_ref[pl.ds(r0, ch), h * GDN_DIM:(h + 1) * GDN_DIM]
            o_ref[pl.ds(r0, ch), h * GDN_DIM:(h + 1) * GDN_DIM] = (on * (zz * jax.nn.sigmoid(zz))).astype(o_ref.dtype)
        return 0

    lax.fori_loop(0, tt // ch, chunk_step, 0)


def _gdn_mixer(c, conv_w, a_log, dt_bias, norm_g, bsz, seq):
    t = bsz * seq
    tt = min(GDN_TILE, seq)
    nt = seq // tt
    lane_row = lambda vals, off: jnp.zeros((1, LANES), F32).at[0, off:off + GDN_HEADS].set(vals.astype(F32))
    alog = lane_row(a_log, MISC_GA)
    dtb = lane_row(dt_bias, MISC_GA)
    row = lambda w: pl.BlockSpec((tt, w), lambda b, j: (b * nt + j, 0))
    const = lambda a: pl.BlockSpec(a.shape, lambda b, j: (0, 0))
    ng = norm_g.reshape(1, GDN_DIM).astype(F32)
    hs = pltpu.VMEM((GDN_HEADS, tt, GDN_DIM), F32)
    return pl.pallas_call(
        _gdn_body,
        grid=(bsz, nt),
        in_specs=[row(GDN_QKV), row(LANES), row(GDN_HEADS * GDN_DIM), const(conv_w), const(alog), const(dtb), const(ng)],
        out_specs=row(GDN_HEADS * GDN_DIM),
        out_shape=jax.ShapeDtypeStruct((t, GDN_HEADS * GDN_DIM), BF16),
        scratch_shapes=[pltpu.VMEM((GDN_HEADS, GDN_DIM, GDN_DIM), F32), pltpu.VMEM((8, GDN_QKV), F32),
                        hs, hs, hs, pltpu.VMEM((tt, LANES), F32), pltpu.VMEM((tt, LANES), F32)],
        compiler_params=_cparams("parallel", "arbitrary"),
        name="gdn_mixer",
    )(c["gqkv"], c["misc"], c["gz"], conv_w.astype(F32), alog, dtb, ng)


ROW_TILE = 256
MERGE_TILE = 512


def _layer_norm(v, g, b):
    mu = jnp.mean(v, axis=-1, keepdims=True)
    vc = v - mu
    var = jnp.mean(vc * vc, axis=-1, keepdims=True)
    return vc * lax.rsqrt(var + LN_EPS) * g + b


def _merge_body(ya_ref, yb_ref, yc_ref, mg_ref, x_ref, wb_ref, wo_ref, g_ref, b_ref, wrt_ref,
                x1_ref, x1b_ref, xpk_ref, sct_ref, *, alpha):
    d = x_ref.shape[1]
    merged = mg_ref[:, 0:d] * jnp.dot(ya_ref[...], wb_ref[0], preferred_element_type=F32)
    merged = merged + mg_ref[:, d:2 * d] * jnp.dot(yb_ref[...], wb_ref[1], preferred_element_type=F32)
    merged = merged + mg_ref[:, 2 * d:3 * d] * jnp.dot(yc_ref[...], wb_ref[2], preferred_element_type=F32)
    y = alpha * x_ref[...] + jnp.dot(merged.astype(BF16), wo_ref[...], preferred_element_type=F32)
    x1 = _layer_norm(y, g_ref[...], b_ref[...])
    x1_ref[...] = x1
    x1b = x1.astype(BF16)
    x1b_ref[...] = x1b
    xpk_ref[...] = _pack_bf16_pairs(x1b)
    logits_t = lax.dot_general(wrt_ref[...], x1b, _NT, preferred_element_type=F32)
    sct_ref[...] = jax.nn.sigmoid(logits_t)


def _merge_out(ya, yb, yc, mg, x2d, w_branch, w_out, ln_g, ln_b, w_router, alpha):
    t, d = x2d.shape
    tm = MERGE_TILE
    row = lambda w: pl.BlockSpec((tm, w), lambda i: (i, 0))
    wb = w_branch.astype(BF16)
    wo = w_out.astype(BF16)
    wrt = w_router.T.astype(BF16)
    g = ln_g.reshape(1, d)
    b = ln_b.reshape(1, d)
    return pl.pallas_call(
        functools.partial(_merge_body, alpha=alpha),
        grid=(t // tm,),
        in_specs=[row(BRANCH_WIDTH)] * 3 + [row(3 * d), row(d),
                  pl.BlockSpec(wb.shape, lambda i: (0, 0, 0)), pl.BlockSpec(wo.shape, lambda i: (0, 0)),
                  pl.BlockSpec((1, d), lambda i: (0, 0)), pl.BlockSpec((1, d), lambda i: (0, 0)),
                  pl.BlockSpec(wrt.shape, lambda i: (0, 0))],
        out_specs=[row(d), row(d), row(d // 2), pl.BlockSpec((N_EXPERTS, tm), lambda i: (0, i))],
        out_shape=[jax.ShapeDtypeStruct((t, d), F32), jax.ShapeDtypeStruct((t, d), BF16),
                   jax.ShapeDtypeStruct((t, d // 2), jnp.uint32), jax.ShapeDtypeStruct((N_EXPERTS, t), F32)],
        compiler_params=_cparams("parallel"),
        name="merge_out_ln",
    )(ya, yb, yc, mg, x2d, wb, wo, g, b, wrt)


ROUTE_TILE = 512


def _route_body(sct_ref, bias_ref, eidx_ref, gate_ref, rank_ref, cnt_ref):
    sc = sct_ref[...]
    n = sc.shape[1]
    per = N_EXPERTS // N_GROUPS
    biased = sc + bias_ref[...]
    b3 = biased.reshape(N_GROUPS, per, n)
    sub = lax.broadcasted_iota(I32, (N_GROUPS, per, n), 1)
    m1 = jnp.max(b3, axis=1, keepdims=True)
    first = jnp.min(jnp.where(b3 == m1, sub, per), axis=1, keepdims=True)
    m2 = jnp.max(jnp.where(sub == first, -jnp.inf, b3), axis=1, keepdims=True)
    gs = (m1 + m2).reshape(N_GROUPS, n)
    gi = lax.broadcasted_iota(I32, (N_GROUPS, n), 0)
    gmask = jnp.zeros((N_GROUPS, n), F32)
    for _ in range(TOPK_GROUPS):
        top = jnp.max(gs, axis=0, keepdims=True)
        pick = gi == jnp.min(jnp.where(gs == top, gi, N_GROUPS), axis=0, keepdims=True)
        gmask = jnp.where(pick, 1.0, gmask)
        gs = jnp.where(pick, -jnp.inf, gs)
    emask = jnp.broadcast_to(gmask.reshape(N_GROUPS, 1, n), (N_GROUPS, per, n)).reshape(N_EXPERTS, n)
    cand = jnp.where(emask > 0.5, biased, -jnp.inf)
    ei = lax.broadcasted_iota(I32, (N_EXPERTS, n), 0)
    ids, gates, picks = [], [], []
    for _ in range(TOP_K):
        top = jnp.max(cand, axis=0, keepdims=True)
        idx = jnp.min(jnp.where(cand == top, ei, N_EXPERTS), axis=0, keepdims=True)
        pick = ei == idx
        ids.append(idx)
        picks.append(pick)
        gates.append(jnp.sum(jnp.where(pick, sc, 0.0), axis=0, keepdims=True))
        cand = jnp.where(pick, -jnp.inf, cand)
    gate = jnp.concatenate(gates, axis=0)
    gate = gate / jnp.sum(gate, axis=0, keepdims=True) * ROUTED_SCALE
    eidx_ref[...] = jnp.concatenate(ids, axis=0)
    gate_ref[...] = gate

    @pl.when(pl.program_id(0) == 0)
    def _():
        cnt_ref[...] = jnp.zeros(cnt_ref.shape, F32)

    sel = jnp.where(cand == -jnp.inf, 1.0, 0.0) * jnp.where(emask > 0.5, 1.0, 0.0)
    ri = lax.broadcasted_iota(I32, (n, n), 0)
    ci = lax.broadcasted_iota(I32, (n, n), 1)
    before = jnp.dot(sel.astype(BF16), jnp.where(ri < ci, 1.0, 0.0).astype(BF16), preferred_element_type=F32)
    pos = before + cnt_ref[...]
    rank_ref[...] = jnp.concatenate(
        [jnp.sum(jnp.where(pk, pos, 0.0), axis=0, keepdims=True) for pk in picks], axis=0).astype(I32)
    cnt_ref[...] = cnt_ref[...] + jnp.sum(sel, axis=1, keepdims=True)


def _route(sct, router_bias):
    e, t = sct.shape
    tn = ROUTE_TILE
    col = lambda r: pl.BlockSpec((r, tn), lambda i: (0, i))
    return pl.pallas_call(
        _route_body,
        grid=(t // tn,),
        in_specs=[col(e), pl.BlockSpec((e, 1), lambda i: (0, 0))],
        out_specs=[col(TOP_K), col(TOP_K), col(TOP_K), pl.BlockSpec((e, 1), lambda i: (0, 0))],
        out_shape=[jax.ShapeDtypeStruct((TOP_K, t), I32), jax.ShapeDtypeStruct((TOP_K, t), F32),
                   jax.ShapeDtypeStruct((TOP_K, t), I32), jax.ShapeDtypeStruct((e, 1), F32)],
        compiler_params=_cparams("arbitrary"),
        name="route_topk",
    )(sct, router_bias.reshape(e, 1).astype(F32))


U32 = jnp.uint32
DISPATCH_TILE = 512


def _pack_bf16_pairs(xb):
    n = xb.shape[1] // 2
    bits = pltpu.bitcast(xb.astype(F32), U32)
    return (bits[:, :n] >> 16) | (bits[:, n:] & jnp.uint32(0xFFFF0000))


def _unpack_bf16_pairs(packed):
    lo = pltpu.bitcast(packed << 16, F32).astype(BF16)
    hi = pltpu.bitcast(packed & jnp.uint32(0xFFFF0000), F32).astype(BF16)
    return lo, hi


def _dispatch_body(last_ref, nused_ref, dest_hbm, x_ref, xs_hbm, idx_smem, zero_ref, isem, rsem, zsem):
    i = pl.program_id(0)
    tn = DISPATCH_TILE
    n_blocks = xs_hbm.shape[0] // EXPERT_BLOCK

    @pl.when(i == 0)
    def _():
        zero_ref[...] = jnp.zeros(zero_ref.shape, zero_ref.dtype)

        def block_copy(start):
            return pltpu.make_async_copy(zero_ref, xs_hbm.at[pl.ds(pl.multiple_of(start, EXPERT_BLOCK), EXPERT_BLOCK)],
                                         zsem)

        def fill(e, _):
            @pl.when(last_ref[e] >= 0)
            def _():
                block_copy(jnp.maximum(last_ref[e], 0)).start()
            return 0

        def drain(e, _):
            @pl.when(last_ref[e] >= 0)
            def _():
                block_copy(jnp.maximum(last_ref[e], 0)).wait()
            return 0

        def fill_tail(b, _):
            block_copy(b * EXPERT_BLOCK).start()
            return 0

        def drain_tail(b, _):
            block_copy(b * EXPERT_BLOCK).wait()
            return 0

        lax.fori_loop(0, N_EXPERTS, fill, 0)
        lax.fori_loop(nused_ref[0], n_blocks, fill_tail, 0)
        lax.fori_loop(0, N_EXPERTS, drain, 0)
        lax.fori_loop(nused_ref[0], n_blocks, drain_tail, 0)

    idx_cp = pltpu.make_async_copy(dest_hbm.at[i], idx_smem, isem)
    idx_cp.start()
    idx_cp.wait()

    def issue(j8, _):
        base = pl.multiple_of(j8 * 8, 8)
        for jj in range(8):
            src = x_ref.at[pl.ds(base + jj, 1)]
            for k in range(TOP_K):
                pltpu.make_async_copy(src, xs_hbm.at[pl.ds(idx_smem[k * tn + base + jj], 1)],
                                      rsem).start(priority=k % 2)
        return 0

    lax.fori_loop(0, tn // 8, issue, 0)
    for k in range(TOP_K):
        pltpu.make_async_copy(x_ref, xs_hbm.at[pl.ds(0, tn)], rsem).wait()


def _dispatch(xpk, dest_tiles, last_blk, n_used, n_pad):
    t, w = xpk.shape
    tn = DISPATCH_TILE
    anyspec = pl.BlockSpec(memory_space=pl.ANY)
    grid_spec = pltpu.PrefetchScalarGridSpec(
        num_scalar_prefetch=2,
        grid=(t // tn,),
        in_specs=[anyspec, pl.BlockSpec((tn, w), lambda i, last, nu: (i, 0))],
        out_specs=anyspec,
        scratch_shapes=[pltpu.SMEM((TOP_K * tn,), I32), pltpu.VMEM((EXPERT_BLOCK, w), xpk.dtype),
                        pltpu.SemaphoreType.DMA, pltpu.SemaphoreType.DMA, pltpu.SemaphoreType.DMA],
    )
    return pl.pallas_call(
        _dispatch_body,
        grid_spec=grid_spec,
        out_shape=jax.ShapeDtypeStruct((n_pad, w), xpk.dtype),
        compiler_params=_cparams("arbitrary"),
        name="moe_dispatch",
    )(last_blk, n_used, dest_tiles, xpk)


def _expert_body(blk_e_ref, nused_ref, xb_ref, wgu_ref, wd_ref, o_ref):
    i = pl.program_id(0)
    half = xb_ref.shape[1]

    @pl.when(i < nused_ref[0])
    def _():
        lo, hi = _unpack_bf16_pairs(xb_ref[...])
        gu = (jnp.dot(lo, wgu_ref[0, :half, :], preferred_element_type=F32)
              + jnp.dot(hi, wgu_ref[0, half:, :], preferred_element_type=F32))
        gg, uu = gu[:, :D_EXPERT], gu[:, D_EXPERT:]
        hmid = (gg * jax.nn.sigmoid(gg) * uu).astype(BF16)
        o_ref[...] = jnp.dot(hmid, wd_ref[0], preferred_element_type=F32)

    @pl.when(i >= nused_ref[0])
    def _():
        o_ref[...] = jnp.zeros(o_ref.shape, o_ref.dtype)


def _expert_ffn(xs, blk_e, n_used, wgu, wd):
    n_pad, half = xs.shape
    d = 2 * half
    nb = n_pad // EXPERT_BLOCK
    grid_spec = pltpu.PrefetchScalarGridSpec(
        num_scalar_prefetch=2,
        grid=(nb,),
        in_specs=[pl.BlockSpec((EXPERT_BLOCK, half), lambda i, be, nu: (jnp.minimum(i, nu[0] - 1), 0)),
                  pl.BlockSpec((1, d, 2 * D_EXPERT), lambda i, be, nu: (be[i], 0, 0)),
                  pl.BlockSpec((1, D_EXPERT, d), lambda i, be, nu: (be[i], 0, 0))],
        out_specs=pl.BlockSpec((EXPERT_BLOCK, d), lambda i, be, nu: (i, 0)),
    )
    return pl.pallas_call(
        _expert_body,
        grid_spec=grid_spec,
        out_shape=jax.ShapeDtypeStruct((n_pad, d), F32),
        compiler_params=_cparams("arbitrary"),
        name="expert_ffn",
    )(blk_e, n_used, xs, wgu, wd)


def _dest_body(eidx_ref, rank_ref, start_ref, dest_ref):
    n = eidx_ref.shape[1]
    ei = lax.broadcasted_iota(I32, (N_EXPERTS, n), 0)
    start = start_ref[...]
    rows = [jnp.sum(jnp.where(ei == eidx_ref[k:k + 1, :], start, 0.0), axis=0, keepdims=True)
            for k in range(TOP_K)]
    dest_ref[...] = rank_ref[...] + jnp.concatenate(rows, axis=0).astype(I32)


def _moe_plan(eidx_t, rank_t, counts, n_tok):
    n_rows = n_tok * TOP_K
    n_blocks = -(-n_rows // EXPERT_BLOCK) + N_EXPERTS
    cnt = counts.reshape(-1).astype(I32)
    padded = (cnt + EXPERT_BLOCK - 1) // EXPERT_BLOCK * EXPERT_BLOCK
    pad_end = jnp.cumsum(padded)
    pad_start = pad_end - padded
    tn = ROUTE_TILE
    col = pl.BlockSpec((TOP_K, tn), lambda i: (0, i))
    dest = pl.pallas_call(
        _dest_body,
        grid=(n_tok // tn,),
        in_specs=[col, col, pl.BlockSpec((N_EXPERTS, 1), lambda i: (0, 0))],
        out_specs=col,
        out_shape=jax.ShapeDtypeStruct((TOP_K, n_tok), I32),
        compiler_params=_cparams("parallel"),
        name="route_dest",
    )(eidx_t, rank_t, pad_start.astype(F32).reshape(N_EXPERTS, 1))
    blk_first = jnp.arange(n_blocks, dtype=I32) * EXPERT_BLOCK
    blk_e = jnp.minimum(jnp.sum((pad_end[None, :] <= blk_first[:, None]).astype(I32), axis=1), N_EXPERTS - 1)
    n_used = (pad_end[-1] // EXPERT_BLOCK).astype(I32).reshape(1)
    last_blk = jnp.where(padded > 0, pad_end - EXPERT_BLOCK, -1).astype(I32)
    return dest, blk_e, n_used, last_blk, n_blocks * EXPERT_BLOCK


def _tile_indices(dest, tn):
    k, t = dest.shape
    return dest.reshape(k, t // tn, tn).transpose(1, 0, 2).reshape(t // tn, k * tn)


def _ffn_out_body(dest_hbm, ye_hbm, gate_ref, x1_ref, x1b_ref, p_ref, wsgu_ref, wsd_ref, wpg_ref, wpp_ref,
                  g_ref, b_ref, o_ref, idx0_smem, idx1_smem, rows_ref, isem, rsem, *, alpha):
    idx_smem = (idx0_smem, idx1_smem)
    i = pl.program_id(0)
    n_tiles = pl.num_programs(0)
    tm = x1_ref.shape[0]

    def gather_tile(tile, s):
        idx = idx_smem[s]
        idx_cp = pltpu.make_async_copy(dest_hbm.at[tile], idx, isem)
        idx_cp.start()
        idx_cp.wait()

        def issue(j8, _):
            base = pl.multiple_of(j8 * 8, 8)
            for jj in range(8):
                for k in range(TOP_K):
                    pltpu.make_async_copy(ye_hbm.at[pl.ds(idx[k * tm + base + jj], 1)],
                                          rows_ref.at[s, k, pl.ds(base + jj, 1)],
                                          rsem.at[s]).start(priority=k % 2)
            return 0

        lax.fori_loop(0, tm // 8, issue, 0)

    def run(slot):
        @pl.when(i + 1 < n_tiles)
        def _():
            gather_tile(i + 1, 1 - slot)

        xb = x1b_ref[...]
        su = jnp.dot(xb, wsgu_ref[...], preferred_element_type=F32)
        sg, uu = su[:, :D_SHARED], su[:, D_SHARED:]
        shared = jnp.dot((sg * jax.nn.sigmoid(sg) * uu).astype(BF16), wsd_ref[...], preferred_element_type=F32)
        pg = jax.nn.sigmoid(jnp.dot(xb, wpg_ref[...], preferred_element_type=F32))
        ple = pg * jnp.dot(p_ref[...].astype(BF16), wpp_ref[...], preferred_element_type=F32)

        g8 = gate_ref[...]
        gate_cols = jnp.concatenate([g8, jnp.zeros((LANES - TOP_K, tm), F32)], axis=0).T
        for k in range(TOP_K):
            pltpu.make_async_copy(ye_hbm.at[pl.ds(0, tm)], rows_ref.at[slot, k], rsem.at[slot]).wait()
        routed = gate_cols[:, 0:1] * rows_ref[slot, 0]
        for k in range(1, TOP_K):
            routed = routed + gate_cols[:, k:k + 1] * rows_ref[slot, k]
        y = alpha * x1_ref[...] + (routed + shared) + ple
        o_ref[...] = _layer_norm(y, g_ref[...], b_ref[...])

    @pl.when(i == 0)
    def _():
        gather_tile(0, 0)

    @pl.when(i % 2 == 0)
    def _():
        run(0)

    @pl.when(i % 2 == 1)
    def _():
        run(1)


def _ffn_out(dest_tiles, ye, gate_t, x1, x1b, p2d, w_sgu, w_sd, w_pg, w_pp, ln_g, ln_b, alpha):
    t, d = x1.shape
    tm = ROW_TILE
    row = lambda w: pl.BlockSpec((tm, w), lambda i: (i, 0))
    const = lambda a: pl.BlockSpec(a.shape, lambda i: (0, 0))
    anyspec = pl.BlockSpec(memory_space=pl.ANY)
    ws = [w.astype(BF16) for w in (w_sgu, w_sd, w_pg, w_pp)]
    g = ln_g.reshape(1, d)
    b = ln_b.reshape(1, d)
    return pl.pallas_call(
        functools.partial(_ffn_out_body, alpha=alpha),
        grid=(t // tm,),
        in_specs=[anyspec, anyspec, pl.BlockSpec((TOP_K, tm), lambda i: (0, i)), row(d), row(d), row(PLE_DIM)]
                 + [const(w) for w in ws] + [const(g), const(b)],
        out_specs=row(d),
        out_shape=jax.ShapeDtypeStruct((t, d), F32),
        scratch_shapes=[pltpu.SMEM((TOP_K * tm,), I32), pltpu.SMEM((TOP_K * tm,), I32),
                        pltpu.VMEM((2, TOP_K, tm, d), F32),
                        pltpu.SemaphoreType.DMA, pltpu.SemaphoreType.DMA((2,))],
        compiler_params=_cparams("arbitrary"),
        name="ffn_out_ln",
    )(dest_tiles, ye, gate_t, x1, x1b, p2d, *ws, g, b)


def kernel(x, p, w_in, w_dsa_uv, w_cmp_k, pe_cmp_k, w_cmp_v, pe_cmp_v, conv_w, a_log, dt_bias, gdn_norm, w_branch, w_out, ln1_g, ln1_b, w_router, router_bias, w_gate_up, w_down, w_sh_gate_up, w_sh_down, w_ple_proj, w_ple_gate, ln2_g, ln2_b):
    bsz, seq, d = x.shape
    depth = w_in.shape[0]
    alpha = (2 * depth) ** 0.25
    t = bsz * seq
    tabs64, _ = _rope_lane_tables(seq, HEAD_DIM)
    tabs32, _ = _rope_lane_tables(seq, IDX_DIM)
    x2d = x.reshape(t, d)
    for i in range(depth):
        c = _in_projection(x2d, _split_w_in(w_in[i]), tabs64, tabs32, seq)
        ya = _dsa_attention(c, w_dsa_uv[i], bsz, seq)
        kcmp, vcmpt = _nsa_compress(c, w_cmp_k[i], pe_cmp_k[i], w_cmp_v[i], pe_cmp_v[i], bsz, seq)
        yb = _nsa_attention(c, kcmp, vcmpt, bsz, seq)
        yc = _gdn_mixer(c, conv_w[i], a_log[i], dt_bias[i], gdn_norm[i], bsz, seq)
        x1, x1b, xpk, sct = _merge_out(ya, yb, yc, c["mg"], x2d, w_branch[i], w_out[i], ln1_g[i], ln1_b[i],
                                       w_router[i], alpha)
        eidx_t, gate_t, rank_t, counts = _route(sct, router_bias[i])
        dest, blk_e, n_used, last_blk, n_pad = _moe_plan(eidx_t, rank_t, counts, t)
        xs = _dispatch(xpk, _tile_indices(dest, DISPATCH_TILE), last_blk, n_used, n_pad)
        ye = _expert_ffn(xs, blk_e, n_used, w_gate_up[i].astype(BF16), w_down[i].astype(BF16))
        x2d = _ffn_out(_tile_indices(dest, ROW_TILE), ye, gate_t, x1, x1b, p[i].reshape(t, PLE_DIM),
                       w_sh_gate_up[i], w_sh_down[i], w_ple_gate[i], w_ple_proj[i], ln2_g[i], ln2_b[i], alpha)
    return x2d.reshape(bsz, seq, d)
```
